```python
import jax, jax.numpy as jnp
from jax import lax
import numpy as np

D_MODEL = 1024
BATCH = 8
SEQ = 2048
DEPTH = 1
DEC_BATCH = 16
DEC_SEQ = 16
PAST_LEN = 2048

CHUNK = 64
RET_HEADS = 4
RET_DK = 128
RET_DV = 128
RET_THETA = 10000.0
RET_W = RET_HEADS * RET_DV
SWA_HEADS = 8
SWA_KV_HEADS = 2
SWA_HD = 64
SWA_REP = SWA_HEADS // SWA_KV_HEADS
SWA_WINDOW = 128
SWA_NB = SWA_WINDOW // CHUNK
SWA_ROT_DIM = SWA_HD // 4
SWA_THETA = 500000.0
SWA_W = SWA_HEADS * SWA_HD
MIX_W = RET_W + SWA_W
D_FF = 4 * D_MODEL
PROJ_SPLITS = (RET_HEADS * RET_DK, RET_HEADS * RET_DK, RET_W, RET_W, SWA_W, SWA_KV_HEADS * SWA_HD, SWA_KV_HEADS * SWA_HD)
PROJ_OFFSETS = tuple(int(o) for o in np.cumsum(PROJ_SPLITS)[:-1])
PROJ_W = int(sum(PROJ_SPLITS))
ALPHA = (2.0 * DEPTH) ** 0.25
BETA = (8.0 * DEPTH) ** -0.25
LN_EPS = 1e-5
GN_EPS = 1e-5
NEG_INF = -1e30

kernel_name = "hybrid_retention_swa_streaming_step"


def rope(x, pos, rot_dim, theta):
    half = rot_dim // 2
    inv = theta ** (-jnp.arange(half, dtype=jnp.float32) / half)
    ang = pos.astype(jnp.float32)[:, None] * inv[None, :]
    cos = jnp.cos(ang)[:, None, :]
    sin = jnp.sin(ang)[:, None, :]
    xr = x[..., :rot_dim].astype(jnp.float32)
    x1, x2 = xr[..., :half], xr[..., half:]
    rot = jnp.concatenate([x1 * cos - x2 * sin, x2 * cos + x1 * sin], axis=-1).astype(x.dtype)
    return jnp.concatenate([rot, x[..., rot_dim:]], axis=-1)


def layer_norm(x, w, b):
    xf = x.astype(jnp.float32)
    mu = xf.mean(-1, keepdims=True)
    var = jnp.square(xf - mu).mean(-1, keepdims=True)
    return ((xf - mu) * lax.rsqrt(var + LN_EPS) * w + b).astype(x.dtype)


def in_proj(x, pos, w_in):
    b, t, _ = x.shape
    p = jnp.einsum('btd,de->bte', x, w_in)
    rq, rk, rv, rg, sq, sk, sv = jnp.split(p, PROJ_OFFSETS, axis=-1)
    rq = rope(rq.reshape(b, t, RET_HEADS, RET_DK), pos, RET_DK, RET_THETA)
    rk = rope(rk.reshape(b, t, RET_HEADS, RET_DK) * (RET_DK ** -0.5), pos, RET_DK, RET_THETA)
    rv = rv.reshape(b, t, RET_HEADS, RET_DV)
    sq = rope(sq.reshape(b, t, SWA_HEADS, SWA_HD), pos, SWA_ROT_DIM, SWA_THETA)
    sk = rope(sk.reshape(b, t, SWA_KV_HEADS, SWA_HD), pos, SWA_ROT_DIM, SWA_THETA)
    sv = sv.reshape(b, t, SWA_KV_HEADS, SWA_HD)
    return rq, rk, rv, rg, sq, sk, sv


def ret_log_gamma():
    return jnp.log1p(-(2.0 ** (-5.0 - jnp.arange(RET_HEADS, dtype=jnp.float32))))


def retention_chunk(s, q, k, v):
    t = q.shape[1]
    lg = ret_log_gamma()
    idx = jnp.arange(t, dtype=jnp.float32)
    diff = idx[:, None] - idx[None, :]
    dmask = jnp.where(diff >= 0, jnp.exp(lg[:, None, None] * jnp.maximum(diff, 0.0)), 0.0)
    scores = jnp.einsum('bnhd,bmhd->bhnm', q, k) * dmask
    o = jnp.einsum('bhnm,bmhe->bnhe', scores, v)
    q_decay = jnp.exp(lg[None, :] * (idx[:, None] + 1.0))
    o = o + jnp.einsum('bnhd,bhde->bnhe', q, s) * q_decay[None, :, :, None]
    k_decay = jnp.exp(lg[None, :] * (t - 1.0 - idx[:, None]))
    s_new = jnp.exp(lg * t)[None, :, None, None] * s + jnp.einsum('bmhd,bmhe->bhde', k * k_decay[None, :, :, None], v)
    return o, s_new


def retention_prompt(q, k, v):
    b, L = q.shape[:2]
    nc = L // CHUNK

    def to_chunks(a):
        return a.astype(jnp.float32).reshape(b, nc, CHUNK, RET_HEADS, -1).transpose(1, 0, 2, 3, 4)

    def step(s, qkv):
        o, s_new = retention_chunk(s, *qkv)
        return s_new, o

    s0 = jnp.zeros((b, RET_HEADS, RET_DK, RET_DV), jnp.float32)
    s_fin, o = lax.scan(step, s0, (to_chunks(q), to_chunks(k), to_chunks(v)))
    o = o.transpose(1, 0, 2, 3, 4).reshape(b, L, RET_HEADS, RET_DV)
    return o, s_fin


def retention_out(o, g, gn_w):
    b, t = o.shape[:2]
    mu = o.mean(-1, keepdims=True)
    var = jnp.square(o - mu).mean(-1, keepdims=True)
    on = ((o - mu) * lax.rsqrt(var + GN_EPS)).reshape(b, t, RET_W) * gn_w
    return (jax.nn.silu(g.astype(jnp.float32)) * on).astype(g.dtype)


def sink_softmax(s, sinks):
    snk = jnp.broadcast_to(sinks.astype(jnp.float32).reshape(SWA_KV_HEADS, SWA_REP, 1, 1), s.shape[:-1] + (1,))
    p = jax.nn.softmax(jnp.concatenate([s, snk], axis=-1), axis=-1)
    return p[..., :-1]


def swa_prompt(q, k, v, sinks):
    b, L = q.shape[:2]
    nc = L // CHUNK
    qc = q.reshape(b, nc, CHUNK, SWA_KV_HEADS, SWA_REP, SWA_HD)

    def band(a):
        ap = jnp.pad(a, ((0, 0), (SWA_WINDOW, 0), (0, 0), (0, 0))).reshape(b, nc + SWA_NB, CHUNK, SWA_KV_HEADS, SWA_HD)
        return jnp.concatenate([ap[:, j:j + nc] for j in range(SWA_NB + 1)], axis=2)

    kb, vb = band(k), band(v)
    kb_len = (SWA_NB + 1) * CHUNK
    kpos = jnp.arange(nc)[:, None] * CHUNK - SWA_WINDOW + jnp.arange(kb_len)[None, :]
    valid = kpos >= 0
    s = jnp.einsum('bcqgrd,bckgd->bcgrqk', qc, kb).astype(jnp.float32) * (SWA_HD ** -0.5)
    s = jnp.where(valid[None, :, None, None, None, :], s, NEG_INF)
    p = sink_softmax(s, sinks)
    o = jnp.einsum('bcgrqk,bckgd->bcqgrd', p.astype(v.dtype), vb)
    return o.reshape(b, L, SWA_W)


def swa_sample(q, k_new, v_new, cache_k, cache_v, sinks):
    b, t = q.shape[:2]
    kk = jnp.concatenate([cache_k.astype(k_new.dtype), k_new], axis=1)
    vv = jnp.concatenate([cache_v.astype(v_new.dtype), v_new], axis=1)
    qg = q.reshape(b, t, SWA_KV_HEADS, SWA_REP, SWA_HD)
    s = jnp.einsum('btgrd,bkgd->bgrtk', qg, kk).astype(jnp.float32) * (SWA_HD ** -0.5)
    p = sink_softmax(s, sinks)
    o = jnp.einsum('bgrtk,bkgd->btgrd', p.astype(vv.dtype), vv)
    return o.reshape(b, t, SWA_W)


def post_block(x, mix, w_out, ln1_w, ln1_b, w_up, w_down, ln2_w, ln2_b):
    x = layer_norm(ALPHA * x + jnp.einsum('btm,md->btd', mix, w_out), ln1_w, ln1_b)
    h = jnp.square(jax.nn.relu(jnp.einsum('btd,df->btf', x, w_up)))
    return layer_norm(ALPHA * x + jnp.einsum('btf,fd->btd', h, w_down), ln2_w, ln2_b)


def setup_inputs(seed: int = 0) -> dict:
    key = jax.random.key(seed)
    ks = jax.random.split(key, 16)
    win = min(SWA_WINDOW, PAST_LEN)
    f32 = jnp.float32
    return {
        "x_prompt": jax.random.normal(ks[0], (BATCH, SEQ, D_MODEL), f32),
        "x_sample": jax.random.normal(ks[1], (DEC_BATCH, DEC_SEQ, D_MODEL), f32),
        "cache_swa_k": jax.random.normal(ks[2], (DEPTH, DEC_BATCH, win, SWA_KV_HEADS, SWA_HD), f32),
        "cache_swa_v": jax.random.normal(ks[3], (DEPTH, DEC_BATCH, win, SWA_KV_HEADS, SWA_HD), f32),
        "state_ret": 0.5 * jax.random.normal(ks[4], (DEPTH, DEC_BATCH, RET_HEADS, RET_DK, RET_DV), f32),
        "w_in": jax.random.normal(ks[5], (DEPTH, D_MODEL, PROJ_W), f32) * D_MODEL ** -0.5,
        "ret_gn_w": 1.0 + 0.02 * jax.random.normal(ks[6], (DEPTH, RET_W), f32),
        "swa_sinks": 0.5 * jax.random.normal(ks[7], (DEPTH, SWA_HEADS), f32),
        "w_out": jax.random.normal(ks[8], (DEPTH, MIX_W, D_MODEL), f32) * (MIX_W ** -0.5) * BETA,
        "ln1_w": 1.0 + 0.02 * jax.random.normal(ks[9], (DEPTH, D_MODEL), f32),
        "ln1_b": 0.02 * jax.random.normal(ks[10], (DEPTH, D_MODEL), f32),
        "w_up": jax.random.normal(ks[11], (DEPTH, D_MODEL, D_FF), f32) * D_MODEL ** -0.5,
        "w_down": jax.random.normal(ks[12], (DEPTH, D_FF, D_MODEL), f32) * (D_FF ** -0.5) * BETA,
        "ln2_w": 1.0 + 0.02 * jax.random.normal(ks[13], (DEPTH, D_MODEL), f32),
        "ln2_b": 0.02 * jax.random.normal(ks[14], (DEPTH, D_MODEL), f32),
    }


def reference(x_prompt, x_sample, cache_swa_k, cache_swa_v, state_ret, w_in, ret_gn_w, swa_sinks,
              w_out, ln1_w, ln1_b, w_up, w_down, ln2_w, ln2_b):
    pos_p = jnp.arange(x_prompt.shape[1])
    pos_s = PAST_LEN + jnp.arange(x_sample.shape[1])
    xp, xs = x_prompt, x_sample
    k_p, v_p, r_p, k_s, v_s, r_s = [], [], [], [], [], []
    for l in range(DEPTH):
        rq, rk, rv, rg, sq, sk, sv = in_proj(xp, pos_p, w_in[l])
        o_ret, s_fin = retention_prompt(rq, rk, rv)
        mix = jnp.concatenate([retention_out(o_ret, rg, ret_gn_w[l]), swa_prompt(sq, sk, sv, swa_sinks[l])], axis=-1)
        xp_new = post_block(xp, mix, w_out[l], ln1_w[l], ln1_b[l], w_up[l], w_down[l], ln2_w[l], ln2_b[l])
        k_p.append(sk[:, -SWA_WINDOW:])
        v_p.append(sv[:, -SWA_WINDOW:])
        r_p.append(s_fin.astype(xp.dtype))
        xp = xp_new
        rq, rk, rv, rg, sq, sk, sv = in_proj(xs, pos_s, w_in[l])
        o_ret, s_new = retention_chunk(state_ret[l].astype(jnp.float32), rq.astype(jnp.float32),
                                       rk.astype(jnp.float32), rv.astype(jnp.float32))
        mix = jnp.concatenate([retention_out(o_ret, rg, ret_gn_w[l]),
                               swa_sample(sq, sk, sv, cache_swa_k[l], cache_swa_v[l], swa_sinks[l])], axis=-1)
        xs_new = post_block(xs, mix, w_out[l], ln1_w[l], ln1_b[l], w_up[l], w_down[l], ln2_w[l], ln2_b[l])
        k_s.append(sk)
        v_s.append(sv)
        r_s.append(s_new.astype(state_ret.dtype))
        xs = xs_new
    return (xp, xs, jnp.stack(k_p), jnp.stack(v_p), jnp.stack(r_p), jnp.stack(k_s), jnp.stack(v_s), jnp.stack(r_s))
```

```python
import functools

import jax
import jax.numpy as jnp
from jax import lax
from jax.experimental import pallas as pl
from jax.experimental.pallas import tpu as pltpu

D_MODEL = 1024
DEPTH = 1
PAST_LEN = 2048
CHUNK = 64
RET_HEADS = 4
RET_DK = 128
RET_DV = 128
RET_THETA = 10000.0
RET_W = RET_HEADS * RET_DV
SWA_HEADS = 8
SWA_KV_HEADS = 2
SWA_HD = 64
SWA_WINDOW = 128
SWA_ROT_DIM = SWA_HD // 4
SWA_THETA = 500000.0
SWA_W = SWA_HEADS * SWA_HD
KV_W = SWA_KV_HEADS * SWA_HD
MIX_W = RET_W + SWA_W
D_FF = 4 * D_MODEL
OFF_RQ, OFF_RK, OFF_RV, OFF_RG = 0, RET_W, 2 * RET_W, 3 * RET_W
OFF_SQ = 4 * RET_W
OFF_SK = OFF_SQ + SWA_W
OFF_SV = OFF_SK + KV_W
PROJ_W = OFF_SV + KV_W
ALPHA = (2.0 * DEPTH) ** 0.25
LN_EPS = 1e-5
GN_EPS = 1e-5
NEG_INF = -1e30

LANES = 128
N_PAIRS = SWA_W // LANES
T_RQC, T_RQS, T_RKC, T_RKS, T_SQC, T_SQ1, T_SQ2, T_SKC, T_SK1, T_SK2 = range(10)
N_TABS = 10

PROMPT_TILE = 256
MLP_TILE = 512
VMEM_LIMIT = 56 * 1024 * 1024

F32 = jnp.float32
BF16 = jnp.bfloat16


def _tab(tab_ref, i):
    return tab_ref[:, i * LANES:(i + 1) * LANES]


def _dot(a, b):
    return jnp.dot(a, b, preferred_element_type=F32)


def _dot_nt(a, b):
    return lax.dot_general(a, b, (((1,), (1,)), ((), ())), preferred_element_type=F32)


def _dot_tn(a, b):
    return lax.dot_general(a, b, (((0,), (0,)), ((), ())), preferred_element_type=F32)


def _layer_norm(z, w, b):
    mu = jnp.mean(z, axis=-1, keepdims=True)
    zc = z - mu
    var = jnp.mean(zc * zc, axis=-1, keepdims=True)
    return zc * lax.rsqrt(var + LN_EPS) * w + b


def _rope_ret(x, c, s):
    return x * c + pltpu.roll(x, RET_DK // 2, axis=1) * s


def _rope_swa(x, c, s1, s2):
    half = SWA_ROT_DIM // 2
    return x * c + pltpu.roll(x, half, axis=1) * s1 + pltpu.roll(x, LANES - half, axis=1) * s2


def _retention_head(q, k, v, g, s_prev, dmask, qdec, kdec, gt, gnw):
    qb = q.astype(BF16)
    kb = k.astype(BF16)
    vb = v.astype(BF16)
    scores = _dot_nt(qb, kb) * dmask
    o = _dot(scores.astype(BF16), vb)
    o = o + _dot(qb, s_prev.astype(BF16)) * qdec
    s_new = gt * s_prev + _dot_tn((k * kdec).astype(BF16), vb)
    mu = jnp.mean(o, axis=-1, keepdims=True)
    oc = o - mu
    var = jnp.mean(oc * oc, axis=-1, keepdims=True)
    on = oc * lax.rsqrt(var + GN_EPS) * gnw
    return jax.nn.silu(g) * on, s_new


def _kv_variants(a):
    lane = lax.broadcasted_iota(jnp.int32, a.shape, 1)
    lo = lane < SWA_HD
    ar = pltpu.roll(a, SWA_HD, axis=1)
    zero = jnp.zeros_like(a)
    return (jnp.where(lo, a, zero).astype(BF16), jnp.where(lo, zero, ar).astype(BF16),
            jnp.where(lo, ar, zero).astype(BF16), jnp.where(lo, zero, a).astype(BF16))


def _swa_pair(qp, kt, vt, nk, sink_e, sink_o, valid):
    s = _dot_nt(qp, kt)
    if valid is not None:
        s = jnp.where(valid, s, NEG_INF)
    col = lax.broadcasted_iota(jnp.int32, s.shape, 1)
    is_o = col >= nk
    m_e = jnp.maximum(jnp.max(jnp.where(is_o, -jnp.inf, s), axis=-1, keepdims=True), sink_e)
    m_o = jnp.maximum(jnp.max(jnp.where(is_o, s, -jnp.inf), axis=-1, keepdims=True), sink_o)
    e = jnp.exp(s - jnp.where(is_o, m_o, m_e))
    d_e = jnp.sum(jnp.where(is_o, 0.0, e), axis=-1, keepdims=True) + jnp.exp(sink_e - m_e)
    d_o = jnp.sum(jnp.where(is_o, e, 0.0), axis=-1, keepdims=True) + jnp.exp(sink_o - m_o)
    o = _dot(e.astype(BF16), vt)
    lane = lax.broadcasted_iota(jnp.int32, o.shape, 1)
    return o / jnp.where(lane < SWA_HD, d_e, d_o)


def _prompt_mixer_kernel(sinks_ref, x_ref, tab_ref, dmask_ref, qdec_ref, kdec_ref, gt_ref, w_in_ref,
                         gnw_ref, w_out_ref, ln1w_ref, ln1b_ref,
                         x1_ref, kout_ref, vout_ref, sout_ref,
                         p_scr, s_scr, kext_scr, vext_scr, mix_scr):
    tt = PROMPT_TILE
    j = pl.program_id(1)
    last_j = pl.num_programs(1) - 1

    @pl.when(j == 0)
    def _():
        s_scr[...] = jnp.zeros_like(s_scr)
        kext_scr[0:SWA_WINDOW, :] = jnp.zeros((SWA_WINDOW, KV_W), F32)
        vext_scr[0:SWA_WINDOW, :] = jnp.zeros((SWA_WINDOW, KV_W), F32)

    x = x_ref[0]
    p_scr[...] = _dot(x.astype(BF16), w_in_ref[...])

    for h in range(RET_HEADS):
        sl = slice(h * LANES, (h + 1) * LANES)
        q = _rope_ret(p_scr[:, OFF_RQ + h * LANES:OFF_RQ + (h + 1) * LANES], _tab(tab_ref, T_RQC), _tab(tab_ref, T_RQS))
        k = _rope_ret(p_scr[:, OFF_RK + h * LANES:OFF_RK + (h + 1) * LANES], _tab(tab_ref, T_RKC), _tab(tab_ref, T_RKS))
        v = p_scr[:, OFF_RV + h * LANES:OFF_RV + (h + 1) * LANES]
        g = p_scr[:, OFF_RG + h * LANES:OFF_RG + (h + 1) * LANES]
        out, s_new = _retention_head(q, k, v, g, s_scr[h], dmask_ref[h], qdec_ref[:, sl], kdec_ref[:, sl],
                                     gt_ref[:, sl], gnw_ref[:, sl])
        s_scr[h] = s_new
        mix_scr[:, sl] = out.astype(BF16)

    k_new = _rope_swa(p_scr[:, OFF_SK:OFF_SK + KV_W], _tab(tab_ref, T_SKC), _tab(tab_ref, T_SK1), _tab(tab_ref, T_SK2))
    v_new = p_scr[:, OFF_SV:OFF_SV + KV_W]
    kext_scr[SWA_WINDOW:SWA_WINDOW + tt, :] = k_new
    vext_scr[SWA_WINDOW:SWA_WINDOW + tt, :] = v_new
    k_var = _kv_variants(kext_scr[...])
    v_var = _kv_variants(vext_scr[...])
    nk = SWA_WINDOW + CHUNK
    col = lax.broadcasted_iota(jnp.int32, (CHUNK, 2 * nk), 1)
    col = jnp.where(col >= nk, col - nk, col)
    for m in range(N_PAIRS):
        qs = p_scr[:, OFF_SQ + m * LANES:OFF_SQ + (m + 1) * LANES]
        qs = _rope_swa(qs, _tab(tab_ref, T_SQC), _tab(tab_ref, T_SQ1), _tab(tab_ref, T_SQ2)).astype(BF16)
        grp = (2 * m) // (SWA_HEADS // SWA_KV_HEADS)
        sink_e = sinks_ref[2 * m]
        sink_o = sinks_ref[2 * m + 1]
        for i in range(tt // CHUNK):
            r0 = i * CHUNK
            kt = jnp.concatenate([k_var[2 * grp][r0:r0 + nk], k_var[2 * grp + 1][r0:r0 + nk]], axis=0)
            vt = jnp.concatenate([v_var[2 * grp][r0:r0 + nk], v_var[2 * grp + 1][r0:r0 + nk]], axis=0)
            valid = None
            if i < SWA_WINDOW // CHUNK:
                valid = jnp.logical_or(col >= SWA_WINDOW - r0, j > 0)
            o = _swa_pair(qs[r0:r0 + CHUNK], kt, vt, nk, sink_e, sink_o, valid)
            mix_scr[r0:r0 + CHUNK, RET_W + m * LANES:RET_W + (m + 1) * LANES] = o.astype(BF16)

    kext_scr[0:SWA_WINDOW, :] = k_new[tt - SWA_WINDOW:, :]
    vext_scr[0:SWA_WINDOW, :] = v_new[tt - SWA_WINDOW:, :]

    z = ALPHA * x + _dot(mix_scr[...], w_out_ref[...])
    x1_ref[0] = _layer_norm(z, ln1w_ref[...], ln1b_ref[...])

    @pl.when(j == last_j)
    def _():
        kout_ref[0] = k_new[tt - SWA_WINDOW:, :]
        vout_ref[0] = v_new[tt - SWA_WINDOW:, :]
        sout_ref[0] = s_scr[...]


def _prompt_mixer(x, tab, dmask, qdec, kdec, gt, sinks, w_in, gnw, w_out, ln1w, ln1b):
    b, t, d = x.shape
    tt = PROMPT_TILE
    const2 = lambda bi, ji: (0, 0)
    const3 = lambda bi, ji: (0, 0, 0)
    return pl.pallas_call(
        _prompt_mixer_kernel,
        grid=(b, t // tt),
        in_specs=[
            pl.BlockSpec(memory_space=pltpu.SMEM),
            pl.BlockSpec((1, tt, d), lambda bi, ji: (bi, ji, 0)),
            pl.BlockSpec((tt, N_TABS * LANES), lambda bi, ji: (ji, 0)),
            pl.BlockSpec((RET_HEADS, tt, tt), const3),
            pl.BlockSpec((tt, RET_W), const2),
            pl.BlockSpec((tt, RET_W), const2),
            pl.BlockSpec((1, RET_W), const2),
            pl.BlockSpec((d, PROJ_W), const2),
            pl.BlockSpec((1, RET_W), const2),
            pl.BlockSpec((MIX_W, d), const2),
            pl.BlockSpec((1, d), const2),
            pl.BlockSpec((1, d), const2),
        ],
        out_specs=[
            pl.BlockSpec((1, tt, d), lambda bi, ji: (bi, ji, 0)),
            pl.BlockSpec((1, SWA_WINDOW, KV_W), lambda bi, ji: (bi, 0, 0)),
            pl.BlockSpec((1, SWA_WINDOW, KV_W), lambda bi, ji: (bi, 0, 0)),
            pl.BlockSpec((1, RET_HEADS, RET_DK, RET_DV), lambda bi, ji: (bi, 0, 0, 0)),
        ],
        out_shape=[
            jax.ShapeDtypeStruct((b, t, d), F32),
            jax.ShapeDtypeStruct((b, SWA_WINDOW, KV_W), F32),
            jax.ShapeDtypeStruct((b, SWA_WINDOW, KV_W), F32),
            jax.ShapeDtypeStruct((b, RET_HEADS, RET_DK, RET_DV), F32),
        ],
        scratch_shapes=[
            pltpu.VMEM((tt, PROJ_W), F32),
            pltpu.VMEM((RET_HEADS, RET_DK, RET_DV), F32),
            pltpu.VMEM((SWA_WINDOW + tt, KV_W), F32),
            pltpu.VMEM((SWA_WINDOW + tt, KV_W), F32),
            pltpu.VMEM((tt, MIX_W), BF16),
        ],
        compiler_params=pltpu.CompilerParams(
            dimension_semantics=("arbitrary", "arbitrary"), vmem_limit_bytes=VMEM_LIMIT),
        name="prompt_mixer",
    )(sinks, x, tab, dmask, qdec, kdec, gt, w_in, gnw, w_out, ln1w, ln1b)


def _sample_mixer_kernel(sinks_ref, x_ref, tab_ref, dmask_ref, qdec_ref, kdec_ref, gt_ref, ck_ref, cv_ref,
                         state_ref, w_in_ref, gnw_ref, w_out_ref, ln1w_ref, ln1b_ref,
                         x1_ref, kout_ref, vout_ref, sout_ref,
                         p_scr, mix_scr, *, n_new):
    s_idx = pl.program_id(0)
    last_s = pl.num_programs(0) - 1

    @pl.when(s_idx == 0)
    def _():
        p_scr[...] = _dot(x_ref[...].astype(BF16), w_in_ref[...])

    rows = pl.ds(pl.multiple_of(s_idx * n_new, n_new), n_new)

    for h in range(RET_HEADS):
        sl = slice(h * LANES, (h + 1) * LANES)
        q = _rope_ret(p_scr[rows, OFF_RQ + h * LANES:OFF_RQ + (h + 1) * LANES], _tab(tab_ref, T_RQC), _tab(tab_ref, T_RQS))
        k = _rope_ret(p_scr[rows, OFF_RK + h * LANES:OFF_RK + (h + 1) * LANES], _tab(tab_ref, T_RKC), _tab(tab_ref, T_RKS))
        v = p_scr[rows, OFF_RV + h * LANES:OFF_RV + (h + 1) * LANES]
        g = p_scr[rows, OFF_RG + h * LANES:OFF_RG + (h + 1) * LANES]
        out, s_new = _retention_head(q, k, v, g, state_ref[0, h], dmask_ref[h], qdec_ref[:, sl], kdec_ref[:, sl],
                                     gt_ref[:, sl], gnw_ref[:, sl])
        sout_ref[0, h] = s_new
        mix_scr[rows, sl] = out.astype(BF16)

    k_new = _rope_swa(p_scr[rows, OFF_SK:OFF_SK + KV_W], _tab(tab_ref, T_SKC), _tab(tab_ref, T_SK1), _tab(tab_ref, T_SK2))
    v_new = p_scr[rows, OFF_SV:OFF_SV + KV_W]
    kout_ref[0] = k_new
    vout_ref[0] = v_new
    k_var = _kv_variants(jnp.concatenate([ck_ref[0], k_new], axis=0))
    v_var = _kv_variants(jnp.concatenate([cv_ref[0], v_new], axis=0))
    nk = ck_ref.shape[1] + n_new
    for m in range(N_PAIRS):
        qs = p_scr[rows, OFF_SQ + m * LANES:OFF_SQ + (m + 1) * LANES]
        qs = _rope_swa(qs, _tab(tab_ref, T_SQC), _tab(tab_ref, T_SQ1), _tab(tab_ref, T_SQ2)).astype(BF16)
        grp = (2 * m) // (SWA_HEADS // SWA_KV_HEADS)
        kt = jnp.concatenate([k_var[2 * grp], k_var[2 * grp + 1]], axis=0)
        vt = jnp.concatenate([v_var[2 * grp], v_var[2 * grp + 1]], axis=0)
        o = _swa_pair(qs, kt, vt, nk, sinks_ref[2 * m], sinks_ref[2 * m + 1], None)
        mix_scr[rows, RET_W + m * LANES:RET_W + (m + 1) * LANES] = o.astype(BF16)

    @pl.when(s_idx == last_s)
    def _():
        z = ALPHA * x_ref[...] + _dot(mix_scr[...], w_out_ref[...])
        x1_ref[...] = _layer_norm(z, ln1w_ref[...], ln1b_ref[...])


def _sample_mixer(x, tab, dmask, qdec, kdec, gt, cache_k, cache_v, state, sinks, w_in, gnw, w_out, ln1w, ln1b):
    nb, n_new, d = x.shape
    rows = nb * n_new
    win = cache_k.shape[1]
    const2 = lambda si: (0, 0)
    const3 = lambda si: (0, 0, 0)
    return pl.pallas_call(
        functools.partial(_sample_mixer_kernel, n_new=n_new),
        grid=(nb,),
        in_specs=[
            pl.BlockSpec(memory_space=pltpu.SMEM),
            pl.BlockSpec((rows, d), const2),
            pl.BlockSpec((n_new, N_TABS * LANES), const2),
            pl.BlockSpec((RET_HEADS, n_new, n_new), const3),
            pl.BlockSpec((n_new, RET_W), const2),
            pl.BlockSpec((n_new, RET_W), const2),
            pl.BlockSpec((1, RET_W), const2),
            pl.BlockSpec((1, win, KV_W), lambda si: (si, 0, 0)),
            pl.BlockSpec((1, win, KV_W), lambda si: (si, 0, 0)),
            pl.BlockSpec((1, RET_HEADS, RET_DK, RET_DV), lambda si: (si, 0, 0, 0)),
            pl.BlockSpec((d, PROJ_W), const2),
            pl.BlockSpec((1, RET_W), const2),
            pl.BlockSpec((MIX_W, d), const2),
            pl.BlockSpec((1, d), const2),
            pl.BlockSpec((1, d), const2),
        ],
        out_specs=[
            pl.BlockSpec((rows, d), const2),
            pl.BlockSpec((1, n_new, KV_W), lambda si: (si, 0, 0)),
            pl.BlockSpec((1, n_new, KV_W), lambda si: (si, 0, 0)),
            pl.BlockSpec((1, RET_HEADS, RET_DK, RET_DV), lambda si: (si, 0, 0, 0)),
        ],
        out_shape=[
            jax.ShapeDtypeStruct((rows, d), F32),
            jax.ShapeDtypeStruct((nb, n_new, KV_W), F32),
            jax.ShapeDtypeStruct((nb, n_new, KV_W), F32),
            jax.ShapeDtypeStruct((nb, RET_HEADS, RET_DK, RET_DV), F32),
        ],
        scratch_shapes=[
            pltpu.VMEM((rows, PROJ_W), F32),
            pltpu.VMEM((rows, MIX_W), BF16),
        ],
        compiler_params=pltpu.CompilerParams(
            dimension_semantics=("arbitrary",), vmem_limit_bytes=VMEM_LIMIT),
        name="sample_mixer",
    )(sinks, x.reshape(rows, d), tab, dmask, qdec, kdec, gt, cache_k, cache_v, state, w_in, gnw, w_out, ln1w, ln1b)


def _mlp_kernel(x_ref, w_up_ref, w_down_ref, lnw_ref, lnb_ref, y_ref):
    x = x_ref[...]
    h = _dot(x.astype(BF16), w_up_ref[...])
    h = jnp.square(jnp.maximum(h, 0.0)).astype(BF16)
    z = ALPHA * x + _dot(h, w_down_ref[...])
    y_ref[...] = _layer_norm(z, lnw_ref[...], lnb_ref[...])


def _mlp(x, w_up, w_down, lnw, lnb, tile):
    rows, d = x.shape
    const2 = lambda i: (0, 0)
    return pl.pallas_call(
        _mlp_kernel,
        grid=(rows // tile,),
        in_specs=[
            pl.BlockSpec((tile, d), lambda i: (i, 0)),
            pl.BlockSpec((d, D_FF), const2, pipeline_mode=pl.Buffered(1)),
            pl.BlockSpec((D_FF, d), const2, pipeline_mode=pl.Buffered(1)),
            pl.BlockSpec((1, d), const2),
            pl.BlockSpec((1, d), const2),
        ],
        out_specs=pl.BlockSpec((tile, d), lambda i: (i, 0)),
        out_shape=jax.ShapeDtypeStruct((rows, d), F32),
        compiler_params=pltpu.CompilerParams(
            dimension_semantics=("arbitrary",), vmem_limit_bytes=VMEM_LIMIT),
        name="mlp",
    )(x, w_up, w_down, lnw, lnb)


def _rope_tables(pos):
    posf = pos.astype(F32)[:, None]
    half_r = RET_DK // 2
    ang = posf * (RET_THETA ** (-jnp.arange(half_r, dtype=F32) / half_r))[None, :]
    c, s = jnp.cos(ang), jnp.sin(ang)
    rc = jnp.concatenate([c, c], axis=-1)
    rs = jnp.concatenate([-s, s], axis=-1)
    k_scale = RET_DK ** -0.5
    half_s = SWA_ROT_DIM // 2
    ang = posf * (SWA_THETA ** (-jnp.arange(half_s, dtype=F32) / half_s))[None, :]
    c, s = jnp.cos(ang), jnp.sin(ang)
    t = pos.shape[0]
    rest = SWA_HD - SWA_ROT_DIM
    z = jnp.zeros((t, half_s), F32)
    sc = jnp.tile(jnp.concatenate([c, c, jnp.ones((t, rest), F32)], axis=-1), (1, LANES // SWA_HD))
    s1 = jnp.tile(jnp.concatenate([z, s, jnp.zeros((t, rest), F32)], axis=-1), (1, LANES // SWA_HD))
    s2 = jnp.tile(jnp.concatenate([-s, z, jnp.zeros((t, rest), F32)], axis=-1), (1, LANES // SWA_HD))
    q_scale = SWA_HD ** -0.5
    return jnp.concatenate([rc, rs, rc * k_scale, rs * k_scale,
                            sc * q_scale, s1 * q_scale, s2 * q_scale, sc, s1, s2], axis=-1)


def _decay_tables(t):
    lg = jnp.log1p(-(2.0 ** (-5.0 - jnp.arange(RET_HEADS, dtype=F32))))
    idx = jnp.arange(t, dtype=F32)
    diff = idx[:, None] - idx[None, :]
    dmask = jnp.where(diff >= 0, jnp.exp(lg[:, None, None] * jnp.maximum(diff, 0.0)), 0.0)
    qdec = jnp.repeat(jnp.exp(lg[None, :] * (idx[:, None] + 1.0)), RET_DV, axis=1)
    kdec = jnp.repeat(jnp.exp(lg[None, :] * (t - 1.0 - idx[:, None])), RET_DV, axis=1)
    gt = jnp.repeat(jnp.exp(lg * t)[None, :], RET_DV, axis=1)
    return dmask, qdec, kdec, gt


def kernel(x_prompt, x_sample, cache_swa_k, cache_swa_v, state_ret, w_in, ret_gn_w, swa_sinks,
           w_out, ln1_w, ln1_b, w_up, w_down, ln2_w, ln2_b):
    assert w_in.shape[0] == DEPTH == 1
    b, t, d = x_prompt.shape
    nb, n_new, _ = x_sample.shape
    win = cache_swa_k.shape[2]

    w_in_b = w_in[0].astype(BF16)
    w_out_b = w_out[0].astype(BF16)
    w_up_b = w_up[0].astype(BF16)
    w_down_b = w_down[0].astype(BF16)
    sinks = swa_sinks[0]

    tab_p = _rope_tables(jnp.arange(t))
    tab_s = _rope_tables(PAST_LEN + jnp.arange(n_new))
    dec_p = _decay_tables(PROMPT_TILE)
    dec_s = _decay_tables(n_new)

    x1_p, k_p, v_p, r_p = _prompt_mixer(x_prompt, tab_p, *dec_p, sinks, w_in_b, ret_gn_w, w_out_b, ln1_w, ln1_b)
    y_p = _mlp(x1_p.reshape(b * t, d), w_up_b, w_down_b, ln2_w, ln2_b, MLP_TILE).reshape(b, t, d)

    x1_s, k_s, v_s, r_s = _sample_mixer(
        x_sample, tab_s, *dec_s, cache_swa_k[0].reshape(nb, win, KV_W), cache_swa_v[0].reshape(nb, win, KV_W),
        state_ret[0], sinks, w_in_b, ret_gn_w, w_out_b, ln1_w, ln1_b)
    y_s = _mlp(x1_s, w_up_b, w_down_b, ln2_w, ln2_b, nb * n_new).reshape(nb, n_new, d)

    return (y_p, y_s,
            k_p.reshape(DEPTH, b, SWA_WINDOW, SWA_KV_HEADS, SWA_HD),
            v_p.reshape(DEPTH, b, SWA_WINDOW, SWA_KV_HEADS, SWA_HD),
            r_p.reshape(DEPTH, b, RET_HEADS, RET_DK, RET_DV),
            k_s.reshape(DEPTH, nb, n_new, SWA_KV_HEADS, SWA_HD),
            v_s.reshape(DEPTH, nb, n_new, SWA_KV_HEADS, SWA_HD),
            r_s.reshape(DEPTH, nb, RET_HEADS, RET_DK, RET_DV))
```

```python
import functools

import jax
import jax.numpy as jnp
from jax import lax
from jax.experimental import pallas as pl
from jax.experimental.pallas import tpu as pltpu

D_MODEL = 1024
DEPTH = 1
PAST_LEN = 2048
CHUNK = 64
RET_HEADS = 4
RET_DK = 128
RET_DV = 128
RET_THETA = 10000.0
RET_W = RET_HEADS * RET_DV
SWA_HEADS = 8
SWA_KV_HEADS = 2
SWA_HD = 64
SWA_WINDOW = 128
SWA_ROT_DIM = SWA_HD // 4
SWA_THETA = 500000.0
SWA_W = SWA_HEADS * SWA_HD
KV_W = SWA_KV_HEADS * SWA_HD
MIX_W = RET_W + SWA_W
D_FF = 4 * D_MODEL
OFF_RQ, OFF_RK, OFF_RV, OFF_RG = 0, RET_W, 2 * RET_W, 3 * RET_W
OFF_SQ = 4 * RET_W
OFF_SK = OFF_SQ + SWA_W
OFF_SV = OFF_SK + KV_W
PROJ_W = OFF_SV + KV_W
ALPHA = (2.0 * DEPTH) ** 0.25
LN_EPS = 1e-5
GN_EPS = 1e-5
NEG_INF = -1e30

LANES = 128
N_PAIRS = SWA_W // LANES
PAIRS_PER_KV = N_PAIRS // SWA_KV_HEADS
T_RQC, T_RQS, T_RKC, T_RKS, T_SQC, T_SQ1, T_SQ2, T_SKC, T_SK1, T_SK2 = range(10)
N_TABS = 10

PROMPT_TILE = 256
MLP_TILE = 512
VMEM_LIMIT = 56 * 1024 * 1024

F32 = jnp.float32
BF16 = jnp.bfloat16


def _tab(tab_ref, i):
    return tab_ref[:, i * LANES:(i + 1) * LANES]


def _dot(a, b):
    return jnp.dot(a, b, preferred_element_type=F32)


def _dot_nt(a, b):
    return lax.dot_general(a, b, (((1,), (1,)), ((), ())), preferred_element_type=F32)


def _dot_tn(a, b):
    return lax.dot_general(a, b, (((0,), (0,)), ((), ())), preferred_element_type=F32)


def _layer_norm(z, w, b):
    mu = jnp.mean(z, axis=-1, keepdims=True)
    zc = z - mu
    var = jnp.mean(zc * zc, axis=-1, keepdims=True)
    return zc * lax.rsqrt(var + LN_EPS) * w + b


def _rope_ret(x, c, s):
    return x * c + pltpu.roll(x, RET_DK // 2, axis=1) * s


def _rope_swa(x, c, s1, s2):
    half = SWA_ROT_DIM // 2
    return x * c + pltpu.roll(x, half, axis=1) * s1 + pltpu.roll(x, LANES - half, axis=1) * s2


def _retention_head(q, k, v, g, s_prev, dmask, qdec, kdec, gt, gnw):
    qb = q.astype(BF16)
    kb = k.astype(BF16)
    vb = v.astype(BF16)
    scores = _dot_nt(qb, kb) * dmask
    o = _dot(scores.astype(BF16), vb)
    o = o + _dot(qb, s_prev.astype(BF16)) * qdec
    s_new = gt * s_prev + _dot_tn((k * kdec).astype(BF16), vb)
    mu = jnp.mean(o, axis=-1, keepdims=True)
    oc = o - mu
    var = jnp.mean(oc * oc, axis=-1, keepdims=True)
    on = oc * lax.rsqrt(var + GN_EPS) * gnw
    return jax.nn.silu(g) * on, s_new


def _kv_variants(a):
    lane = lax.broadcasted_iota(jnp.int32, a.shape, 1)
    lo = lane < SWA_HD
    ar = pltpu.roll(a, SWA_HD, axis=1)
    zero = jnp.zeros_like(a)
    return (jnp.where(lo, a, zero).astype(BF16), jnp.where(lo, zero, ar).astype(BF16),
            jnp.where(lo, ar, zero).astype(BF16), jnp.where(lo, zero, a).astype(BF16))


def _swa_softmax(s, nk, sink_e, sink_o, valid):
    if valid is not None:
        s = jnp.where(valid, s, NEG_INF)
    col = lax.broadcasted_iota(jnp.int32, s.shape, 1)
    is_o = col >= nk
    m_e = jnp.maximum(jnp.max(jnp.where(is_o, -jnp.inf, s), axis=-1, keepdims=True), sink_e)
    m_o = jnp.maximum(jnp.max(jnp.where(is_o, s, -jnp.inf), axis=-1, keepdims=True), sink_o)
    e = jnp.exp(s - jnp.where(is_o, m_o, m_e))
    d_e = jnp.sum(jnp.where(is_o, 0.0, e), axis=-1, keepdims=True) + jnp.exp(sink_e - m_e)
    d_o = jnp.sum(jnp.where(is_o, e, 0.0), axis=-1, keepdims=True) + jnp.exp(sink_o - m_o)
    lane = lax.broadcasted_iota(jnp.int32, (s.shape[0], LANES), 1)
    return e.astype(BF16), jnp.where(lane < SWA_HD, d_e, d_o)


def _swa_pair(qp, kt, vt, nk, sink_e, sink_o, valid):
    e, den = _swa_softmax(_dot_nt(qp, kt), nk, sink_e, sink_o, valid)
    return _dot(e, vt) / den


def _prompt_mixer_kernel(sinks_ref, x_ref, tab_ref, dmask_ref, qdec_ref, kdec_ref, gt_ref, w_in_ref,
                         gnw_ref, w_out_ref, ln1w_ref, ln1b_ref,
                         x1_ref, kout_ref, vout_ref, sout_ref,
                         p_scr, s_scr, kext_scr, vext_scr, mix_scr, sw_s_scr, sw_e_scr, sw_d_scr):
    tt = PROMPT_TILE
    j = pl.program_id(1)
    last_j = pl.num_programs(1) - 1

    @pl.when(j == 0)
    def _():
        s_scr[...] = jnp.zeros_like(s_scr)
        kext_scr[0:SWA_WINDOW, :] = jnp.zeros((SWA_WINDOW, KV_W), F32)
        vext_scr[0:SWA_WINDOW, :] = jnp.zeros((SWA_WINDOW, KV_W), F32)

    x = x_ref[0]
    p_scr[...] = _dot(x.astype(BF16), w_in_ref[...])

    for h in range(RET_HEADS):
        sl = slice(h * LANES, (h + 1) * LANES)
        q = _rope_ret(p_scr[:, OFF_RQ + h * LANES:OFF_RQ + (h + 1) * LANES], _tab(tab_ref, T_RQC), _tab(tab_ref, T_RQS))
        k = _rope_ret(p_scr[:, OFF_RK + h * LANES:OFF_RK + (h + 1) * LANES], _tab(tab_ref, T_RKC), _tab(tab_ref, T_RKS))
        v = p_scr[:, OFF_RV + h * LANES:OFF_RV + (h + 1) * LANES]
        g = p_scr[:, OFF_RG + h * LANES:OFF_RG + (h + 1) * LANES]
        out, s_new = _retention_head(q, k, v, g, s_scr[h], dmask_ref[h], qdec_ref[:, sl], kdec_ref[:, sl],
                                     gt_ref[:, sl], gnw_ref[:, sl])
        s_scr[h] = s_new
        mix_scr[:, sl] = out.astype(BF16)

    k_new = _rope_swa(p_scr[:, OFF_SK:OFF_SK + KV_W], _tab(tab_ref, T_SKC), _tab(tab_ref, T_SK1), _tab(tab_ref, T_SK2))
    v_new = p_scr[:, OFF_SV:OFF_SV + KV_W]
    kext_scr[SWA_WINDOW:SWA_WINDOW + tt, :] = k_new
    vext_scr[SWA_WINDOW:SWA_WINDOW + tt, :] = v_new
    k_var = _kv_variants(kext_scr[...])
    v_var = _kv_variants(vext_scr[...])
    nk = SWA_WINDOW + CHUNK
    n_chunks = tt // CHUNK
    qs = [_rope_swa(p_scr[:, OFF_SQ + m * LANES:OFF_SQ + (m + 1) * LANES],
                    _tab(tab_ref, T_SQC), _tab(tab_ref, T_SQ1), _tab(tab_ref, T_SQ2)).astype(BF16)
          for m in range(N_PAIRS)]
    col = lax.broadcasted_iota(jnp.int32, (PAIRS_PER_KV * CHUNK, 2 * nk), 1)
    col = jnp.where(col >= nk, col - nk, col)
    row = lax.broadcasted_iota(jnp.int32, (PAIRS_PER_KV * CHUNK, 1), 0)
    blocks = [(grp, i) for grp in range(SWA_KV_HEADS) for i in range(n_chunks)]
    rows_of = lambda n: slice(n * PAIRS_PER_KV * CHUNK, (n + 1) * PAIRS_PER_KV * CHUNK)
    for n, (grp, i) in enumerate(blocks):
        r0 = i * CHUNK
        q_blk = jnp.concatenate([qs[grp * PAIRS_PER_KV + a][r0:r0 + CHUNK] for a in range(PAIRS_PER_KV)], axis=0)
        kt = jnp.concatenate([k_var[2 * grp][r0:r0 + nk], k_var[2 * grp + 1][r0:r0 + nk]], axis=0)
        sw_s_scr[rows_of(n), :] = _dot_nt(q_blk, kt)
    for n, (grp, i) in enumerate(blocks):
        sink_e = sinks_ref[2 * grp * PAIRS_PER_KV]
        sink_o = sinks_ref[2 * grp * PAIRS_PER_KV + 1]
        for a in range(1, PAIRS_PER_KV):
            sink_e = jnp.where(row >= a * CHUNK, sinks_ref[2 * (grp * PAIRS_PER_KV + a)], sink_e)
            sink_o = jnp.where(row >= a * CHUNK, sinks_ref[2 * (grp * PAIRS_PER_KV + a) + 1], sink_o)
        valid = None
        if i < SWA_WINDOW // CHUNK:
            valid = jnp.logical_or(col >= SWA_WINDOW - i * CHUNK, j > 0)
        e, den = _swa_softmax(sw_s_scr[rows_of(n), :], nk, sink_e, sink_o, valid)
        sw_e_scr[rows_of(n), :] = e
        sw_d_scr[rows_of(n), :] = den
    for n, (grp, i) in enumerate(blocks):
        r0 = i * CHUNK
        vt = jnp.concatenate([v_var[2 * grp][r0:r0 + nk], v_var[2 * grp + 1][r0:r0 + nk]], axis=0)
        o = (_dot(sw_e_scr[rows_of(n), :], vt) / sw_d_scr[rows_of(n), :]).astype(BF16)
        for a in range(PAIRS_PER_KV):
            m = grp * PAIRS_PER_KV + a
            mix_scr[r0:r0 + CHUNK, RET_W + m * LANES:RET_W + (m + 1) * LANES] = o[a * CHUNK:(a + 1) * CHUNK]

    kext_scr[0:SWA_WINDOW, :] = k_new[tt - SWA_WINDOW:, :]
    vext_scr[0:SWA_WINDOW, :] = v_new[tt - SWA_WINDOW:, :]

    z = ALPHA * x + _dot(mix_scr[...], w_out_ref[...])
    x1_ref[0] = _layer_norm(z, ln1w_ref[...], ln1b_ref[...])

    @pl.when(j == last_j)
    def _():
        kout_ref[0] = k_new[tt - SWA_WINDOW:, :]
        vout_ref[0] = v_new[tt - SWA_WINDOW:, :]
        sout_ref[0] = s_scr[...]


def _prompt_mixer(x, tab, dmask, qdec, kdec, gt, sinks, w_in, gnw, w_out, ln1w, ln1b):
    b, t, d = x.shape
    tt = PROMPT_TILE
    const2 = lambda bi, ji: (0, 0)
    const3 = lambda bi, ji: (0, 0, 0)
    return pl.pallas_call(
        _prompt_mixer_kernel,
        grid=(b, t // tt),
        in_specs=[
            pl.BlockSpec(memory_space=pltpu.SMEM),
            pl.BlockSpec((1, tt, d), lambda bi, ji: (bi, ji, 0)),
            pl.BlockSpec((tt, N_TABS * LANES), lambda bi, ji: (ji, 0)),
            pl.BlockSpec((RET_HEADS, tt, tt), const3),
            pl.BlockSpec((tt, RET_W), const2),
            pl.BlockSpec((tt, RET_W), const2),
            pl.BlockSpec((1, RET_W), const2),
            pl.BlockSpec((d, PROJ_W), const2),
            pl.BlockSpec((1, RET_W), const2),
            pl.BlockSpec((MIX_W, d), const2),
            pl.BlockSpec((1, d), const2),
            pl.BlockSpec((1, d), const2),
        ],
        out_specs=[
            pl.BlockSpec((1, tt, d), lambda bi, ji: (bi, ji, 0)),
            pl.BlockSpec((1, SWA_WINDOW, KV_W), lambda bi, ji: (bi, 0, 0)),
            pl.BlockSpec((1, SWA_WINDOW, KV_W), lambda bi, ji: (bi, 0, 0)),
            pl.BlockSpec((1, RET_HEADS, RET_DK, RET_DV), lambda bi, ji: (bi, 0, 0, 0)),
        ],
        out_shape=[
            jax.ShapeDtypeStruct((b, t, d), F32),
            jax.ShapeDtypeStruct((b, SWA_WINDOW, KV_W), F32),
            jax.ShapeDtypeStruct((b, SWA_WINDOW, KV_W), F32),
            jax.ShapeDtypeStruct((b, RET_HEADS, RET_DK, RET_DV), F32),
        ],
        scratch_shapes=[
            pltpu.VMEM((tt, PROJ_W), F32),
            pltpu.VMEM((RET_HEADS, RET_DK, RET_DV), F32),
            pltpu.VMEM((SWA_WINDOW + tt, KV_W), F32),
            pltpu.VMEM((SWA_WINDOW + tt, KV_W), F32),
            pltpu.VMEM((tt, MIX_W), BF16),
            pltpu.VMEM((N_PAIRS * tt, 2 * (SWA_WINDOW + CHUNK)), F32),
            pltpu.VMEM((N_PAIRS * tt, 2 * (SWA_WINDOW + CHUNK)), BF16),
            pltpu.VMEM((N_PAIRS * tt, LANES), F32),
        ],
        compiler_params=pltpu.CompilerParams(
            dimension_semantics=("arbitrary", "arbitrary"), vmem_limit_bytes=VMEM_LIMIT),
        name="prompt_mixer",
    )(sinks, x, tab, dmask, qdec, kdec, gt, w_in, gnw, w_out, ln1w, ln1b)


def _sample_mixer_kernel(sinks_ref, x_ref, tab_ref, dmask_ref, qdec_ref, kdec_ref, gt_ref, ck_ref, cv_ref,
                         state_ref, w_in_ref, gnw_ref, w_out_ref, ln1w_ref, ln1b_ref,
                         x1_ref, kout_ref, vout_ref, sout_ref,
                         p_scr, mix_scr, *, n_new):
    s_idx = pl.program_id(0)
    last_s = pl.num_programs(0) - 1

    @pl.when(s_idx == 0)
    def _():
        p_scr[...] = _dot(x_ref[...].astype(BF16), w_in_ref[...])

    rows = pl.ds(pl.multiple_of(s_idx * n_new, n_new), n_new)

    for h in range(RET_HEADS):
        sl = slice(h * LANES, (h + 1) * LANES)
        q = _rope_ret(p_scr[rows, OFF_RQ + h * LANES:OFF_RQ + (h + 1) * LANES], _tab(tab_ref, T_RQC), _tab(tab_ref, T_RQS))
        k = _rope_ret(p_scr[rows, OFF_RK + h * LANES:OFF_RK + (h + 1) * LANES], _tab(tab_ref, T_RKC), _tab(tab_ref, T_RKS))
        v = p_scr[rows, OFF_RV + h * LANES:OFF_RV + (h + 1) * LANES]
        g = p_scr[rows, OFF_RG + h * LANES:OFF_RG + (h + 1) * LANES]
        out, s_new = _retention_head(q, k, v, g, state_ref[0, h], dmask_ref[h], qdec_ref[:, sl], kdec_ref[:, sl],
                                     gt_ref[:, sl], gnw_ref[:, sl])
        sout_ref[0, h] = s_new
        mix_scr[rows, sl] = out.astype(BF16)

    k_new = _rope_swa(p_scr[rows, OFF_SK:OFF_SK + KV_W], _tab(tab_ref, T_SKC), _tab(tab_ref, T_SK1), _tab(tab_ref, T_SK2))
    v_new = p_scr[rows, OFF_SV:OFF_SV + KV_W]
    kout_ref[0] = k_new
    vout_ref[0] = v_new
    k_var = _kv_variants(jnp.concatenate([ck_ref[0], k_new], axis=0))
    v_var = _kv_variants(jnp.concatenate([cv_ref[0], v_new], axis=0))
    nk = ck_ref.shape[1] + n_new
    for m in range(N_PAIRS):
        qs = p_scr[rows, OFF_SQ + m * LANES:OFF_SQ + (m + 1) * LANES]
        qs = _rope_swa(qs, _tab(tab_ref, T_SQC), _tab(tab_ref, T_SQ1), _tab(tab_ref, T_SQ2)).astype(BF16)
        grp = (2 * m) // (SWA_HEADS // SWA_KV_HEADS)
        kt = jnp.concatenate([k_var[2 * grp], k_var[2 * grp + 1]], axis=0)
        vt = jnp.concatenate([v_var[2 * grp], v_var[2 * grp + 1]], axis=0)
        o = _swa_pair(qs, kt, vt, nk, sinks_ref[2 * m], sinks_ref[2 * m + 1], None)
        mix_scr[rows, RET_W + m * LANES:RET_W + (m + 1) * LANES] = o.astype(BF16)

    @pl.when(s_idx == last_s)
    def _():
        z = ALPHA * x_ref[...] + _dot(mix_scr[...], w_out_ref[...])
        x1_ref[...] = _layer_norm(z, ln1w_ref[...], ln1b_ref[...])


def _sample_mixer(x, tab, dmask, qdec, kdec, gt, cache_k, cache_v, state, sinks, w_in, gnw, w_out, ln1w, ln1b):
    nb, n_new, d = x.shape
    rows = nb * n_new
    win = cache_k.shape[1]
    const2 = lambda si: (0, 0)
    const3 = lambda si: (0, 0, 0)
    return pl.pallas_call(
        functools.partial(_sample_mixer_kernel, n_new=n_new),
        grid=(nb,),
        in_specs=[
            pl.BlockSpec(memory_space=pltpu.SMEM),
            pl.BlockSpec((rows, d), const2),
            pl.BlockSpec((n_new, N_TABS * LANES), const2),
            pl.BlockSpec((RET_HEADS, n_new, n_new), const3),
            pl.BlockSpec((n_new, RET_W), const2),
            pl.BlockSpec((n_new, RET_W), const2),
            pl.BlockSpec((1, RET_W), const2),
            pl.BlockSpec((1, win, KV_W), lambda si: (si, 0, 0)),
            pl.BlockSpec((1, win, KV_W), lambda si: (si, 0, 0)),
            pl.BlockSpec((1, RET_HEADS, RET_DK, RET_DV), lambda si: (si, 0, 0, 0)),
            pl.BlockSpec((d, PROJ_W), const2),
            pl.BlockSpec((1, RET_W), const2),
            pl.BlockSpec((MIX_W, d), const2),
            pl.BlockSpec((1, d), const2),
            pl.BlockSpec((1, d), const2),
        ],
        out_specs=[
            pl.BlockSpec((rows, d), const2),
            pl.BlockSpec((1, n_new, KV_W), lambda si: (si, 0, 0)),
            pl.BlockSpec((1, n_new, KV_W), lambda si: (si, 0, 0)),
            pl.BlockSpec((1, RET_HEADS, RET_DK, RET_DV), lambda si: (si, 0, 0, 0)),
        ],
        out_shape=[
            jax.ShapeDtypeStruct((rows, d), F32),
            jax.ShapeDtypeStruct((nb, n_new, KV_W), F32),
            jax.ShapeDtypeStruct((nb, n_new, KV_W), F32),
            jax.ShapeDtypeStruct((nb, RET_HEADS, RET_DK, RET_DV), F32),
        ],
        scratch_shapes=[
            pltpu.VMEM((rows, PROJ_W), F32),
            pltpu.VMEM((rows, MIX_W), BF16),
        ],
        compiler_params=pltpu.CompilerParams(
            dimension_semantics=("arbitrary",), vmem_limit_bytes=VMEM_LIMIT),
        name="sample_mixer",
    )(sinks, x.reshape(rows, d), tab, dmask, qdec, kdec, gt, cache_k, cache_v, state, w_in, gnw, w_out, ln1w, ln1b)


def _mlp_kernel(x_ref, w_up_ref, w_down_ref, lnw_ref, lnb_ref, y_ref):
    x = x_ref[...]
    h = _dot(x.astype(BF16), w_up_ref[...])
    h = jnp.square(jnp.maximum(h, 0.0)).astype(BF16)
    z = ALPHA * x + _dot(h, w_down_ref[...])
    y_ref[...] = _layer_norm(z, lnw_ref[...], lnb_ref[...])


def _mlp(x, w_up, w_down, lnw, lnb, tile):
    rows, d = x.shape
    const2 = lambda i: (0, 0)
    return pl.pallas_call(
        _mlp_kernel,
        grid=(rows // tile,),
        in_specs=[
            pl.BlockSpec((tile, d), lambda i: (i, 0)),
            pl.BlockSpec((d, D_FF), const2, pipeline_mode=pl.Buffered(1)),
            pl.BlockSpec((D_FF, d), const2, pipeline_mode=pl.Buffered(1)),
            pl.BlockSpec((1, d), const2),
            pl.BlockSpec((1, d), const2),
        ],
        out_specs=pl.BlockSpec((tile, d), lambda i: (i, 0)),
        out_shape=jax.ShapeDtypeStruct((rows, d), F32),
        compiler_params=pltpu.CompilerParams(
            dimension_semantics=("arbitrary",), vmem_limit_bytes=VMEM_LIMIT),
        name="mlp",
    )(x, w_up, w_down, lnw, lnb)


def _rope_tables(pos):
    posf = pos.astype(F32)[:, None]
    half_r = RET_DK // 2
    ang = posf * (RET_THETA ** (-jnp.arange(half_r, dtype=F32) / half_r))[None, :]
    c, s = jnp.cos(ang), jnp.sin(ang)
    rc = jnp.concatenate([c, c], axis=-1)
    rs = jnp.concatenate([-s, s], axis=-1)
    k_scale = RET_DK ** -0.5
    half_s = SWA_ROT_DIM // 2
    ang = posf * (SWA_THETA ** (-jnp.arange(half_s, dtype=F32) / half_s))[None, :]
    c, s = jnp.cos(ang), jnp.sin(ang)
    t = pos.shape[0]
    rest = SWA_HD - SWA_ROT_DIM
    z = jnp.zeros((t, half_s), F32)
    sc = jnp.tile(jnp.concatenate([c, c, jnp.ones((t, rest), F32)], axis=-1), (1, LANES // SWA_HD))
    s1 = jnp.tile(jnp.concatenate([z, s, jnp.zeros((t, rest), F32)], axis=-1), (1, LANES // SWA_HD))
    s2 = jnp.tile(jnp.concatenate([-s, z, jnp.zeros((t, rest), F32)], axis=-1), (1, LANES // SWA_HD))
    q_scale = SWA_HD ** -0.5
    return jnp.concatenate([rc, rs, rc * k_scale, rs * k_scale,
                            sc * q_scale, s1 * q_scale, s2 * q_scale, sc, s1, s2], axis=-1)


def _decay_tables(t):
    lg = jnp.log1p(-(2.0 ** (-5.0 - jnp.arange(RET_HEADS, dtype=F32))))
    idx = jnp.arange(t, dtype=F32)
    diff = idx[:, None] - idx[None, :]
    dmask = jnp.where(diff >= 0, jnp.exp(lg[:, None, None] * jnp.maximum(diff, 0.0)), 0.0)
    qdec = jnp.repeat(jnp.exp(lg[None, :] * (idx[:, None] + 1.0)), RET_DV, axis=1)
    kdec = jnp.repeat(jnp.exp(lg[None, :] * (t - 1.0 - idx[:, None])), RET_DV, axis=1)
    gt = jnp.repeat(jnp.exp(lg * t)[None, :], RET_DV, axis=1)
    return dmask, qdec, kdec, gt


def kernel(x_prompt, x_sample, cache_swa_k, cache_swa_v, state_ret, w_in, ret_gn_w, swa_sinks,
           w_out, ln1_w, ln1_b, w_up, w_down, ln2_w, ln2_b):
    assert w_in.shape[0] == DEPTH == 1
    b, t, d = x_prompt.shape
    nb, n_new, _ = x_sample.shape
    win = cache_swa_k.shape[2]

    w_in_b = w_in[0].astype(BF16)
    w_out_b = w_out[0].astype(BF16)
    w_up_b = w_up[0].astype(BF16)
    w_down_b = w_down[0].astype(BF16)
    sinks = swa_sinks[0]

    tab_p = _rope_tables(jnp.arange(t))
    tab_s = _rope_tables(PAST_LEN + jnp.arange(n_new))
    dec_p = _decay_tables(PROMPT_TILE)
    dec_s = _decay_tables(n_new)

    x1_p, k_p, v_p, r_p = _prompt_mixer(x_prompt, tab_p, *dec_p, sinks, w_in_b, ret_gn_w, w_out_b, ln1_w, ln1_b)
    y_p = _mlp(x1_p.reshape(b * t, d), w_up_b, w_down_b, ln2_w, ln2_b, MLP_TILE).reshape(b, t, d)

    x1_s, k_s, v_s, r_s = _sample_mixer(
        x_sample, tab_s, *dec_s, cache_swa_k[0].reshape(nb, win, KV_W), cache_swa_v[0].reshape(nb, win, KV_W),
        state_ret[0], sinks, w_in_b, ret_gn_w, w_out_b, ln1_w, ln1_b)
    y_s = _mlp(x1_s, w_up_b, w_down_b, ln2_w, ln2_b, nb * n_new).reshape(nb, n_new, d)

    return (y_p, y_s,
            k_p.reshape(DEPTH, b, SWA_WINDOW, SWA_KV_HEADS, SWA_HD),
            v_p.reshape(DEPTH, b, SWA_WINDOW, SWA_KV_HEADS, SWA_HD),
            r_p.reshape(DEPTH, b, RET_HEADS, RET_DK, RET_DV),
            k_s.reshape(DEPTH, nb, n_new, SWA_KV_HEADS, SWA_HD),
            v_s.reshape(DEPTH, nb, n_new, SWA_KV_HEADS, SWA_HD),
            r_s.reshape(DEPTH, nb, RET_HEADS, RET_DK, RET_DV))
```

```python
import functools

import jax
import jax.numpy as jnp
from jax import lax
from jax.experimental import pallas as pl
from jax.experimental.pallas import tpu as pltpu

D_MODEL = 1024
DEPTH = 1
PAST_LEN = 2048
CHUNK = 64
RET_HEADS = 4
RET_DK = 128
RET_DV = 128
RET_THETA = 10000.0
RET_W = RET_HEADS * RET_DV
SWA_HEADS = 8
SWA_KV_HEADS = 2
SWA_HD = 64
SWA_WINDOW = 128
SWA_ROT_DIM = SWA_HD // 4
SWA_THETA = 500000.0
SWA_W = SWA_HEADS * SWA_HD
KV_W = SWA_KV_HEADS * SWA_HD
MIX_W = RET_W + SWA_W
D_FF = 4 * D_MODEL
OFF_RQ, OFF_RK, OFF_RV, OFF_RG = 0, RET_W, 2 * RET_W, 3 * RET_W
OFF_SQ = 4 * RET_W
OFF_SK = OFF_SQ + SWA_W
OFF_SV = OFF_SK + KV_W
PROJ_W = OFF_SV + KV_W
ALPHA = (2.0 * DEPTH) ** 0.25
LN_EPS = 1e-5
GN_EPS = 1e-5
NEG_INF = -1e30

LANES = 128
MXU_COLS = 256
N_PAIRS = SWA_W // LANES
PAIRS_PER_KV = N_PAIRS // SWA_KV_HEADS
T_RQC, T_RQS, T_RKC, T_RKS, T_SQC, T_SQ1, T_SQ2, T_SKC, T_SK1, T_SK2 = range(10)
N_TABS = 10

PROMPT_TILE = 256
N_SLABS = PROJ_W // MXU_COLS
LN_BLOCKS = 4
MLP_TILE = 512
VMEM_LIMIT = 56 * 1024 * 1024

F32 = jnp.float32
BF16 = jnp.bfloat16


def _tab(tab_ref, i):
    return tab_ref[:, i * LANES:(i + 1) * LANES]


def _dot(a, b):
    return jnp.dot(a, b, preferred_element_type=F32)


def _dot_nt(a, b):
    return lax.dot_general(a, b, (((1,), (1,)), ((), ())), preferred_element_type=F32)


def _dot_tn(a, b):
    return lax.dot_general(a, b, (((0,), (0,)), ((), ())), preferred_element_type=F32)


def _layer_norm(z, w, b):
    mu = jnp.mean(z, axis=-1, keepdims=True)
    zc = z - mu
    var = jnp.mean(zc * zc, axis=-1, keepdims=True)
    return zc * lax.rsqrt(var + LN_EPS) * w + b


def _rope_ret(x, c, s):
    return x * c + pltpu.roll(x, RET_DK // 2, axis=1) * s


def _rope_swa(x, c, s1, s2):
    half = SWA_ROT_DIM // 2
    return x * c + pltpu.roll(x, half, axis=1) * s1 + pltpu.roll(x, LANES - half, axis=1) * s2


def _retention_head(q, k, v, g, s_prev, dmask, qdec, kdec, gt, gnw):
    qb = q.astype(BF16)
    kb = k.astype(BF16)
    vb = v.astype(BF16)
    scores = _dot_nt(qb, kb) * dmask
    o = _dot(scores.astype(BF16), vb)
    o = o + _dot(qb, s_prev.astype(BF16)) * qdec
    s_new = gt * s_prev + _dot_tn((k * kdec).astype(BF16), vb)
    mu = jnp.mean(o, axis=-1, keepdims=True)
    oc = o - mu
    var = jnp.mean(oc * oc, axis=-1, keepdims=True)
    on = oc * lax.rsqrt(var + GN_EPS) * gnw
    return jax.nn.silu(g) * on, s_new


def _kv_variants(a):
    lane = lax.broadcasted_iota(jnp.int32, a.shape, 1)
    lo = lane < SWA_HD
    ar = pltpu.roll(a, SWA_HD, axis=1)
    zero = jnp.zeros_like(a)
    return (jnp.where(lo, a, zero).astype(BF16), jnp.where(lo, zero, ar).astype(BF16),
            jnp.where(lo, ar, zero).astype(BF16), jnp.where(lo, zero, a).astype(BF16))


def _swa_softmax(s, nk, sink_e, sink_o, valid):
    if valid is not None:
        s = jnp.where(valid, s, NEG_INF)
    col = lax.broadcasted_iota(jnp.int32, s.shape, 1)
    is_o = col >= nk
    m_e = jnp.maximum(jnp.max(jnp.where(is_o, -jnp.inf, s), axis=-1, keepdims=True), sink_e)
    m_o = jnp.maximum(jnp.max(jnp.where(is_o, s, -jnp.inf), axis=-1, keepdims=True), sink_o)
    e = jnp.exp(s - jnp.where(is_o, m_o, m_e))
    d_e = jnp.sum(jnp.where(is_o, 0.0, e), axis=-1, keepdims=True) + jnp.exp(sink_e - m_e)
    d_o = jnp.sum(jnp.where(is_o, e, 0.0), axis=-1, keepdims=True) + jnp.exp(sink_o - m_o)
    lane = lax.broadcasted_iota(jnp.int32, (s.shape[0], LANES), 1)
    return e.astype(BF16), jnp.where(lane < SWA_HD, d_e, d_o)


def _swa_pair(qp, kt, vt, nk, sink_e, sink_o, valid):
    e, den = _swa_softmax(_dot_nt(qp, kt), nk, sink_e, sink_o, valid)
    return _dot(e, vt) / den


def _prompt_mixer_kernel(sinks_ref, x_ref, xn_ref, tab_ref, dmask_ref, qdec_ref, kdec_ref, gt_ref, w_in_ref,
                         gnw_ref, w_out_ref, ln1w_ref, ln1b_ref,
                         x1_ref, kout_ref, vout_ref, sout_ref,
                         pa_scr, pb_scr, xb_scr, s_scr, kext_scr, vext_scr, mix_scr, sw_s_scr, sw_e_scr, sw_d_scr,
                         *, steps_per_seq):
    tt = PROMPT_TILE
    g = pl.program_id(0)
    seq_step = g % steps_per_seq

    @pl.when(seq_step == 0)
    def _():
        s_scr[...] = jnp.zeros_like(s_scr)
        kext_scr[0:SWA_WINDOW, :] = jnp.zeros((SWA_WINDOW, KV_W), F32)
        vext_scr[0:SWA_WINDOW, :] = jnp.zeros((SWA_WINDOW, KV_W), F32)

    @pl.when(g == 0)
    def _():
        pa_scr[...] = _dot(x_ref[0, 0:tt, :].astype(BF16), w_in_ref[...])

    def project_next_slab(p_nxt, c):
        cols = slice(c * MXU_COLS, (c + 1) * MXU_COLS)
        p_nxt[:, cols] = _dot(xb_scr[...], w_in_ref[:, cols])

    def tile(r0x, p_cur, p_nxt, load_next_x, seq_start):
        rows_x = slice(r0x, r0x + tt)
        tab = lambda i: tab_ref[rows_x, i * LANES:(i + 1) * LANES]
        xb_scr[...] = load_next_x().astype(BF16)

        for h in range(RET_HEADS):
            sl = slice(h * LANES, (h + 1) * LANES)
            q = _rope_ret(p_cur[:, OFF_RQ + h * LANES:OFF_RQ + (h + 1) * LANES], tab(T_RQC), tab(T_RQS))
            k = _rope_ret(p_cur[:, OFF_RK + h * LANES:OFF_RK + (h + 1) * LANES], tab(T_RKC), tab(T_RKS))
            v = p_cur[:, OFF_RV + h * LANES:OFF_RV + (h + 1) * LANES]
            gate = p_cur[:, OFF_RG + h * LANES:OFF_RG + (h + 1) * LANES]
            out, s_new = _retention_head(q, k, v, gate, s_scr[h], dmask_ref[h], qdec_ref[:, sl], kdec_ref[:, sl],
                                         gt_ref[:, sl], gnw_ref[:, sl])
            s_scr[h] = s_new
            mix_scr[:, sl] = out.astype(BF16)

        k_new = _rope_swa(p_cur[:, OFF_SK:OFF_SK + KV_W], tab(T_SKC), tab(T_SK1), tab(T_SK2))
        v_new = p_cur[:, OFF_SV:OFF_SV + KV_W]
        kext_scr[SWA_WINDOW:SWA_WINDOW + tt, :] = k_new
        vext_scr[SWA_WINDOW:SWA_WINDOW + tt, :] = v_new
        k_var = _kv_variants(kext_scr[...])
        v_var = _kv_variants(vext_scr[...])
        nk = SWA_WINDOW + CHUNK
        n_chunks = tt // CHUNK
        qs = [_rope_swa(p_cur[:, OFF_SQ + m * LANES:OFF_SQ + (m + 1) * LANES],
                        tab(T_SQC), tab(T_SQ1), tab(T_SQ2)).astype(BF16)
              for m in range(N_PAIRS)]
        col = lax.broadcasted_iota(jnp.int32, (PAIRS_PER_KV * CHUNK, 2 * nk), 1)
        col = jnp.where(col >= nk, col - nk, col)
        row = lax.broadcasted_iota(jnp.int32, (PAIRS_PER_KV * CHUNK, 1), 0)
        blocks = [(grp, i) for grp in range(SWA_KV_HEADS) for i in range(n_chunks)]
        rows_of = lambda n: slice(n * PAIRS_PER_KV * CHUNK, (n + 1) * PAIRS_PER_KV * CHUNK)
        for n, (grp, i) in enumerate(blocks):
            r0 = i * CHUNK
            q_blk = jnp.concatenate([qs[grp * PAIRS_PER_KV + a][r0:r0 + CHUNK] for a in range(PAIRS_PER_KV)], axis=0)
            kt = jnp.concatenate([k_var[2 * grp][r0:r0 + nk], k_var[2 * grp + 1][r0:r0 + nk]], axis=0)
            sw_s_scr[rows_of(n), :] = _dot_nt(q_blk, kt)
        slab = 0
        for n, (grp, i) in enumerate(blocks):
            sink_e = sinks_ref[2 * grp * PAIRS_PER_KV]
            sink_o = sinks_ref[2 * grp * PAIRS_PER_KV + 1]
            for a in range(1, PAIRS_PER_KV):
                sink_e = jnp.where(row >= a * CHUNK, sinks_ref[2 * (grp * PAIRS_PER_KV + a)], sink_e)
                sink_o = jnp.where(row >= a * CHUNK, sinks_ref[2 * (grp * PAIRS_PER_KV + a) + 1], sink_o)
            valid = None
            if seq_start is not None and i < SWA_WINDOW // CHUNK:
                valid = jnp.logical_or(col >= SWA_WINDOW - i * CHUNK, jnp.logical_not(seq_start))
            e, den = _swa_softmax(sw_s_scr[rows_of(n), :], nk, sink_e, sink_o, valid)
            sw_e_scr[rows_of(n), :] = e
            sw_d_scr[rows_of(n), :] = den
            if slab < N_SLABS - (LN_BLOCKS - 1):
                project_next_slab(p_nxt, slab)
                slab += 1
        for n, (grp, i) in enumerate(blocks):
            r0 = i * CHUNK
            vt = jnp.concatenate([v_var[2 * grp][r0:r0 + nk], v_var[2 * grp + 1][r0:r0 + nk]], axis=0)
            o = (_dot(sw_e_scr[rows_of(n), :], vt) / sw_d_scr[rows_of(n), :]).astype(BF16)
            for a in range(PAIRS_PER_KV):
                m = grp * PAIRS_PER_KV + a
                mix_scr[r0:r0 + CHUNK, RET_W + m * LANES:RET_W + (m + 1) * LANES] = o[a * CHUNK:(a + 1) * CHUNK]

        kext_scr[0:SWA_WINDOW, :] = k_new[tt - SWA_WINDOW:, :]
        vext_scr[0:SWA_WINDOW, :] = v_new[tt - SWA_WINDOW:, :]

        x1_ref[0, rows_x, :] = ALPHA * x_ref[0, rows_x, :] + _dot(mix_scr[...], w_out_ref[...])
        lb = tt // LN_BLOCKS
        for r in range(LN_BLOCKS):
            rows_ln = slice(r0x + r * lb, r0x + (r + 1) * lb)
            x1_ref[0, rows_ln, :] = _layer_norm(x1_ref[0, rows_ln, :], ln1w_ref[...], ln1b_ref[...])
            if slab < N_SLABS:
                project_next_slab(p_nxt, slab)
                slab += 1
        assert slab == N_SLABS
        return k_new, v_new

    tile(0, pa_scr, pb_scr, lambda: x_ref[0, tt:2 * tt, :], seq_step == 0)
    k_new, v_new = tile(tt, pb_scr, pa_scr, lambda: xn_ref[0], None)

    @pl.when(seq_step == steps_per_seq - 1)
    def _():
        kout_ref[0] = k_new[tt - SWA_WINDOW:, :]
        vout_ref[0] = v_new[tt - SWA_WINDOW:, :]
        sout_ref[0] = s_scr[...]


def _prompt_mixer(x, tab, dmask, qdec, kdec, gt, sinks, w_in, gnw, w_out, ln1w, ln1b):
    b, t, d = x.shape
    tt = PROMPT_TILE
    sps = t // (2 * tt)
    n_steps = b * sps
    const2 = lambda g: (0, 0)
    const3 = lambda g: (0, 0, 0)

    def next_tile(g):
        gn = jnp.minimum(g + 1, n_steps - 1)
        return (gn // sps, (gn % sps) * 2, 0)

    resident = dict(pipeline_mode=pl.Buffered(1))
    return pl.pallas_call(
        functools.partial(_prompt_mixer_kernel, steps_per_seq=sps),
        grid=(n_steps,),
        in_specs=[
            pl.BlockSpec(memory_space=pltpu.SMEM),
            pl.BlockSpec((1, 2 * tt, d), lambda g: (g // sps, g % sps, 0)),
            pl.BlockSpec((1, tt, d), next_tile),
            pl.BlockSpec((2 * tt, N_TABS * LANES), lambda g: (g % sps, 0)),
            pl.BlockSpec((RET_HEADS, tt, tt), const3, **resident),
            pl.BlockSpec((tt, RET_W), const2, **resident),
            pl.BlockSpec((tt, RET_W), const2, **resident),
            pl.BlockSpec((1, RET_W), const2),
            pl.BlockSpec((d, PROJ_W), const2, **resident),
            pl.BlockSpec((1, RET_W), const2),
            pl.BlockSpec((MIX_W, d), const2, **resident),
            pl.BlockSpec((1, d), const2),
            pl.BlockSpec((1, d), const2),
        ],
        out_specs=[
            pl.BlockSpec((1, 2 * tt, d), lambda g: (g // sps, g % sps, 0)),
            pl.BlockSpec((1, SWA_WINDOW, KV_W), lambda g: (g // sps, 0, 0)),
            pl.BlockSpec((1, SWA_WINDOW, KV_W), lambda g: (g // sps, 0, 0)),
            pl.BlockSpec((1, RET_HEADS, RET_DK, RET_DV), lambda g: (g // sps, 0, 0, 0)),
        ],
        out_shape=[
            jax.ShapeDtypeStruct((b, t, d), F32),
            jax.ShapeDtypeStruct((b, SWA_WINDOW, KV_W), F32),
            jax.ShapeDtypeStruct((b, SWA_WINDOW, KV_W), F32),
            jax.ShapeDtypeStruct((b, RET_HEADS, RET_DK, RET_DV), F32),
        ],
        scratch_shapes=[
            pltpu.VMEM((tt, PROJ_W), F32),
            pltpu.VMEM((tt, PROJ_W), F32),
            pltpu.VMEM((tt, d), BF16),
            pltpu.VMEM((RET_HEADS, RET_DK, RET_DV), F32),
            pltpu.VMEM((SWA_WINDOW + tt, KV_W), F32),
            pltpu.VMEM((SWA_WINDOW + tt, KV_W), F32),
            pltpu.VMEM((tt, MIX_W), BF16),
            pltpu.VMEM((N_PAIRS * tt, 2 * (SWA_WINDOW + CHUNK)), F32),
            pltpu.VMEM((N_PAIRS * tt, 2 * (SWA_WINDOW + CHUNK)), BF16),
            pltpu.VMEM((N_PAIRS * tt, LANES), F32),
        ],
        compiler_params=pltpu.CompilerParams(
            dimension_semantics=("arbitrary",), vmem_limit_bytes=VMEM_LIMIT),
        name="prompt_mixer",
    )(sinks, x, x, tab, dmask, qdec, kdec, gt, w_in, gnw, w_out, ln1w, ln1b)


def _sample_mixer_kernel(sinks_ref, x_ref, tab_ref, dmask_ref, qdec_ref, kdec_ref, gt_ref, ck_ref, cv_ref,
                         state_ref, w_in_ref, gnw_ref, w_out_ref, ln1w_ref, ln1b_ref,
                         x1_ref, kout_ref, vout_ref, sout_ref,
                         p_scr, mix_scr, *, n_new):
    s_idx = pl.program_id(0)
    last_s = pl.num_programs(0) - 1

    @pl.when(s_idx == 0)
    def _():
        p_scr[...] = _dot(x_ref[...].astype(BF16), w_in_ref[...])

    rows = pl.ds(pl.multiple_of(s_idx * n_new, n_new), n_new)

    for h in range(RET_HEADS):
        sl = slice(h * LANES, (h + 1) * LANES)
        q = _rope_ret(p_scr[rows, OFF_RQ + h * LANES:OFF_RQ + (h + 1) * LANES], _tab(tab_ref, T_RQC), _tab(tab_ref, T_RQS))
        k = _rope_ret(p_scr[rows, OFF_RK + h * LANES:OFF_RK + (h + 1) * LANES], _tab(tab_ref, T_RKC), _tab(tab_ref, T_RKS))
        v = p_scr[rows, OFF_RV + h * LANES:OFF_RV + (h + 1) * LANES]
        g = p_scr[rows, OFF_RG + h * LANES:OFF_RG + (h + 1) * LANES]
        out, s_new = _retention_head(q, k, v, g, state_ref[0, h], dmask_ref[h], qdec_ref[:, sl], kdec_ref[:, sl],
                                     gt_ref[:, sl], gnw_ref[:, sl])
        sout_ref[0, h] = s_new
        mix_scr[rows, sl] = out.astype(BF16)

    k_new = _rope_swa(p_scr[rows, OFF_SK:OFF_SK + KV_W], _tab(tab_ref, T_SKC), _tab(tab_ref, T_SK1), _tab(tab_ref, T_SK2))
    v_new = p_scr[rows, OFF_SV:OFF_SV + KV_W]
    kout_ref[0] = k_new
    vout_ref[0] = v_new
    k_var = _kv_variants(jnp.concatenate([ck_ref[0], k_new], axis=0))
    v_var = _kv_variants(jnp.concatenate([cv_ref[0], v_new], axis=0))
    nk = ck_ref.shape[1] + n_new
    for m in range(N_PAIRS):
        qs = p_scr[rows, OFF_SQ + m * LANES:OFF_SQ + (m + 1) * LANES]
        qs = _rope_swa(qs, _tab(tab_ref, T_SQC), _tab(tab_ref, T_SQ1), _tab(tab_ref, T_SQ2)).astype(BF16)
        grp = (2 * m) // (SWA_HEADS // SWA_KV_HEADS)
        kt = jnp.concatenate([k_var[2 * grp], k_var[2 * grp + 1]], axis=0)
        vt = jnp.concatenate([v_var[2 * grp], v_var[2 * grp + 1]], axis=0)
        o = _swa_pair(qs, kt, vt, nk, sinks_ref[2 * m], sinks_ref[2 * m + 1], None)
        mix_scr[rows, RET_W + m * LANES:RET_W + (m + 1) * LANES] = o.astype(BF16)

    @pl.when(s_idx == last_s)
    def _():
        z = ALPHA * x_ref[...] + _dot(mix_scr[...], w_out_ref[...])
        x1_ref[...] = _layer_norm(z, ln1w_ref[...], ln1b_ref[...])


def _sample_mixer(x, tab, dmask, qdec, kdec, gt, cache_k, cache_v, state, sinks, w_in, gnw, w_out, ln1w, ln1b):
    nb, n_new, d = x.shape
    rows = nb * n_new
    win = cache_k.shape[1]
    const2 = lambda si: (0, 0)
    const3 = lambda si: (0, 0, 0)
    return pl.pallas_call(
        functools.partial(_sample_mixer_kernel, n_new=n_new),
        grid=(nb,),
        in_specs=[
            pl.BlockSpec(memory_space=pltpu.SMEM),
            pl.BlockSpec((rows, d), const2),
            pl.BlockSpec((n_new, N_TABS * LANES), const2),
            pl.BlockSpec((RET_HEADS, n_new, n_new), const3),
            pl.BlockSpec((n_new, RET_W), const2),
            pl.BlockSpec((n_new, RET_W), const2),
            pl.BlockSpec((1, RET_W), const2),
            pl.BlockSpec((1, win, KV_W), lambda si: (si, 0, 0)),
            pl.BlockSpec((1, win, KV_W), lambda si: (si, 0, 0)),
            pl.BlockSpec((1, RET_HEADS, RET_DK, RET_DV), lambda si: (si, 0, 0, 0)),
            pl.BlockSpec((d, PROJ_W), const2),
            pl.BlockSpec((1, RET_W), const2),
            pl.BlockSpec((MIX_W, d), const2),
            pl.BlockSpec((1, d), const2),
            pl.BlockSpec((1, d), const2),
        ],
        out_specs=[
            pl.BlockSpec((rows, d), const2),
            pl.BlockSpec((1, n_new, KV_W), lambda si: (si, 0, 0)),
            pl.BlockSpec((1, n_new, KV_W), lambda si: (si, 0, 0)),
            pl.BlockSpec((1, RET_HEADS, RET_DK, RET_DV), lambda si: (si, 0, 0, 0)),
        ],
        out_shape=[
            jax.ShapeDtypeStruct((rows, d), F32),
            jax.ShapeDtypeStruct((nb, n_new, KV_W), F32),
            jax.ShapeDtypeStruct((nb, n_new, KV_W), F32),
            jax.ShapeDtypeStruct((nb, RET_HEADS, RET_DK, RET_DV), F32),
        ],
        scratch_shapes=[
            pltpu.VMEM((rows, PROJ_W), F32),
            pltpu.VMEM((rows, MIX_W), BF16),
        ],
        compiler_params=pltpu.CompilerParams(
            dimension_semantics=("arbitrary",), vmem_limit_bytes=VMEM_LIMIT),
        name="sample_mixer",
    )(sinks, x.reshape(rows, d), tab, dmask, qdec, kdec, gt, cache_k, cache_v, state, w_in, gnw, w_out, ln1w, ln1b)


def _mlp_kernel(x_ref, w_up_ref, w_down_ref, lnw_ref, lnb_ref, y_ref):
    x = x_ref[...]
    h = _dot(x.astype(BF16), w_up_ref[...])
    h = jnp.square(jnp.maximum(h, 0.0)).astype(BF16)
    z = ALPHA * x + _dot(h, w_down_ref[...])
    y_ref[...] = _layer_norm(z, lnw_ref[...], lnb_ref[...])


def _mlp(x, w_up, w_down, lnw, lnb, tile):
    rows, d = x.shape
    const2 = lambda i: (0, 0)
    return pl.pallas_call(
        _mlp_kernel,
        grid=(rows // tile,),
        in_specs=[
            pl.BlockSpec((tile, d), lambda i: (i, 0)),
            pl.BlockSpec((d, D_FF), const2, pipeline_mode=pl.Buffered(1)),
            pl.BlockSpec((D_FF, d), const2, pipeline_mode=pl.Buffered(1)),
            pl.BlockSpec((1, d), const2),
            pl.BlockSpec((1, d), const2),
        ],
        out_specs=pl.BlockSpec((tile, d), lambda i: (i, 0)),
        out_shape=jax.ShapeDtypeStruct((rows, d), F32),
        compiler_params=pltpu.CompilerParams(
            dimension_semantics=("arbitrary",), vmem_limit_bytes=VMEM_LIMIT),
        name="mlp",
    )(x, w_up, w_down, lnw, lnb)


def _rope_tables(pos):
    posf = pos.astype(F32)[:, None]
    half_r = RET_DK // 2
    ang = posf * (RET_THETA ** (-jnp.arange(half_r, dtype=F32) / half_r))[None, :]
    c, s = jnp.cos(ang), jnp.sin(ang)
    rc = jnp.concatenate([c, c], axis=-1)
    rs = jnp.concatenate([-s, s], axis=-1)
    k_scale = RET_DK ** -0.5
    half_s = SWA_ROT_DIM // 2
    ang = posf * (SWA_THETA ** (-jnp.arange(half_s, dtype=F32) / half_s))[None, :]
    c, s = jnp.cos(ang), jnp.sin(ang)
    t = pos.shape[0]
    rest = SWA_HD - SWA_ROT_DIM
    z = jnp.zeros((t, half_s), F32)
    sc = jnp.tile(jnp.concatenate([c, c, jnp.ones((t, rest), F32)], axis=-1), (1, LANES // SWA_HD))
    s1 = jnp.tile(jnp.concatenate([z, s, jnp.zeros((t, rest), F32)], axis=-1), (1, LANES // SWA_HD))
    s2 = jnp.tile(jnp.concatenate([-s, z, jnp.zeros((t, rest), F32)], axis=-1), (1, LANES // SWA_HD))
    q_scale = SWA_HD ** -0.5
    return jnp.concatenate([rc, rs, rc * k_scale, rs * k_scale,
                            sc * q_scale, s1 * q_scale, s2 * q_scale, sc, s1, s2], axis=-1)


def _decay_tables(t):
    lg = jnp.log1p(-(2.0 ** (-5.0 - jnp.arange(RET_HEADS, dtype=F32))))
    idx = jnp.arange(t, dtype=F32)
    diff = idx[:, None] - idx[None, :]
    dmask = jnp.where(diff >= 0, jnp.exp(lg[:, None, None] * jnp.maximum(diff, 0.0)), 0.0)
    qdec = jnp.repeat(jnp.exp(lg[None, :] * (idx[:, None] + 1.0)), RET_DV, axis=1)
    kdec = jnp.repeat(jnp.exp(lg[None, :] * (t - 1.0 - idx[:, None])), RET_DV, axis=1)
    gt = jnp.repeat(jnp.exp(lg * t)[None, :], RET_DV, axis=1)
    return dmask, qdec, kdec, gt


def kernel(x_prompt, x_sample, cache_swa_k, cache_swa_v, state_ret, w_in, ret_gn_w, swa_sinks,
           w_out, ln1_w, ln1_b, w_up, w_down, ln2_w, ln2_b):
    assert w_in.shape[0] == DEPTH == 1
    b, t, d = x_prompt.shape
    nb, n_new, _ = x_sample.shape
    win = cache_swa_k.shape[2]

    w_in_b = w_in[0].astype(BF16)
    w_out_b = w_out[0].astype(BF16)
    w_up_b = w_up[0].astype(BF16)
    w_down_b = w_down[0].astype(BF16)
    sinks = swa_sinks[0]

    tab_p = _rope_tables(jnp.arange(t))
    tab_s = _rope_tables(PAST_LEN + jnp.arange(n_new))
    dec_p = _decay_tables(PROMPT_TILE)
    dec_s = _decay_tables(n_new)

    x1_p, k_p, v_p, r_p = _prompt_mixer(x_prompt, tab_p, *dec_p, sinks, w_in_b, ret_gn_w, w_out_b, ln1_w, ln1_b)
    y_p = _mlp(x1_p.reshape(b * t, d), w_up_b, w_down_b, ln2_w, ln2_b, MLP_TILE).reshape(b, t, d)

    x1_s, k_s, v_s, r_s = _sample_mixer(
        x_sample, tab_s, *dec_s, cache_swa_k[0].reshape(nb, win, KV_W), cache_swa_v[0].reshape(nb, win, KV_W),
        state_ret[0], sinks, w_in_b, ret_gn_w, w_out_b, ln1_w, ln1_b)
    y_s = _mlp(x1_s, w_up_b, w_down_b, ln2_w, ln2_b, nb * n_new).reshape(nb, n_new, d)

    return (y_p, y_s,
            k_p.reshape(DEPTH, b, SWA_WINDOW, SWA_KV_HEADS, SWA_HD),
            v_p.reshape(DEPTH, b, SWA_WINDOW, SWA_KV_HEADS, SWA_HD),
            r_p.reshape(DEPTH, b, RET_HEADS, RET_DK, RET_DV),
            k_s.reshape(DEPTH, nb, n_new, SWA_KV_HEADS, SWA_HD),
            v_s.reshape(DEPTH, nb, n_new, SWA_KV_HEADS, SWA_HD),
            r_s.reshape(DEPTH, nb, RET_HEADS, RET_DK, RET_DV))
```

```python
import functools

import numpy as np
import jax
import jax.numpy as jnp
from jax import lax
from jax.experimental import pallas as pl
from jax.experimental.pallas import tpu as pltpu

D_MODEL = 1024
DEPTH = 1
PAST_LEN = 2048
CHUNK = 64
RET_HEADS = 4
RET_DK = 128
RET_DV = 128
RET_THETA = 10000.0
RET_W = RET_HEADS * RET_DV
SWA_HEADS = 8
SWA_KV_HEADS = 2
SWA_HD = 64
SWA_WINDOW = 128
SWA_ROT_DIM = SWA_HD // 4
SWA_THETA = 500000.0
SWA_W = SWA_HEADS * SWA_HD
KV_W = SWA_KV_HEADS * SWA_HD
MIX_W = RET_W + SWA_W
D_FF = 4 * D_MODEL
OFF_RQ, OFF_RK, OFF_RV, OFF_RG = 0, RET_W, 2 * RET_W, 3 * RET_W
OFF_SQ = 4 * RET_W
OFF_SK = OFF_SQ + SWA_W
OFF_SV = OFF_SK + KV_W
PROJ_W = OFF_SV + KV_W
ALPHA = (2.0 * DEPTH) ** 0.25
LN_EPS = 1e-5
GN_EPS = 1e-5
NEG_INF = -1e30

LANES = 128
MXU_COLS = 256
N_PAIRS = SWA_W // LANES
PAIRS_PER_KV = N_PAIRS // SWA_KV_HEADS
T_RQC, T_RQS, T_RKC, T_RKS, T_SQC, T_SQ1, T_SQ2, T_SKC, T_SK1, T_SK2 = range(10)
N_TABS = 10

PROMPT_TILE = 256
N_SLABS = PROJ_W // MXU_COLS
LN_BLOCKS = 4
MLP_TILE = 512
VMEM_LIMIT = 56 * 1024 * 1024

F32 = jnp.float32
BF16 = jnp.bfloat16


def _tab(tab_ref, i):
    return tab_ref[:, i * LANES:(i + 1) * LANES]


def _dot(a, b):
    return jnp.dot(a, b, preferred_element_type=F32)


def _dot_nt(a, b):
    return lax.dot_general(a, b, (((1,), (1,)), ((), ())), preferred_element_type=F32)


def _dot_tn(a, b):
    return lax.dot_general(a, b, (((0,), (0,)), ((), ())), preferred_element_type=F32)


def _layer_norm(z, w, b):
    mu = jnp.mean(z, axis=-1, keepdims=True)
    zc = z - mu
    var = jnp.mean(zc * zc, axis=-1, keepdims=True)
    return zc * lax.rsqrt(var + LN_EPS) * w + b


def _rope_ret(x, c, s):
    return x * c + pltpu.roll(x, RET_DK // 2, axis=1) * s


def _rope_swa(x, c, s1, s2):
    half = SWA_ROT_DIM // 2
    return x * c + pltpu.roll(x, half, axis=1) * s1 + pltpu.roll(x, LANES - half, axis=1) * s2


def _retention_head(q, k, v, g, s_prev, dmask, qdec, kdec, gt, gnw):
    qb = q.astype(BF16)
    kb = k.astype(BF16)
    vb = v.astype(BF16)
    scores = _dot_nt(qb, kb) * dmask
    o = _dot(scores.astype(BF16), vb)
    o = o + _dot(qb, s_prev.astype(BF16)) * qdec
    s_new = gt * s_prev + _dot_tn((k * kdec).astype(BF16), vb)
    mu = jnp.mean(o, axis=-1, keepdims=True)
    oc = o - mu
    var = jnp.mean(oc * oc, axis=-1, keepdims=True)
    on = oc * lax.rsqrt(var + GN_EPS) * gnw
    return jax.nn.silu(g) * on, s_new


def _kv_variants(a):
    lane = lax.broadcasted_iota(jnp.int32, a.shape, 1)
    lo = lane < SWA_HD
    ar = pltpu.roll(a, SWA_HD, axis=1)
    zero = jnp.zeros_like(a)
    return (jnp.where(lo, a, zero).astype(BF16), jnp.where(lo, zero, ar).astype(BF16),
            jnp.where(lo, ar, zero).astype(BF16), jnp.where(lo, zero, a).astype(BF16))


def _swa_softmax(s, nk, sink_e, sink_o, valid):
    if valid is not None:
        s = jnp.where(valid, s, NEG_INF)
    col = lax.broadcasted_iota(jnp.int32, s.shape, 1)
    is_o = col >= nk
    m_e = jnp.maximum(jnp.max(jnp.where(is_o, -jnp.inf, s), axis=-1, keepdims=True), sink_e)
    m_o = jnp.maximum(jnp.max(jnp.where(is_o, s, -jnp.inf), axis=-1, keepdims=True), sink_o)
    e = jnp.exp(s - jnp.where(is_o, m_o, m_e))
    d_e = jnp.sum(jnp.where(is_o, 0.0, e), axis=-1, keepdims=True) + jnp.exp(sink_e - m_e)
    d_o = jnp.sum(jnp.where(is_o, e, 0.0), axis=-1, keepdims=True) + jnp.exp(sink_o - m_o)
    lane = lax.broadcasted_iota(jnp.int32, (s.shape[0], LANES), 1)
    return e.astype(BF16), jnp.where(lane < SWA_HD, d_e, d_o)


def _swa_pair(qp, kt, vt, nk, sink_e, sink_o, valid):
    e, den = _swa_softmax(_dot_nt(qp, kt), nk, sink_e, sink_o, valid)
    return _dot(e, vt) / den


def _prompt_mixer_kernel(sinks_ref, x_ref, xn_ref, tab_ref, dmask_ref, qdec_ref, kdec_ref, gt_ref, w_in_ref,
                         gnw_ref, w_out_ref, ln1w_ref, ln1b_ref,
                         x1_ref, kout_ref, vout_ref, sout_ref,
                         pa_scr, pb_scr, xb_scr, s_scr, kext_scr, vext_scr, mix_scr, sw_s_scr, sw_e_scr, sw_d_scr,
                         *, steps_per_seq):
    tt = PROMPT_TILE
    g = pl.program_id(0)
    seq_step = g % steps_per_seq

    @pl.when(seq_step == 0)
    def _():
        s_scr[...] = jnp.zeros_like(s_scr)
        kext_scr[0:SWA_WINDOW, :] = jnp.zeros((SWA_WINDOW, KV_W), F32)
        vext_scr[0:SWA_WINDOW, :] = jnp.zeros((SWA_WINDOW, KV_W), F32)

    @pl.when(g == 0)
    def _():
        pa_scr[...] = _dot(x_ref[0, 0:tt, :].astype(BF16), w_in_ref[...])

    def project_next_slab(p_nxt, c):
        cols = slice(c * MXU_COLS, (c + 1) * MXU_COLS)
        p_nxt[:, cols] = _dot(xb_scr[...], w_in_ref[:, cols])

    def tile(r0x, p_cur, p_nxt, load_next_x, seq_start):
        rows_x = slice(r0x, r0x + tt)
        tab = lambda i: tab_ref[rows_x, i * LANES:(i + 1) * LANES]
        xb_scr[...] = load_next_x().astype(BF16)

        for h in range(RET_HEADS):
            sl = slice(h * LANES, (h + 1) * LANES)
            q = _rope_ret(p_cur[:, OFF_RQ + h * LANES:OFF_RQ + (h + 1) * LANES], tab(T_RQC), tab(T_RQS))
            k = _rope_ret(p_cur[:, OFF_RK + h * LANES:OFF_RK + (h + 1) * LANES], tab(T_RKC), tab(T_RKS))
            v = p_cur[:, OFF_RV + h * LANES:OFF_RV + (h + 1) * LANES]
            gate = p_cur[:, OFF_RG + h * LANES:OFF_RG + (h + 1) * LANES]
            out, s_new = _retention_head(q, k, v, gate, s_scr[h], dmask_ref[h], qdec_ref[:, sl], kdec_ref[:, sl],
                                         gt_ref[:, sl], gnw_ref[:, sl])
            s_scr[h] = s_new
            mix_scr[:, sl] = out.astype(BF16)

        k_new = _rope_swa(p_cur[:, OFF_SK:OFF_SK + KV_W], tab(T_SKC), tab(T_SK1), tab(T_SK2))
        v_new = p_cur[:, OFF_SV:OFF_SV + KV_W]
        kext_scr[SWA_WINDOW:SWA_WINDOW + tt, :] = k_new
        vext_scr[SWA_WINDOW:SWA_WINDOW + tt, :] = v_new
        k_var = _kv_variants(kext_scr[...])
        v_var = _kv_variants(vext_scr[...])
        nk = SWA_WINDOW + CHUNK
        n_chunks = tt // CHUNK
        qs = [_rope_swa(p_cur[:, OFF_SQ + m * LANES:OFF_SQ + (m + 1) * LANES],
                        tab(T_SQC), tab(T_SQ1), tab(T_SQ2)).astype(BF16)
              for m in range(N_PAIRS)]
        col = lax.broadcasted_iota(jnp.int32, (PAIRS_PER_KV * CHUNK, 2 * nk), 1)
        col = jnp.where(col >= nk, col - nk, col)
        row = lax.broadcasted_iota(jnp.int32, (PAIRS_PER_KV * CHUNK, 1), 0)
        blocks = [(grp, i) for grp in range(SWA_KV_HEADS) for i in range(n_chunks)]
        rows_of = lambda n: slice(n * PAIRS_PER_KV * CHUNK, (n + 1) * PAIRS_PER_KV * CHUNK)
        for n, (grp, i) in enumerate(blocks):
            r0 = i * CHUNK
            q_blk = jnp.concatenate([qs[grp * PAIRS_PER_KV + a][r0:r0 + CHUNK] for a in range(PAIRS_PER_KV)], axis=0)
            kt = jnp.concatenate([k_var[2 * grp][r0:r0 + nk], k_var[2 * grp + 1][r0:r0 + nk]], axis=0)
            sw_s_scr[rows_of(n), :] = _dot_nt(q_blk, kt)
        slab = 0
        for n, (grp, i) in enumerate(blocks):
            sink_e = sinks_ref[2 * grp * PAIRS_PER_KV]
            sink_o = sinks_ref[2 * grp * PAIRS_PER_KV + 1]
            for a in range(1, PAIRS_PER_KV):
                sink_e = jnp.where(row >= a * CHUNK, sinks_ref[2 * (grp * PAIRS_PER_KV + a)], sink_e)
                sink_o = jnp.where(row >= a * CHUNK, sinks_ref[2 * (grp * PAIRS_PER_KV + a) + 1], sink_o)
            valid = None
            if seq_start is not None and i < SWA_WINDOW // CHUNK:
                valid = jnp.logical_or(col >= SWA_WINDOW - i * CHUNK, jnp.logical_not(seq_start))
            e, den = _swa_softmax(sw_s_scr[rows_of(n), :], nk, sink_e, sink_o, valid)
            sw_e_scr[rows_of(n), :] = e
            sw_d_scr[rows_of(n), :] = den
            if slab < N_SLABS - (LN_BLOCKS - 1):
                project_next_slab(p_nxt, slab)
                slab += 1
        for n, (grp, i) in enumerate(blocks):
            r0 = i * CHUNK
            vt = jnp.concatenate([v_var[2 * grp][r0:r0 + nk], v_var[2 * grp + 1][r0:r0 + nk]], axis=0)
            o = (_dot(sw_e_scr[rows_of(n), :], vt) / sw_d_scr[rows_of(n), :]).astype(BF16)
            for a in range(PAIRS_PER_KV):
                m = grp * PAIRS_PER_KV + a
                mix_scr[r0:r0 + CHUNK, RET_W + m * LANES:RET_W + (m + 1) * LANES] = o[a * CHUNK:(a + 1) * CHUNK]

        kext_scr[0:SWA_WINDOW, :] = k_new[tt - SWA_WINDOW:, :]
        vext_scr[0:SWA_WINDOW, :] = v_new[tt - SWA_WINDOW:, :]

        x1_ref[0, rows_x, :] = ALPHA * x_ref[0, rows_x, :] + _dot(mix_scr[...], w_out_ref[...])
        lb = tt // LN_BLOCKS
        for r in range(LN_BLOCKS):
            rows_ln = slice(r0x + r * lb, r0x + (r + 1) * lb)
            x1_ref[0, rows_ln, :] = _layer_norm(x1_ref[0, rows_ln, :], ln1w_ref[...], ln1b_ref[...])
            if slab < N_SLABS:
                project_next_slab(p_nxt, slab)
                slab += 1
        assert slab == N_SLABS
        return k_new, v_new

    tile(0, pa_scr, pb_scr, lambda: x_ref[0, tt:2 * tt, :], seq_step == 0)
    k_new, v_new = tile(tt, pb_scr, pa_scr, lambda: xn_ref[0], None)

    @pl.when(seq_step == steps_per_seq - 1)
    def _():
        kout_ref[0] = k_new[tt - SWA_WINDOW:, :]
        vout_ref[0] = v_new[tt - SWA_WINDOW:, :]
        sout_ref[0] = s_scr[...]


def _prompt_mixer(x, tab, dmask, qdec, kdec, gt, sinks, w_in, gnw, w_out, ln1w, ln1b):
    b, t, d = x.shape
    tt = PROMPT_TILE
    sps = t // (2 * tt)
    n_steps = b * sps
    const2 = lambda g: (0, 0)
    const3 = lambda g: (0, 0, 0)

    def next_tile(g):
        gn = jnp.minimum(g + 1, n_steps - 1)
        return (gn // sps, (gn % sps) * 2, 0)

    resident = dict(pipeline_mode=pl.Buffered(1))
    return pl.pallas_call(
        functools.partial(_prompt_mixer_kernel, steps_per_seq=sps),
        grid=(n_steps,),
        in_specs=[
            pl.BlockSpec(memory_space=pltpu.SMEM),
            pl.BlockSpec((1, 2 * tt, d), lambda g: (g // sps, g % sps, 0)),
            pl.BlockSpec((1, tt, d), next_tile),
            pl.BlockSpec((2 * tt, N_TABS * LANES), lambda g: (g % sps, 0)),
            pl.BlockSpec((RET_HEADS, tt, tt), const3, **resident),
            pl.BlockSpec((tt, RET_W), const2, **resident),
            pl.BlockSpec((tt, RET_W), const2, **resident),
            pl.BlockSpec((1, RET_W), const2),
            pl.BlockSpec((d, PROJ_W), const2, **resident),
            pl.BlockSpec((1, RET_W), const2),
            pl.BlockSpec((MIX_W, d), const2, **resident),
            pl.BlockSpec((1, d), const2),
            pl.BlockSpec((1, d), const2),
        ],
        out_specs=[
            pl.BlockSpec((1, 2 * tt, d), lambda g: (g // sps, g % sps, 0)),
            pl.BlockSpec((1, SWA_WINDOW, KV_W), lambda g: (g // sps, 0, 0)),
            pl.BlockSpec((1, SWA_WINDOW, KV_W), lambda g: (g // sps, 0, 0)),
            pl.BlockSpec((1, RET_HEADS, RET_DK, RET_DV), lambda g: (g // sps, 0, 0, 0)),
        ],
        out_shape=[
            jax.ShapeDtypeStruct((b, t, d), F32),
            jax.ShapeDtypeStruct((b, SWA_WINDOW, KV_W), F32),
            jax.ShapeDtypeStruct((b, SWA_WINDOW, KV_W), F32),
            jax.ShapeDtypeStruct((b, RET_HEADS, RET_DK, RET_DV), F32),
        ],
        scratch_shapes=[
            pltpu.VMEM((tt, PROJ_W), F32),
            pltpu.VMEM((tt, PROJ_W), F32),
            pltpu.VMEM((tt, d), BF16),
            pltpu.VMEM((RET_HEADS, RET_DK, RET_DV), F32),
            pltpu.VMEM((SWA_WINDOW + tt, KV_W), F32),
            pltpu.VMEM((SWA_WINDOW + tt, KV_W), F32),
            pltpu.VMEM((tt, MIX_W), BF16),
            pltpu.VMEM((N_PAIRS * tt, 2 * (SWA_WINDOW + CHUNK)), F32),
            pltpu.VMEM((N_PAIRS * tt, 2 * (SWA_WINDOW + CHUNK)), BF16),
            pltpu.VMEM((N_PAIRS * tt, LANES), F32),
        ],
        compiler_params=pltpu.CompilerParams(
            dimension_semantics=("arbitrary",), vmem_limit_bytes=VMEM_LIMIT),
        name="prompt_mixer",
    )(sinks, x, x, tab, dmask, qdec, kdec, gt, w_in, gnw, w_out, ln1w, ln1b)


def _sample_mixer_kernel(sinks_ref, x_ref, tab_ref, dmask_ref, qdec_ref, kdec_ref, gt_ref, ck_ref, cv_ref,
                         state_ref, w_in_ref, gnw_ref, w_out_ref, ln1w_ref, ln1b_ref,
                         x1_ref, kout_ref, vout_ref, sout_ref,
                         p_scr, mix_scr, *, n_new):
    s_idx = pl.program_id(0)
    last_s = pl.num_programs(0) - 1

    @pl.when(s_idx == 0)
    def _():
        p_scr[...] = _dot(x_ref[...].astype(BF16), w_in_ref[...])

    rows = pl.ds(pl.multiple_of(s_idx * n_new, n_new), n_new)

    for h in range(RET_HEADS):
        sl = slice(h * LANES, (h + 1) * LANES)
        q = _rope_ret(p_scr[rows, OFF_RQ + h * LANES:OFF_RQ + (h + 1) * LANES], _tab(tab_ref, T_RQC), _tab(tab_ref, T_RQS))
        k = _rope_ret(p_scr[rows, OFF_RK + h * LANES:OFF_RK + (h + 1) * LANES], _tab(tab_ref, T_RKC), _tab(tab_ref, T_RKS))
        v = p_scr[rows, OFF_RV + h * LANES:OFF_RV + (h + 1) * LANES]
        g = p_scr[rows, OFF_RG + h * LANES:OFF_RG + (h + 1) * LANES]
        out, s_new = _retention_head(q, k, v, g, state_ref[0, h], dmask_ref[h], qdec_ref[:, sl], kdec_ref[:, sl],
                                     gt_ref[:, sl], gnw_ref[:, sl])
        sout_ref[0, h] = s_new
        mix_scr[rows, sl] = out.astype(BF16)

    k_new = _rope_swa(p_scr[rows, OFF_SK:OFF_SK + KV_W], _tab(tab_ref, T_SKC), _tab(tab_ref, T_SK1), _tab(tab_ref, T_SK2))
    v_new = p_scr[rows, OFF_SV:OFF_SV + KV_W]
    kout_ref[0] = k_new
    vout_ref[0] = v_new
    k_var = _kv_variants(jnp.concatenate([ck_ref[0], k_new], axis=0))
    v_var = _kv_variants(jnp.concatenate([cv_ref[0], v_new], axis=0))
    nk = ck_ref.shape[1] + n_new
    for m in range(N_PAIRS):
        qs = p_scr[rows, OFF_SQ + m * LANES:OFF_SQ + (m + 1) * LANES]
        qs = _rope_swa(qs, _tab(tab_ref, T_SQC), _tab(tab_ref, T_SQ1), _tab(tab_ref, T_SQ2)).astype(BF16)
        grp = (2 * m) // (SWA_HEADS // SWA_KV_HEADS)
        kt = jnp.concatenate([k_var[2 * grp], k_var[2 * grp + 1]], axis=0)
        vt = jnp.concatenate([v_var[2 * grp], v_var[2 * grp + 1]], axis=0)
        o = _swa_pair(qs, kt, vt, nk, sinks_ref[2 * m], sinks_ref[2 * m + 1], None)
        mix_scr[rows, RET_W + m * LANES:RET_W + (m + 1) * LANES] = o.astype(BF16)

    @pl.when(s_idx == last_s)
    def _():
        z = ALPHA * x_ref[...] + _dot(mix_scr[...], w_out_ref[...])
        x1_ref[...] = _layer_norm(z, ln1w_ref[...], ln1b_ref[...])


def _sample_mixer(x, tab, dmask, qdec, kdec, gt, cache_k, cache_v, state, sinks, w_in, gnw, w_out, ln1w, ln1b):
    nb, n_new, d = x.shape
    rows = nb * n_new
    win = cache_k.shape[1]
    const2 = lambda si: (0, 0)
    const3 = lambda si: (0, 0, 0)
    return pl.pallas_call(
        functools.partial(_sample_mixer_kernel, n_new=n_new),
        grid=(nb,),
        in_specs=[
            pl.BlockSpec(memory_space=pltpu.SMEM),
            pl.BlockSpec((rows, d), const2),
            pl.BlockSpec((n_new, N_TABS * LANES), const2),
            pl.BlockSpec((RET_HEADS, n_new, n_new), const3),
            pl.BlockSpec((n_new, RET_W), const2),
            pl.BlockSpec((n_new, RET_W), const2),
            pl.BlockSpec((1, RET_W), const2),
            pl.BlockSpec((1, win, KV_W), lambda si: (si, 0, 0)),
            pl.BlockSpec((1, win, KV_W), lambda si: (si, 0, 0)),
            pl.BlockSpec((1, RET_HEADS, RET_DK, RET_DV), lambda si: (si, 0, 0, 0)),
            pl.BlockSpec((d, PROJ_W), const2),
            pl.BlockSpec((1, RET_W), const2),
            pl.BlockSpec((MIX_W, d), const2),
            pl.BlockSpec((1, d), const2),
            pl.BlockSpec((1, d), const2),
        ],
        out_specs=[
            pl.BlockSpec((rows, d), const2),
            pl.BlockSpec((1, n_new, KV_W), lambda si: (si, 0, 0)),
            pl.BlockSpec((1, n_new, KV_W), lambda si: (si, 0, 0)),
            pl.BlockSpec((1, RET_HEADS, RET_DK, RET_DV), lambda si: (si, 0, 0, 0)),
        ],
        out_shape=[
            jax.ShapeDtypeStruct((rows, d), F32),
            jax.ShapeDtypeStruct((nb, n_new, KV_W), F32),
            jax.ShapeDtypeStruct((nb, n_new, KV_W), F32),
            jax.ShapeDtypeStruct((nb, RET_HEADS, RET_DK, RET_DV), F32),
        ],
        scratch_shapes=[
            pltpu.VMEM((rows, PROJ_W), F32),
            pltpu.VMEM((rows, MIX_W), BF16),
        ],
        compiler_params=pltpu.CompilerParams(
            dimension_semantics=("arbitrary",), vmem_limit_bytes=VMEM_LIMIT),
        name="sample_mixer",
    )(sinks, x.reshape(rows, d), tab, dmask, qdec, kdec, gt, cache_k, cache_v, state, w_in, gnw, w_out, ln1w, ln1b)


def _mlp_kernel(xp_ref, xs_ref, w_up_ref, w_down_ref, lnw_ref, lnb_ref, yp_ref, ys_ref, *, prompt_steps):
    i = pl.program_id(0)

    def block(x_ref, y_ref):
        x = x_ref[...]
        h = _dot(x.astype(BF16), w_up_ref[...])
        h = jnp.square(jnp.maximum(h, 0.0)).astype(BF16)
        z = ALPHA * x + _dot(h, w_down_ref[...])
        y_ref[...] = _layer_norm(z, lnw_ref[...], lnb_ref[...])

    @pl.when(i < prompt_steps)
    def _():
        block(xp_ref, yp_ref)

    @pl.when(i == prompt_steps)
    def _():
        block(xs_ref, ys_ref)


def _mlp(xp, xs, w_up, w_down, lnw, lnb):
    rows, d = xp.shape
    rows_s = xs.shape[0]
    steps = rows // MLP_TILE
    const2 = lambda i: (0, 0)
    prompt_tile = lambda i: (jnp.minimum(i, steps - 1), 0)
    resident = dict(pipeline_mode=pl.Buffered(1))
    return pl.pallas_call(
        functools.partial(_mlp_kernel, prompt_steps=steps),
        grid=(steps + 1,),
        in_specs=[
            pl.BlockSpec((MLP_TILE, d), prompt_tile),
            pl.BlockSpec((rows_s, d), const2, **resident),
            pl.BlockSpec((d, D_FF), const2, **resident),
            pl.BlockSpec((D_FF, d), const2, **resident),
            pl.BlockSpec((1, d), const2),
            pl.BlockSpec((1, d), const2),
        ],
        out_specs=[
            pl.BlockSpec((MLP_TILE, d), prompt_tile),
            pl.BlockSpec((rows_s, d), const2),
        ],
        out_shape=[
            jax.ShapeDtypeStruct((rows, d), F32),
            jax.ShapeDtypeStruct((rows_s, d), F32),
        ],
        compiler_params=pltpu.CompilerParams(
            dimension_semantics=("arbitrary",), vmem_limit_bytes=VMEM_LIMIT),
        name="mlp",
    )(xp, xs, w_up, w_down, lnw, lnb)


def _rope_tables(pos):
    posf = np.asarray(pos, np.float64)[:, None]
    half_r = RET_DK // 2
    ang = posf * (RET_THETA ** (-np.arange(half_r, dtype=np.float64) / half_r))[None, :]
    c, s = np.cos(ang), np.sin(ang)
    rc = np.concatenate([c, c], axis=-1)
    rs = np.concatenate([-s, s], axis=-1)
    k_scale = RET_DK ** -0.5
    half_s = SWA_ROT_DIM // 2
    ang = posf * (SWA_THETA ** (-np.arange(half_s, dtype=np.float64) / half_s))[None, :]
    c, s = np.cos(ang), np.sin(ang)
    t = posf.shape[0]
    rest = SWA_HD - SWA_ROT_DIM
    z = np.zeros((t, half_s))
    sc = np.tile(np.concatenate([c, c, np.ones((t, rest))], axis=-1), (1, LANES // SWA_HD))
    s1 = np.tile(np.concatenate([z, s, np.zeros((t, rest))], axis=-1), (1, LANES // SWA_HD))
    s2 = np.tile(np.concatenate([-s, z, np.zeros((t, rest))], axis=-1), (1, LANES // SWA_HD))
    q_scale = SWA_HD ** -0.5
    tab = np.concatenate([rc, rs, rc * k_scale, rs * k_scale,
                          sc * q_scale, s1 * q_scale, s2 * q_scale, sc, s1, s2], axis=-1)
    return jnp.asarray(tab.astype(np.float32))


def _decay_tables(t):
    lg = np.log1p(-(2.0 ** (-5.0 - np.arange(RET_HEADS, dtype=np.float64))))
    idx = np.arange(t, dtype=np.float64)
    diff = idx[:, None] - idx[None, :]
    dmask = np.where(diff >= 0, np.exp(lg[:, None, None] * np.maximum(diff, 0.0)), 0.0)
    qdec = np.repeat(np.exp(lg[None, :] * (idx[:, None] + 1.0)), RET_DV, axis=1)
    kdec = np.repeat(np.exp(lg[None, :] * (t - 1.0 - idx[:, None])), RET_DV, axis=1)
    gt = np.repeat(np.exp(lg * t)[None, :], RET_DV, axis=1)
    return tuple(jnp.asarray(a.astype(np.float32)) for a in (dmask, qdec, kdec, gt))


def kernel(x_prompt, x_sample, cache_swa_k, cache_swa_v, state_ret, w_in, ret_gn_w, swa_sinks,
           w_out, ln1_w, ln1_b, w_up, w_down, ln2_w, ln2_b):
    assert w_in.shape[0] == DEPTH == 1
    b, t, d = x_prompt.shape
    nb, n_new, _ = x_sample.shape
    win = cache_swa_k.shape[2]

    w_in_b = w_in[0].astype(BF16)
    w_out_b = w_out[0].astype(BF16)
    w_up_b = w_up[0].astype(BF16)
    w_down_b = w_down[0].astype(BF16)
    sinks = swa_sinks[0]

    tab_p = _rope_tables(np.arange(t))
    tab_s = _rope_tables(PAST_LEN + np.arange(n_new))
    dec_p = _decay_tables(PROMPT_TILE)
    dec_s = _decay_tables(n_new)

    x1_p, k_p, v_p, r_p = _prompt_mixer(x_prompt, tab_p, *dec_p, sinks, w_in_b, ret_gn_w, w_out_b, ln1_w, ln1_b)
    x1_s, k_s, v_s, r_s = _sample_mixer(
        x_sample, tab_s, *dec_s, cache_swa_k[0].reshape(nb, win, KV_W), cache_swa_v[0].reshape(nb, win, KV_W),
        state_ret[0], sinks, w_in_b, ret_gn_w, w_out_b, ln1_w, ln1_b)
    y_p, y_s = _mlp(x1_p.reshape(b * t, d), x1_s, w_up_b, w_down_b, ln2_w, ln2_b)

    return (y_p.reshape(b, t, d), y_s.reshape(nb, n_new, d),
            k_p.reshape(DEPTH, b, SWA_WINDOW, SWA_KV_HEADS, SWA_HD),
            v_p.reshape(DEPTH, b, SWA_WINDOW, SWA_KV_HEADS, SWA_HD),
            r_p.reshape(DEPTH, b, RET_HEADS, RET_DK, RET_DV),
            k_s.reshape(DEPTH, nb, n_new, SWA_KV_HEADS, SWA_HD),
            v_s.reshape(DEPTH, nb, n_new, SWA_KV_HEADS, SWA_HD),
            r_s.reshape(DEPTH, nb, RET_HEADS, RET_DK, RET_DV))
```

```python
import functools

import numpy as np
import jax
import jax.numpy as jnp
from jax import lax
from jax.experimental import pallas as pl
from jax.experimental.pallas import tpu as pltpu

D_MODEL = 1024
DEPTH = 1
PAST_LEN = 2048
CHUNK = 64
RET_HEADS = 4
RET_DK = 128
RET_DV = 128
RET_THETA = 10000.0
RET_W = RET_HEADS * RET_DV
SWA_HEADS = 8
SWA_KV_HEADS = 2
SWA_HD = 64
SWA_WINDOW = 128
SWA_ROT_DIM = SWA_HD // 4
SWA_THETA = 500000.0
SWA_W = SWA_HEADS * SWA_HD
KV_W = SWA_KV_HEADS * SWA_HD
MIX_W = RET_W + SWA_W
D_FF = 4 * D_MODEL
OFF_RQ, OFF_RK, OFF_RV, OFF_RG = 0, RET_W, 2 * RET_W, 3 * RET_W
OFF_SQ = 4 * RET_W
OFF_SK = OFF_SQ + SWA_W
OFF_SV = OFF_SK + KV_W
PROJ_W = OFF_SV + KV_W
ALPHA = (2.0 * DEPTH) ** 0.25
LN_EPS = 1e-5
GN_EPS = 1e-5
NEG_INF = -1e30

LANES = 128
MXU_COLS = 256
N_PAIRS = SWA_W // LANES
PAIRS_PER_KV = N_PAIRS // SWA_KV_HEADS
T_RQC, T_RQS, T_RKC, T_RKS, T_SQC, T_SQ1, T_SQ2, T_SKC, T_SK1, T_SK2 = range(10)
N_TABS = 10

PROMPT_TILE = 256
N_SLABS = PROJ_W // MXU_COLS
LN_BLOCKS = 4
SLABS_AFTER_RET_HEAD = (1, 1, 1, 0)
SLABS_AFTER_SOFTMAX_BLOCK = (1, 1, 1, 1, 1, 1, 0, 0)
SLABS_AFTER_LN_BLOCK = (1, 1, 0, 0)
MLP_TILE = 512
MLP_LN_ROWS = 128
VMEM_LIMIT = 56 * 1024 * 1024

F32 = jnp.float32
BF16 = jnp.bfloat16


def _tab(tab_ref, i):
    return tab_ref[:, i * LANES:(i + 1) * LANES]


def _dot(a, b):
    return jnp.dot(a, b, preferred_element_type=F32)


def _dot_nt(a, b):
    return lax.dot_general(a, b, (((1,), (1,)), ((), ())), preferred_element_type=F32)


def _dot_tn(a, b):
    return lax.dot_general(a, b, (((0,), (0,)), ((), ())), preferred_element_type=F32)


def _layer_norm(z, w, b):
    mu = jnp.mean(z, axis=-1, keepdims=True)
    zc = z - mu
    var = jnp.mean(zc * zc, axis=-1, keepdims=True)
    return zc * lax.rsqrt(var + LN_EPS) * w + b


def _rope_ret(x, c, s):
    return x * c + pltpu.roll(x, RET_DK // 2, axis=1) * s


def _rope_swa(x, c, s1, s2):
    half = SWA_ROT_DIM // 2
    return x * c + pltpu.roll(x, half, axis=1) * s1 + pltpu.roll(x, LANES - half, axis=1) * s2


def _retention_head(q, k, v, g, s_prev, dmask, qdec, kdec, gt, gnw):
    qb = q.astype(BF16)
    kb = k.astype(BF16)
    vb = v.astype(BF16)
    scores = _dot_nt(qb, kb) * dmask
    o = _dot(scores.astype(BF16), vb)
    o = o + _dot(qb, s_prev.astype(BF16)) * qdec
    s_new = gt * s_prev + _dot_tn((k * kdec).astype(BF16), vb)
    mu = jnp.mean(o, axis=-1, keepdims=True)
    oc = o - mu
    var = jnp.mean(oc * oc, axis=-1, keepdims=True)
    on = oc * lax.rsqrt(var + GN_EPS) * gnw
    return jax.nn.silu(g) * on, s_new


def _kv_variants(a):
    lane = lax.broadcasted_iota(jnp.int32, a.shape, 1)
    lo = lane < SWA_HD
    ar = pltpu.roll(a, SWA_HD, axis=1)
    zero = jnp.zeros_like(a)
    return (jnp.where(lo, a, zero).astype(BF16), jnp.where(lo, zero, ar).astype(BF16),
            jnp.where(lo, ar, zero).astype(BF16), jnp.where(lo, zero, a).astype(BF16))


def _pair_rows(a_e, a_o, r0, nk):
    return jnp.concatenate([a_e[r0:r0 + LANES], a_o[r0:r0 + LANES],
                            a_e[r0 + LANES:r0 + nk], a_o[r0 + LANES:r0 + nk]], axis=0)


def _swa_softmax(s, nk, sink_e, sink_o, valid):
    r = nk - LANES
    s_e, s_o, s_x = s[:, :LANES], s[:, LANES:2 * LANES], s[:, 2 * LANES:]
    if valid is not None:
        s_e = jnp.where(valid, s_e, NEG_INF)
        s_o = jnp.where(valid, s_o, NEG_INF)
    x_is_e = lax.broadcasted_iota(jnp.int32, s_x.shape, 1) < r
    rowmax = lambda a: jnp.max(a, axis=-1, keepdims=True)
    rowsum = lambda a: jnp.sum(a, axis=-1, keepdims=True)
    x_e = jnp.where(x_is_e, s_x, -jnp.inf)
    x_o = jnp.where(x_is_e, -jnp.inf, s_x)
    if 2 * r == LANES:
        m_e, m_o = rowmax(jnp.maximum(s_e, x_e)), rowmax(jnp.maximum(s_o, x_o))
    else:
        m_e, m_o = jnp.maximum(rowmax(s_e), rowmax(x_e)), jnp.maximum(rowmax(s_o), rowmax(x_o))
    m_e = jnp.maximum(m_e, sink_e)
    m_o = jnp.maximum(m_o, sink_o)
    e_e = jnp.exp(s_e - m_e)
    e_o = jnp.exp(s_o - m_o)
    e_x = jnp.exp(s_x - jnp.where(x_is_e, m_e, m_o))
    x_e = jnp.where(x_is_e, e_x, 0.0)
    x_o = jnp.where(x_is_e, 0.0, e_x)
    if 2 * r == LANES:
        d_e, d_o = rowsum(e_e + x_e), rowsum(e_o + x_o)
    else:
        d_e, d_o = rowsum(e_e) + rowsum(x_e), rowsum(e_o) + rowsum(x_o)
    d_e = d_e + jnp.exp(sink_e - m_e)
    d_o = d_o + jnp.exp(sink_o - m_o)
    lane = lax.broadcasted_iota(jnp.int32, (s.shape[0], LANES), 1)
    e = jnp.concatenate([e_e, e_o, e_x], axis=1).astype(BF16)
    return e, jnp.where(lane < SWA_HD, d_e, d_o)


def _swa_pair(qp, kt, vt, nk, sink_e, sink_o):
    e, den = _swa_softmax(_dot_nt(qp, kt), nk, sink_e, sink_o, None)
    return _dot(e, vt) / den


def _prompt_mixer_kernel(sinks_ref, x_ref, xn_ref, tab_ref, dmask_ref, qdec_ref, kdec_ref, gt_ref, w_in_ref,
                         gnw_ref, w_out_ref, ln1w_ref, ln1b_ref,
                         x1_ref, kout_ref, vout_ref, sout_ref,
                         pa_scr, pb_scr, xb_scr, s_scr, kext_scr, vext_scr, mix_scr, sw_s_scr, sw_e_scr, sw_d_scr,
                         *, steps_per_seq):
    tt = PROMPT_TILE
    g = pl.program_id(0)
    seq_step = g % steps_per_seq

    @pl.when(seq_step == 0)
    def _():
        s_scr[...] = jnp.zeros_like(s_scr)
        kext_scr[0:SWA_WINDOW, :] = jnp.zeros((SWA_WINDOW, KV_W), F32)
        vext_scr[0:SWA_WINDOW, :] = jnp.zeros((SWA_WINDOW, KV_W), F32)

    @pl.when(g == 0)
    def _():
        pa_scr[...] = _dot(x_ref[0, 0:tt, :].astype(BF16), w_in_ref[...])

    def project_next_slabs(p_nxt, slabs, count):
        for _ in range(count):
            c = next(slabs)
            cols = slice(c * MXU_COLS, (c + 1) * MXU_COLS)
            p_nxt[:, cols] = _dot(xb_scr[...], w_in_ref[:, cols])

    def tile(r0x, p_cur, p_nxt, load_next_x, seq_start):
        rows_x = slice(r0x, r0x + tt)
        tab = lambda i: tab_ref[rows_x, i * LANES:(i + 1) * LANES]
        xb_scr[...] = load_next_x().astype(BF16)
        slabs = iter(range(N_SLABS))

        for h in range(RET_HEADS):
            sl = slice(h * LANES, (h + 1) * LANES)
            q = _rope_ret(p_cur[:, OFF_RQ + h * LANES:OFF_RQ + (h + 1) * LANES], tab(T_RQC), tab(T_RQS))
            k = _rope_ret(p_cur[:, OFF_RK + h * LANES:OFF_RK + (h + 1) * LANES], tab(T_RKC), tab(T_RKS))
            v = p_cur[:, OFF_RV + h * LANES:OFF_RV + (h + 1) * LANES]
            gate = p_cur[:, OFF_RG + h * LANES:OFF_RG + (h + 1) * LANES]
            out, s_new = _retention_head(q, k, v, gate, s_scr[h], dmask_ref[h], qdec_ref[:, sl], kdec_ref[:, sl],
                                         gt_ref[:, sl], gnw_ref[:, sl])
            s_scr[h] = s_new
            mix_scr[:, sl] = out.astype(BF16)
            project_next_slabs(p_nxt, slabs, SLABS_AFTER_RET_HEAD[h])

        k_new = _rope_swa(p_cur[:, OFF_SK:OFF_SK + KV_W], tab(T_SKC), tab(T_SK1), tab(T_SK2))
        v_new = p_cur[:, OFF_SV:OFF_SV + KV_W]
        kext_scr[SWA_WINDOW:SWA_WINDOW + tt, :] = k_new
        vext_scr[SWA_WINDOW:SWA_WINDOW + tt, :] = v_new
        k_var = _kv_variants(kext_scr[...])
        v_var = _kv_variants(vext_scr[...])
        nk = SWA_WINDOW + CHUNK
        n_chunks = tt // CHUNK
        qs = [_rope_swa(p_cur[:, OFF_SQ + m * LANES:OFF_SQ + (m + 1) * LANES],
                        tab(T_SQC), tab(T_SQ1), tab(T_SQ2)).astype(BF16)
              for m in range(N_PAIRS)]
        col = lax.broadcasted_iota(jnp.int32, (PAIRS_PER_KV * CHUNK, LANES), 1)
        row = lax.broadcasted_iota(jnp.int32, (PAIRS_PER_KV * CHUNK, 1), 0)
        blocks = [(grp, i) for grp in range(SWA_KV_HEADS) for i in range(n_chunks)]
        rows_of = lambda n: slice(n * PAIRS_PER_KV * CHUNK, (n + 1) * PAIRS_PER_KV * CHUNK)
        for n, (grp, i) in enumerate(blocks):
            r0 = i * CHUNK
            q_blk = jnp.concatenate([qs[grp * PAIRS_PER_KV + a][r0:r0 + CHUNK] for a in range(PAIRS_PER_KV)], axis=0)
            sw_s_scr[rows_of(n), :] = _dot_nt(q_blk, _pair_rows(k_var[2 * grp], k_var[2 * grp + 1], r0, nk))
        for n, (grp, i) in enumerate(blocks):
            sink_e = sinks_ref[2 * grp * PAIRS_PER_KV]
            sink_o = sinks_ref[2 * grp * PAIRS_PER_KV + 1]
            for a in range(1, PAIRS_PER_KV):
                sink_e = jnp.where(row >= a * CHUNK, sinks_ref[2 * (grp * PAIRS_PER_KV + a)], sink_e)
                sink_o = jnp.where(row >= a * CHUNK, sinks_ref[2 * (grp * PAIRS_PER_KV + a) + 1], sink_o)
            valid = None
            if seq_start is not None and i < SWA_WINDOW // CHUNK:
                valid = jnp.logical_or(col >= SWA_WINDOW - i * CHUNK, jnp.logical_not(seq_start))
            e, den = _swa_softmax(sw_s_scr[rows_of(n), :], nk, sink_e, sink_o, valid)
            sw_e_scr[rows_of(n), :] = e
            sw_d_scr[rows_of(n), :] = den
            project_next_slabs(p_nxt, slabs, SLABS_AFTER_SOFTMAX_BLOCK[n])
        for n, (grp, i) in enumerate(blocks):
            r0 = i * CHUNK
            vt = _pair_rows(v_var[2 * grp], v_var[2 * grp + 1], r0, nk)
            o = (_dot(sw_e_scr[rows_of(n), :], vt) / sw_d_scr[rows_of(n), :]).astype(BF16)
            for a in range(PAIRS_PER_KV):
                m = grp * PAIRS_PER_KV + a
                mix_scr[r0:r0 + CHUNK, RET_W + m * LANES:RET_W + (m + 1) * LANES] = o[a * CHUNK:(a + 1) * CHUNK]

        kext_scr[0:SWA_WINDOW, :] = k_new[tt - SWA_WINDOW:, :]
        vext_scr[0:SWA_WINDOW, :] = v_new[tt - SWA_WINDOW:, :]

        x1_ref[0, rows_x, :] = ALPHA * x_ref[0, rows_x, :] + _dot(mix_scr[...], w_out_ref[...])
        lb = tt // LN_BLOCKS
        for r in range(LN_BLOCKS):
            rows_ln = slice(r0x + r * lb, r0x + (r + 1) * lb)
            x1_ref[0, rows_ln, :] = _layer_norm(x1_ref[0, rows_ln, :], ln1w_ref[...], ln1b_ref[...])
            project_next_slabs(p_nxt, slabs, SLABS_AFTER_LN_BLOCK[r])
        assert next(slabs, None) is None
        return k_new, v_new

    tile(0, pa_scr, pb_scr, lambda: x_ref[0, tt:2 * tt, :], seq_step == 0)
    k_new, v_new = tile(tt, pb_scr, pa_scr, lambda: xn_ref[0], None)

    @pl.when(seq_step == steps_per_seq - 1)
    def _():
        kout_ref[0] = k_new[tt - SWA_WINDOW:, :]
        vout_ref[0] = v_new[tt - SWA_WINDOW:, :]
        sout_ref[0] = s_scr[...]


def _prompt_mixer(x, tab, dmask, qdec, kdec, gt, sinks, w_in, gnw, w_out, ln1w, ln1b):
    b, t, d = x.shape
    tt = PROMPT_TILE
    sps = t // (2 * tt)
    n_steps = b * sps
    const2 = lambda g: (0, 0)
    const3 = lambda g: (0, 0, 0)

    def next_tile(g):
        gn = jnp.minimum(g + 1, n_steps - 1)
        return (gn // sps, (gn % sps) * 2, 0)

    resident = dict(pipeline_mode=pl.Buffered(1))
    return pl.pallas_call(
        functools.partial(_prompt_mixer_kernel, steps_per_seq=sps),
        grid=(n_steps,),
        in_specs=[
            pl.BlockSpec(memory_space=pltpu.SMEM),
            pl.BlockSpec((1, 2 * tt, d), lambda g: (g // sps, g % sps, 0)),
            pl.BlockSpec((1, tt, d), next_tile),
            pl.BlockSpec((2 * tt, N_TABS * LANES), lambda g: (g % sps, 0)),
            pl.BlockSpec((RET_HEADS, tt, tt), const3, **resident),
            pl.BlockSpec((tt, RET_W), const2, **resident),
            pl.BlockSpec((tt, RET_W), const2, **resident),
            pl.BlockSpec((1, RET_W), const2),
            pl.BlockSpec((d, PROJ_W), const2, **resident),
            pl.BlockSpec((1, RET_W), const2),
            pl.BlockSpec((MIX_W, d), const2, **resident),
            pl.BlockSpec((1, d), const2),
            pl.BlockSpec((1, d), const2),
        ],
        out_specs=[
            pl.BlockSpec((1, 2 * tt, d), lambda g: (g // sps, g % sps, 0)),
            pl.BlockSpec((1, SWA_WINDOW, KV_W), lambda g: (g // sps, 0, 0)),
            pl.BlockSpec((1, SWA_WINDOW, KV_W), lambda g: (g // sps, 0, 0)),
            pl.BlockSpec((1, RET_HEADS, RET_DK, RET_DV), lambda g: (g // sps, 0, 0, 0)),
        ],
        out_shape=[
            jax.ShapeDtypeStruct((b, t, d), F32),
            jax.ShapeDtypeStruct((b, SWA_WINDOW, KV_W), F32),
            jax.ShapeDtypeStruct((b, SWA_WINDOW, KV_W), F32),
            jax.ShapeDtypeStruct((b, RET_HEADS, RET_DK, RET_DV), F32),
        ],
        scratch_shapes=[
            pltpu.VMEM((tt, PROJ_W), F32),
            pltpu.VMEM((tt, PROJ_W), F32),
            pltpu.VMEM((tt, d), BF16),
            pltpu.VMEM((RET_HEADS, RET_DK, RET_DV), F32),
            pltpu.VMEM((SWA_WINDOW + tt, KV_W), F32),
            pltpu.VMEM((SWA_WINDOW + tt, KV_W), F32),
            pltpu.VMEM((tt, MIX_W), BF16),
            pltpu.VMEM((N_PAIRS * tt, 2 * (SWA_WINDOW + CHUNK)), F32),
            pltpu.VMEM((N_PAIRS * tt, 2 * (SWA_WINDOW + CHUNK)), BF16),
            pltpu.VMEM((N_PAIRS * tt, LANES), F32),
        ],
        compiler_params=pltpu.CompilerParams(
            dimension_semantics=("arbitrary",), vmem_limit_bytes=VMEM_LIMIT),
        name="prompt_mixer",
    )(sinks, x, x, tab, dmask, qdec, kdec, gt, w_in, gnw, w_out, ln1w, ln1b)


def _sample_mixer_kernel(sinks_ref, x_ref, tab_ref, dmask_ref, qdec_ref, kdec_ref, gt_ref, ck_ref, cv_ref,
                         state_ref, w_in_ref, gnw_ref, w_out_ref, ln1w_ref, ln1b_ref,
                         x1_ref, kout_ref, vout_ref, sout_ref,
                         p_scr, mix_scr, *, n_new):
    s_idx = pl.program_id(0)
    last_s = pl.num_programs(0) - 1

    @pl.when(s_idx == 0)
    def _():
        p_scr[...] = _dot(x_ref[...].astype(BF16), w_in_ref[...])

    rows = pl.ds(pl.multiple_of(s_idx * n_new, n_new), n_new)

    for h in range(RET_HEADS):
        sl = slice(h * LANES, (h + 1) * LANES)
        q = _rope_ret(p_scr[rows, OFF_RQ + h * LANES:OFF_RQ + (h + 1) * LANES], _tab(tab_ref, T_RQC), _tab(tab_ref, T_RQS))
        k = _rope_ret(p_scr[rows, OFF_RK + h * LANES:OFF_RK + (h + 1) * LANES], _tab(tab_ref, T_RKC), _tab(tab_ref, T_RKS))
        v = p_scr[rows, OFF_RV + h * LANES:OFF_RV + (h + 1) * LANES]
        g = p_scr[rows, OFF_RG + h * LANES:OFF_RG + (h + 1) * LANES]
        out, s_new = _retention_head(q, k, v, g, state_ref[0, h], dmask_ref[h], qdec_ref[:, sl], kdec_ref[:, sl],
                                     gt_ref[:, sl], gnw_ref[:, sl])
        sout_ref[0, h] = s_new
        mix_scr[rows, sl] = out.astype(BF16)

    k_new = _rope_swa(p_scr[rows, OFF_SK:OFF_SK + KV_W], _tab(tab_ref, T_SKC), _tab(tab_ref, T_SK1), _tab(tab_ref, T_SK2))
    v_new = p_scr[rows, OFF_SV:OFF_SV + KV_W]
    kout_ref[0] = k_new
    vout_ref[0] = v_new
    k_var = _kv_variants(jnp.concatenate([ck_ref[0], k_new], axis=0))
    v_var = _kv_variants(jnp.concatenate([cv_ref[0], v_new], axis=0))
    nk = ck_ref.shape[1] + n_new
    for m in range(N_PAIRS):
        qs = p_scr[rows, OFF_SQ + m * LANES:OFF_SQ + (m + 1) * LANES]
        qs = _rope_swa(qs, _tab(tab_ref, T_SQC), _tab(tab_ref, T_SQ1), _tab(tab_ref, T_SQ2)).astype(BF16)
        grp = (2 * m) // (SWA_HEADS // SWA_KV_HEADS)
        kt = _pair_rows(k_var[2 * grp], k_var[2 * grp + 1], 0, nk)
        vt = _pair_rows(v_var[2 * grp], v_var[2 * grp + 1], 0, nk)
        o = _swa_pair(qs, kt, vt, nk, sinks_ref[2 * m], sinks_ref[2 * m + 1])
        mix_scr[rows, RET_W + m * LANES:RET_W + (m + 1) * LANES] = o.astype(BF16)

    @pl.when(s_idx == last_s)
    def _():
        z = ALPHA * x_ref[...] + _dot(mix_scr[...], w_out_ref[...])
        x1_ref[...] = _layer_norm(z, ln1w_ref[...], ln1b_ref[...])


def _sample_mixer(x, tab, dmask, qdec, kdec, gt, cache_k, cache_v, state, sinks, w_in, gnw, w_out, ln1w, ln1b):
    nb, n_new, d = x.shape
    rows = nb * n_new
    win = cache_k.shape[1]
    const2 = lambda si: (0, 0)
    const3 = lambda si: (0, 0, 0)
    return pl.pallas_call(
        functools.partial(_sample_mixer_kernel, n_new=n_new),
        grid=(nb,),
        in_specs=[
            pl.BlockSpec(memory_space=pltpu.SMEM),
            pl.BlockSpec((rows, d), const2),
            pl.BlockSpec((n_new, N_TABS * LANES), const2),
            pl.BlockSpec((RET_HEADS, n_new, n_new), const3),
            pl.BlockSpec((n_new, RET_W), const2),
            pl.BlockSpec((n_new, RET_W), const2),
            pl.BlockSpec((1, RET_W), const2),
            pl.BlockSpec((1, win, KV_W), lambda si: (si, 0, 0)),
            pl.BlockSpec((1, win, KV_W), lambda si: (si, 0, 0)),
            pl.BlockSpec((1, RET_HEADS, RET_DK, RET_DV), lambda si: (si, 0, 0, 0)),
            pl.BlockSpec((d, PROJ_W), const2),
            pl.BlockSpec((1, RET_W), const2),
            pl.BlockSpec((MIX_W, d), const2),
            pl.BlockSpec((1, d), const2),
            pl.BlockSpec((1, d), const2),
        ],
        out_specs=[
            pl.BlockSpec((rows, d), const2),
            pl.BlockSpec((1, n_new, KV_W), lambda si: (si, 0, 0)),
            pl.BlockSpec((1, n_new, KV_W), lambda si: (si, 0, 0)),
            pl.BlockSpec((1, RET_HEADS, RET_DK, RET_DV), lambda si: (si, 0, 0, 0)),
        ],
        out_shape=[
            jax.ShapeDtypeStruct((rows, d), F32),
            jax.ShapeDtypeStruct((nb, n_new, KV_W), F32),
            jax.ShapeDtypeStruct((nb, n_new, KV_W), F32),
            jax.ShapeDtypeStruct((nb, RET_HEADS, RET_DK, RET_DV), F32),
        ],
        scratch_shapes=[
            pltpu.VMEM((rows, PROJ_W), F32),
            pltpu.VMEM((rows, MIX_W), BF16),
        ],
        compiler_params=pltpu.CompilerParams(
            dimension_semantics=("arbitrary",), vmem_limit_bytes=VMEM_LIMIT),
        name="sample_mixer",
    )(sinks, x.reshape(rows, d), tab, dmask, qdec, kdec, gt, cache_k, cache_v, state, w_in, gnw, w_out, ln1w, ln1b)


def _mlp_kernel(xp_ref, xs_ref, w_up_ref, w_down_ref, lnw_ref, lnb_ref, yp_ref, ys_ref, h_scr, *, prompt_steps):
    i = pl.program_id(0)

    def block(x_ref, y_ref):
        n = x_ref.shape[0]
        h = _dot(x_ref[...].astype(BF16), w_up_ref[...])
        h_scr[0:n, :] = jnp.square(jnp.maximum(h, 0.0)).astype(BF16)
        for r in range(n // MLP_LN_ROWS):
            rows = slice(r * MLP_LN_ROWS, (r + 1) * MLP_LN_ROWS)
            z = ALPHA * x_ref[rows, :] + _dot(h_scr[rows, :], w_down_ref[...])
            y_ref[rows, :] = _layer_norm(z, lnw_ref[...], lnb_ref[...])

    @pl.when(i < prompt_steps)
    def _():
        block(xp_ref, yp_ref)

    @pl.when(i == prompt_steps)
    def _():
        block(xs_ref, ys_ref)


def _mlp(xp, xs, w_up, w_down, lnw, lnb):
    rows, d = xp.shape
    rows_s = xs.shape[0]
    steps = rows // MLP_TILE
    const2 = lambda i: (0, 0)
    prompt_tile = lambda i: (jnp.minimum(i, steps - 1), 0)
    resident = dict(pipeline_mode=pl.Buffered(1))
    return pl.pallas_call(
        functools.partial(_mlp_kernel, prompt_steps=steps),
        grid=(steps + 1,),
        in_specs=[
            pl.BlockSpec((MLP_TILE, d), prompt_tile),
            pl.BlockSpec((rows_s, d), const2, **resident),
            pl.BlockSpec((d, D_FF), const2, **resident),
            pl.BlockSpec((D_FF, d), const2, **resident),
            pl.BlockSpec((1, d), const2),
            pl.BlockSpec((1, d), const2),
        ],
        out_specs=[
            pl.BlockSpec((MLP_TILE, d), prompt_tile),
            pl.BlockSpec((rows_s, d), const2),
        ],
        out_shape=[
            jax.ShapeDtypeStruct((rows, d), F32),
            jax.ShapeDtypeStruct((rows_s, d), F32),
        ],
        scratch_shapes=[pltpu.VMEM((MLP_TILE, D_FF), BF16)],
        compiler_params=pltpu.CompilerParams(
            dimension_semantics=("arbitrary",), vmem_limit_bytes=VMEM_LIMIT),
        name="mlp",
    )(xp, xs, w_up, w_down, lnw, lnb)


def _rope_tables(pos):
    posf = np.asarray(pos, np.float64)[:, None]
    half_r = RET_DK // 2
    ang = posf * (RET_THETA ** (-np.arange(half_r, dtype=np.float64) / half_r))[None, :]
    c, s = np.cos(ang), np.sin(ang)
    rc = np.concatenate([c, c], axis=-1)
    rs = np.concatenate([-s, s], axis=-1)
    k_scale = RET_DK ** -0.5
    half_s = SWA_ROT_DIM // 2
    ang = posf * (SWA_THETA ** (-np.arange(half_s, dtype=np.float64) / half_s))[None, :]
    c, s = np.cos(ang), np.sin(ang)
    t = posf.shape[0]
    rest = SWA_HD - SWA_ROT_DIM
    z = np.zeros((t, half_s))
    sc = np.tile(np.concatenate([c, c, np.ones((t, rest))], axis=-1), (1, LANES // SWA_HD))
    s1 = np.tile(np.concatenate([z, s, np.zeros((t, rest))], axis=-1), (1, LANES // SWA_HD))
    s2 = np.tile(np.concatenate([-s, z, np.zeros((t, rest))], axis=-1), (1, LANES // SWA_HD))
    q_scale = SWA_HD ** -0.5
    tab = np.concatenate([rc, rs, rc * k_scale, rs * k_scale,
                          sc * q_scale, s1 * q_scale, s2 * q_scale, sc, s1, s2], axis=-1)
    return jnp.asarray(tab.astype(np.float32))


def _decay_tables(t):
    lg = np.log1p(-(2.0 ** (-5.0 - np.arange(RET_HEADS, dtype=np.float64))))
    idx = np.arange(t, dtype=np.float64)
    diff = idx[:, None] - idx[None, :]
    dmask = np.where(diff >= 0, np.exp(lg[:, None, None] * np.maximum(diff, 0.0)), 0.0)
    qdec = np.repeat(np.exp(lg[None, :] * (idx[:, None] + 1.0)), RET_DV, axis=1)
    kdec = np.repeat(np.exp(lg[None, :] * (t - 1.0 - idx[:, None])), RET_DV, axis=1)
    gt = np.repeat(np.exp(lg * t)[None, :], RET_DV, axis=1)
    return tuple(jnp.asarray(a.astype(np.float32)) for a in (dmask, qdec, kdec, gt))


def kernel(x_prompt, x_sample, cache_swa_k, cache_swa_v, state_ret, w_in, ret_gn_w, swa_sinks,
           w_out, ln1_w, ln1_b, w_up, w_down, ln2_w, ln2_b):
    assert w_in.shape[0] == DEPTH == 1
    b, t, d = x_prompt.shape
    nb, n_new, _ = x_sample.shape
    win = cache_swa_k.shape[2]

    w_in_b = w_in[0].astype(BF16)
    w_out_b = w_out[0].astype(BF16)
    w_up_b = w_up[0].astype(BF16)
    w_down_b = w_down[0].astype(BF16)
    sinks = swa_sinks[0]

    tab_p = _rope_tables(np.arange(t))
    tab_s = _rope_tables(PAST_LEN + np.arange(n_new))
    dec_p = _decay_tables(PROMPT_TILE)
    dec_s = _decay_tables(n_new)

    x1_p, k_p, v_p, r_p = _prompt_mixer(x_prompt, tab_p, *dec_p, sinks, w_in_b, ret_gn_w, w_out_b, ln1_w, ln1_b)
    x1_s, k_s, v_s, r_s = _sample_mixer(
        x_sample, tab_s, *dec_s, cache_swa_k[0].reshape(nb, win, KV_W), cache_swa_v[0].reshape(nb, win, KV_W),
        state_ret[0], sinks, w_in_b, ret_gn_w, w_out_b, ln1_w, ln1_b)
    y_p, y_s = _mlp(x1_p.reshape(b * t, d), x1_s, w_up_b, w_down_b, ln2_w, ln2_b)

    return (y_p.reshape(b, t, d), y_s.reshape(nb, n_new, d),
            k_p.reshape(DEPTH, b, SWA_WINDOW, SWA_KV_HEADS, SWA_HD),
            v_p.reshape(DEPTH, b, SWA_WINDOW, SWA_KV_HEADS, SWA_HD),
            r_p.reshape(DEPTH, b, RET_HEADS, RET_DK, RET_DV),
            k_s.reshape(DEPTH, nb, n_new, SWA_KV_HEADS, SWA_HD),
            v_s.reshape(DEPTH, nb, n_new, SWA_KV_HEADS, SWA_HD),
            r_s.reshape(DEPTH, nb, RET_HEADS, RET_DK, RET_DV))
```

```python
import functools

import numpy as np
import jax
import jax.numpy as jnp
from jax import lax
from jax.experimental import pallas as pl
from jax.experimental.pallas import tpu as pltpu

D_MODEL = 1024
DEPTH = 1
PAST_LEN = 2048
CHUNK = 64
RET_HEADS = 4
RET_DK = 128
RET_DV = 128
RET_THETA = 10000.0
RET_W = RET_HEADS * RET_DV
SWA_HEADS = 8
SWA_KV_HEADS = 2
SWA_HD = 64
SWA_WINDOW = 128
SWA_ROT_DIM = SWA_HD // 4
SWA_THETA = 500000.0
SWA_W = SWA_HEADS * SWA_HD
KV_W = SWA_KV_HEADS * SWA_HD
MIX_W = RET_W + SWA_W
D_FF = 4 * D_MODEL
OFF_RQ, OFF_RK, OFF_RV, OFF_RG = 0, RET_W, 2 * RET_W, 3 * RET_W
OFF_SQ = 4 * RET_W
OFF_SK = OFF_SQ + SWA_W
OFF_SV = OFF_SK + KV_W
PROJ_W = OFF_SV + KV_W
ALPHA = (2.0 * DEPTH) ** 0.25
LN_EPS = 1e-5
GN_EPS = 1e-5
NEG_INF = -1e30

LANES = 128
MXU_COLS = 256
N_PAIRS = SWA_W // LANES
PAIRS_PER_KV = N_PAIRS // SWA_KV_HEADS
T_RQC, T_RQS, T_RKC, T_RKS, T_SQC, T_SQ1, T_SQ2, T_SKC, T_SK1, T_SK2 = range(10)
N_TABS = 10

PROMPT_TILE = 256
N_SLABS = PROJ_W // MXU_COLS
LN_BLOCKS = 4
SLABS_AFTER_RET_HEAD = (1, 1, 1, 0)
SLABS_AFTER_SOFTMAX_BLOCK = (1, 1, 1, 1, 1, 1, 0, 0)
SLABS_AFTER_LN_BLOCK = (1, 1, 0, 0)
MLP_TILE = 1024
MLP_UP_ROWS = 512
MLP_LN_ROWS = 128
VMEM_LIMIT = 56 * 1024 * 1024

F32 = jnp.float32
BF16 = jnp.bfloat16


def _tab(tab_ref, i):
    return tab_ref[:, i * LANES:(i + 1) * LANES]


def _dot(a, b):
    return jnp.dot(a, b, preferred_element_type=F32)


def _dot_nt(a, b):
    return lax.dot_general(a, b, (((1,), (1,)), ((), ())), preferred_element_type=F32)


def _dot_tn(a, b):
    return lax.dot_general(a, b, (((0,), (0,)), ((), ())), preferred_element_type=F32)


def _layer_norm(z, w, b):
    mu = jnp.mean(z, axis=-1, keepdims=True)
    zc = z - mu
    var = jnp.mean(zc * zc, axis=-1, keepdims=True)
    return zc * lax.rsqrt(var + LN_EPS) * w + b


def _rope_ret(x, c, s):
    return x * c + pltpu.roll(x, RET_DK // 2, axis=1) * s


def _rope_swa(x, c, s1, s2):
    half = SWA_ROT_DIM // 2
    return x * c + pltpu.roll(x, half, axis=1) * s1 + pltpu.roll(x, LANES - half, axis=1) * s2


def _retention_head(q, k, v, g, s_prev, dmask, qdec, kdec, gt, gnw):
    qb = q.astype(BF16)
    kb = k.astype(BF16)
    vb = v.astype(BF16)
    scores = _dot_nt(qb, kb) * dmask
    o = _dot(scores.astype(BF16), vb)
    o = o + _dot(qb, s_prev.astype(BF16)) * qdec
    s_new = gt * s_prev + _dot_tn((k * kdec).astype(BF16), vb)
    mu = jnp.mean(o, axis=-1, keepdims=True)
    oc = o - mu
    var = jnp.mean(oc * oc, axis=-1, keepdims=True)
    on = oc * lax.rsqrt(var + GN_EPS) * gnw
    return jax.nn.silu(g) * on, s_new


def _kv_variants(a):
    lane = lax.broadcasted_iota(jnp.int32, a.shape, 1)
    lo = lane < SWA_HD
    ar = pltpu.roll(a, SWA_HD, axis=1)
    zero = jnp.zeros_like(a)
    return (jnp.where(lo, a, zero).astype(BF16), jnp.where(lo, zero, ar).astype(BF16),
            jnp.where(lo, ar, zero).astype(BF16), jnp.where(lo, zero, a).astype(BF16))


def _pair_rows(a_e, a_o, r0, nk):
    return jnp.concatenate([a_e[r0:r0 + LANES], a_o[r0:r0 + LANES],
                            a_e[r0 + LANES:r0 + nk], a_o[r0 + LANES:r0 + nk]], axis=0)


def _swa_softmax(s, nk, sink_e, sink_o, valid):
    r = nk - LANES
    s_e, s_o, s_x = s[:, :LANES], s[:, LANES:2 * LANES], s[:, 2 * LANES:]
    if valid is not None:
        s_e = jnp.where(valid, s_e, NEG_INF)
        s_o = jnp.where(valid, s_o, NEG_INF)
    x_is_e = lax.broadcasted_iota(jnp.int32, s_x.shape, 1) < r
    rowmax = lambda a: jnp.max(a, axis=-1, keepdims=True)
    rowsum = lambda a: jnp.sum(a, axis=-1, keepdims=True)
    x_e = jnp.where(x_is_e, s_x, -jnp.inf)
    x_o = jnp.where(x_is_e, -jnp.inf, s_x)
    if 2 * r == LANES:
        m_e, m_o = rowmax(jnp.maximum(s_e, x_e)), rowmax(jnp.maximum(s_o, x_o))
    else:
        m_e, m_o = jnp.maximum(rowmax(s_e), rowmax(x_e)), jnp.maximum(rowmax(s_o), rowmax(x_o))
    m_e = jnp.maximum(m_e, sink_e)
    m_o = jnp.maximum(m_o, sink_o)
    e_e = jnp.exp(s_e - m_e)
    e_o = jnp.exp(s_o - m_o)
    e_x = jnp.exp(s_x - jnp.where(x_is_e, m_e, m_o))
    x_e = jnp.where(x_is_e, e_x, 0.0)
    x_o = jnp.where(x_is_e, 0.0, e_x)
    if 2 * r == LANES:
        d_e, d_o = rowsum(e_e + x_e), rowsum(e_o + x_o)
    else:
        d_e, d_o = rowsum(e_e) + rowsum(x_e), rowsum(e_o) + rowsum(x_o)
    d_e = d_e + jnp.exp(sink_e - m_e)
    d_o = d_o + jnp.exp(sink_o - m_o)
    lane = lax.broadcasted_iota(jnp.int32, (s.shape[0], LANES), 1)
    e = jnp.concatenate([e_e, e_o, e_x], axis=1).astype(BF16)
    return e, jnp.where(lane < SWA_HD, d_e, d_o)


def _swa_pair(qp, kt, vt, nk, sink_e, sink_o):
    e, den = _swa_softmax(_dot_nt(qp, kt), nk, sink_e, sink_o, None)
    return _dot(e, vt) / den


def _prompt_mixer_kernel(sinks_ref, x_ref, xn_ref, tab_ref, dmask_ref, qdec_ref, kdec_ref, gt_ref, w_in_ref,
                         gnw_ref, w_out_ref, ln1w_ref, ln1b_ref,
                         x1_ref, kout_ref, vout_ref, sout_ref,
                         pa_scr, pb_scr, xb_scr, s_scr, kext_scr, vext_scr, mix_scr, sw_s_scr, sw_e_scr, sw_d_scr,
                         *, steps_per_seq):
    tt = PROMPT_TILE
    g = pl.program_id(0)
    seq_step = g % steps_per_seq

    @pl.when(seq_step == 0)
    def _():
        s_scr[...] = jnp.zeros_like(s_scr)
        kext_scr[0:SWA_WINDOW, :] = jnp.zeros((SWA_WINDOW, KV_W), F32)
        vext_scr[0:SWA_WINDOW, :] = jnp.zeros((SWA_WINDOW, KV_W), F32)

    @pl.when(g == 0)
    def _():
        pa_scr[...] = _dot(x_ref[0, 0:tt, :].astype(BF16), w_in_ref[...])

    def project_next_slabs(p_nxt, slabs, count):
        for _ in range(count):
            c = next(slabs)
            cols = slice(c * MXU_COLS, (c + 1) * MXU_COLS)
            p_nxt[:, cols] = _dot(xb_scr[...], w_in_ref[:, cols])

    def tile(r0x, p_cur, p_nxt, load_next_x, seq_start):
        rows_x = slice(r0x, r0x + tt)
        tab = lambda i: tab_ref[rows_x, i * LANES:(i + 1) * LANES]
        xb_scr[...] = load_next_x().astype(BF16)
        slabs = iter(range(N_SLABS))

        for h in range(RET_HEADS):
            sl = slice(h * LANES, (h + 1) * LANES)
            q = _rope_ret(p_cur[:, OFF_RQ + h * LANES:OFF_RQ + (h + 1) * LANES], tab(T_RQC), tab(T_RQS))
            k = _rope_ret(p_cur[:, OFF_RK + h * LANES:OFF_RK + (h + 1) * LANES], tab(T_RKC), tab(T_RKS))
            v = p_cur[:, OFF_RV + h * LANES:OFF_RV + (h + 1) * LANES]
            gate = p_cur[:, OFF_RG + h * LANES:OFF_RG + (h + 1) * LANES]
            out, s_new = _retention_head(q, k, v, gate, s_scr[h], dmask_ref[h], qdec_ref[:, sl], kdec_ref[:, sl],
                                         gt_ref[:, sl], gnw_ref[:, sl])
            s_scr[h] = s_new
            mix_scr[:, sl] = out.astype(BF16)
            project_next_slabs(p_nxt, slabs, SLABS_AFTER_RET_HEAD[h])

        k_new = _rope_swa(p_cur[:, OFF_SK:OFF_SK + KV_W], tab(T_SKC), tab(T_SK1), tab(T_SK2))
        v_new = p_cur[:, OFF_SV:OFF_SV + KV_W]
        kext_scr[SWA_WINDOW:SWA_WINDOW + tt, :] = k_new
        vext_scr[SWA_WINDOW:SWA_WINDOW + tt, :] = v_new
        k_var = _kv_variants(kext_scr[...])
        v_var = _kv_variants(vext_scr[...])
        nk = SWA_WINDOW + CHUNK
        n_chunks = tt // CHUNK
        qs = [_rope_swa(p_cur[:, OFF_SQ + m * LANES:OFF_SQ + (m + 1) * LANES],
                        tab(T_SQC), tab(T_SQ1), tab(T_SQ2)).astype(BF16)
              for m in range(N_PAIRS)]
        col = lax.broadcasted_iota(jnp.int32, (PAIRS_PER_KV * CHUNK, LANES), 1)
        row = lax.broadcasted_iota(jnp.int32, (PAIRS_PER_KV * CHUNK, 1), 0)
        blocks = [(grp, i) for grp in range(SWA_KV_HEADS) for i in range(n_chunks)]
        rows_of = lambda n: slice(n * PAIRS_PER_KV * CHUNK, (n + 1) * PAIRS_PER_KV * CHUNK)
        for n, (grp, i) in enumerate(blocks):
            r0 = i * CHUNK
            q_blk = jnp.concatenate([qs[grp * PAIRS_PER_KV + a][r0:r0 + CHUNK] for a in range(PAIRS_PER_KV)], axis=0)
            sw_s_scr[rows_of(n), :] = _dot_nt(q_blk, _pair_rows(k_var[2 * grp], k_var[2 * grp + 1], r0, nk))
        for n, (grp, i) in enumerate(blocks):
            sink_e = sinks_ref[2 * grp * PAIRS_PER_KV]
            sink_o = sinks_ref[2 * grp * PAIRS_PER_KV + 1]
            for a in range(1, PAIRS_PER_KV):
                sink_e = jnp.where(row >= a * CHUNK, sinks_ref[2 * (grp * PAIRS_PER_KV + a)], sink_e)
                sink_o = jnp.where(row >= a * CHUNK, sinks_ref[2 * (grp * PAIRS_PER_KV + a) + 1], sink_o)
            valid = None
            if seq_start is not None and i < SWA_WINDOW // CHUNK:
                valid = jnp.logical_or(col >= SWA_WINDOW - i * CHUNK, jnp.logical_not(seq_start))
            e, den = _swa_softmax(sw_s_scr[rows_of(n), :], nk, sink_e, sink_o, valid)
            sw_e_scr[rows_of(n), :] = e
            sw_d_scr[rows_of(n), :] = den
            project_next_slabs(p_nxt, slabs, SLABS_AFTER_SOFTMAX_BLOCK[n])
        for n, (grp, i) in enumerate(blocks):
            r0 = i * CHUNK
            vt = _pair_rows(v_var[2 * grp], v_var[2 * grp + 1], r0, nk)
            o = (_dot(sw_e_scr[rows_of(n), :], vt) / sw_d_scr[rows_of(n), :]).astype(BF16)
            for a in range(PAIRS_PER_KV):
                m = grp * PAIRS_PER_KV + a
                mix_scr[r0:r0 + CHUNK, RET_W + m * LANES:RET_W + (m + 1) * LANES] = o[a * CHUNK:(a + 1) * CHUNK]

        kext_scr[0:SWA_WINDOW, :] = k_new[tt - SWA_WINDOW:, :]
        vext_scr[0:SWA_WINDOW, :] = v_new[tt - SWA_WINDOW:, :]

        x1_ref[0, rows_x, :] = ALPHA * x_ref[0, rows_x, :] + _dot(mix_scr[...], w_out_ref[...])
        lb = tt // LN_BLOCKS
        for r in range(LN_BLOCKS):
            rows_ln = slice(r0x + r * lb, r0x + (r + 1) * lb)
            x1_ref[0, rows_ln, :] = _layer_norm(x1_ref[0, rows_ln, :], ln1w_ref[...], ln1b_ref[...])
            project_next_slabs(p_nxt, slabs, SLABS_AFTER_LN_BLOCK[r])
        assert next(slabs, None) is None
        return k_new, v_new

    tile(0, pa_scr, pb_scr, lambda: x_ref[0, tt:2 * tt, :], seq_step == 0)
    k_new, v_new = tile(tt, pb_scr, pa_scr, lambda: xn_ref[0], None)

    @pl.when(seq_step == steps_per_seq - 1)
    def _():
        kout_ref[0] = k_new[tt - SWA_WINDOW:, :]
        vout_ref[0] = v_new[tt - SWA_WINDOW:, :]
        sout_ref[0] = s_scr[...]


def _prompt_mixer(x, tab, dmask, qdec, kdec, gt, sinks, w_in, gnw, w_out, ln1w, ln1b):
    b, t, d = x.shape
    tt = PROMPT_TILE
    sps = t // (2 * tt)
    n_steps = b * sps
    const2 = lambda g: (0, 0)
    const3 = lambda g: (0, 0, 0)

    def next_tile(g):
        gn = jnp.minimum(g + 1, n_steps - 1)
        return (gn // sps, (gn % sps) * 2, 0)

    resident = dict(pipeline_mode=pl.Buffered(1))
    return pl.pallas_call(
        functools.partial(_prompt_mixer_kernel, steps_per_seq=sps),
        grid=(n_steps,),
        in_specs=[
            pl.BlockSpec(memory_space=pltpu.SMEM),
            pl.BlockSpec((1, 2 * tt, d), lambda g: (g // sps, g % sps, 0)),
            pl.BlockSpec((1, tt, d), next_tile),
            pl.BlockSpec((2 * tt, N_TABS * LANES), lambda g: (g % sps, 0)),
            pl.BlockSpec((RET_HEADS, tt, tt), const3, **resident),
            pl.BlockSpec((tt, RET_W), const2, **resident),
            pl.BlockSpec((tt, RET_W), const2, **resident),
            pl.BlockSpec((1, RET_W), const2),
            pl.BlockSpec((d, PROJ_W), const2, **resident),
            pl.BlockSpec((1, RET_W), const2),
            pl.BlockSpec((MIX_W, d), const2, **resident),
            pl.BlockSpec((1, d), const2),
            pl.BlockSpec((1, d), const2),
        ],
        out_specs=[
            pl.BlockSpec((1, 2 * tt, d), lambda g: (g // sps, g % sps, 0)),
            pl.BlockSpec((1, SWA_WINDOW, KV_W), lambda g: (g // sps, 0, 0)),
            pl.BlockSpec((1, SWA_WINDOW, KV_W), lambda g: (g // sps, 0, 0)),
            pl.BlockSpec((1, RET_HEADS, RET_DK, RET_DV), lambda g: (g // sps, 0, 0, 0)),
        ],
        out_shape=[
            jax.ShapeDtypeStruct((b, t, d), F32),
            jax.ShapeDtypeStruct((b, SWA_WINDOW, KV_W), F32),
            jax.ShapeDtypeStruct((b, SWA_WINDOW, KV_W), F32),
            jax.ShapeDtypeStruct((b, RET_HEADS, RET_DK, RET_DV), F32),
        ],
        scratch_shapes=[
            pltpu.VMEM((tt, PROJ_W), F32),
            pltpu.VMEM((tt, PROJ_W), F32),
            pltpu.VMEM((tt, d), BF16),
            pltpu.VMEM((RET_HEADS, RET_DK, RET_DV), F32),
            pltpu.VMEM((SWA_WINDOW + tt, KV_W), F32),
            pltpu.VMEM((SWA_WINDOW + tt, KV_W), F32),
            pltpu.VMEM((tt, MIX_W), BF16),
            pltpu.VMEM((N_PAIRS * tt, 2 * (SWA_WINDOW + CHUNK)), F32),
            pltpu.VMEM((N_PAIRS * tt, 2 * (SWA_WINDOW + CHUNK)), BF16),
            pltpu.VMEM((N_PAIRS * tt, LANES), F32),
        ],
        compiler_params=pltpu.CompilerParams(
            dimension_semantics=("arbitrary",), vmem_limit_bytes=VMEM_LIMIT),
        name="prompt_mixer",
    )(sinks, x, x, tab, dmask, qdec, kdec, gt, w_in, gnw, w_out, ln1w, ln1b)


def _sample_mixer_kernel(sinks_ref, x_ref, tab_ref, dmask_ref, qdec_ref, kdec_ref, gt_ref, ck_ref, cv_ref,
                         state_ref, w_in_ref, gnw_ref, w_out_ref, ln1w_ref, ln1b_ref,
                         x1_ref, kout_ref, vout_ref, sout_ref,
                         p_scr, mix_scr, *, n_new):
    s_idx = pl.program_id(0)
    last_s = pl.num_programs(0) - 1

    @pl.when(s_idx == 0)
    def _():
        p_scr[...] = _dot(x_ref[...].astype(BF16), w_in_ref[...])

    rows = pl.ds(pl.multiple_of(s_idx * n_new, n_new), n_new)

    for h in range(RET_HEADS):
        sl = slice(h * LANES, (h + 1) * LANES)
        q = _rope_ret(p_scr[rows, OFF_RQ + h * LANES:OFF_RQ + (h + 1) * LANES], _tab(tab_ref, T_RQC), _tab(tab_ref, T_RQS))
        k = _rope_ret(p_scr[rows, OFF_RK + h * LANES:OFF_RK + (h + 1) * LANES], _tab(tab_ref, T_RKC), _tab(tab_ref, T_RKS))
        v = p_scr[rows, OFF_RV + h * LANES:OFF_RV + (h + 1) * LANES]
        g = p_scr[rows, OFF_RG + h * LANES:OFF_RG + (h + 1) * LANES]
        out, s_new = _retention_head(q, k, v, g, state_ref[0, h], dmask_ref[h], qdec_ref[:, sl], kdec_ref[:, sl],
                                     gt_ref[:, sl], gnw_ref[:, sl])
        sout_ref[0, h] = s_new
        mix_scr[rows, sl] = out.astype(BF16)

    k_new = _rope_swa(p_scr[rows, OFF_SK:OFF_SK + KV_W], _tab(tab_ref, T_SKC), _tab(tab_ref, T_SK1), _tab(tab_ref, T_SK2))
    v_new = p_scr[rows, OFF_SV:OFF_SV + KV_W]
    kout_ref[0] = k_new
    vout_ref[0] = v_new
    k_var = _kv_variants(jnp.concatenate([ck_ref[0], k_new], axis=0))
    v_var = _kv_variants(jnp.concatenate([cv_ref[0], v_new], axis=0))
    nk = ck_ref.shape[1] + n_new
    for m in range(N_PAIRS):
        qs = p_scr[rows, OFF_SQ + m * LANES:OFF_SQ + (m + 1) * LANES]
        qs = _rope_swa(qs, _tab(tab_ref, T_SQC), _tab(tab_ref, T_SQ1), _tab(tab_ref, T_SQ2)).astype(BF16)
        grp = (2 * m) // (SWA_HEADS // SWA_KV_HEADS)
        kt = _pair_rows(k_var[2 * grp], k_var[2 * grp + 1], 0, nk)
        vt = _pair_rows(v_var[2 * grp], v_var[2 * grp + 1], 0, nk)
        o = _swa_pair(qs, kt, vt, nk, sinks_ref[2 * m], sinks_ref[2 * m + 1])
        mix_scr[rows, RET_W + m * LANES:RET_W + (m + 1) * LANES] = o.astype(BF16)

    @pl.when(s_idx == last_s)
    def _():
        z = ALPHA * x_ref[...] + _dot(mix_scr[...], w_out_ref[...])
        x1_ref[...] = _layer_norm(z, ln1w_ref[...], ln1b_ref[...])


def _sample_mixer(x, tab, dmask, qdec, kdec, gt, cache_k, cache_v, state, sinks, w_in, gnw, w_out, ln1w, ln1b):
    nb, n_new, d = x.shape
    rows = nb * n_new
    win = cache_k.shape[1]
    const2 = lambda si: (0, 0)
    const3 = lambda si: (0, 0, 0)
    return pl.pallas_call(
        functools.partial(_sample_mixer_kernel, n_new=n_new),
        grid=(nb,),
        in_specs=[
            pl.BlockSpec(memory_space=pltpu.SMEM),
            pl.BlockSpec((rows, d), const2),
            pl.BlockSpec((n_new, N_TABS * LANES), const2),
            pl.BlockSpec((RET_HEADS, n_new, n_new), const3),
            pl.BlockSpec((n_new, RET_W), const2),
            pl.BlockSpec((n_new, RET_W), const2),
            pl.BlockSpec((1, RET_W), const2),
            pl.BlockSpec((1, win, KV_W), lambda si: (si, 0, 0)),
            pl.BlockSpec((1, win, KV_W), lambda si: (si, 0, 0)),
            pl.BlockSpec((1, RET_HEADS, RET_DK, RET_DV), lambda si: (si, 0, 0, 0)),
            pl.BlockSpec((d, PROJ_W), const2),
            pl.BlockSpec((1, RET_W), const2),
            pl.BlockSpec((MIX_W, d), const2),
            pl.BlockSpec((1, d), const2),
            pl.BlockSpec((1, d), const2),
        ],
        out_specs=[
            pl.BlockSpec((rows, d), const2),
            pl.BlockSpec((1, n_new, KV_W), lambda si: (si, 0, 0)),
            pl.BlockSpec((1, n_new, KV_W), lambda si: (si, 0, 0)),
            pl.BlockSpec((1, RET_HEADS, RET_DK, RET_DV), lambda si: (si, 0, 0, 0)),
        ],
        out_shape=[
            jax.ShapeDtypeStruct((rows, d), F32),
            jax.ShapeDtypeStruct((nb, n_new, KV_W), F32),
            jax.ShapeDtypeStruct((nb, n_new, KV_W), F32),
            jax.ShapeDtypeStruct((nb, RET_HEADS, RET_DK, RET_DV), F32),
        ],
        scratch_shapes=[
            pltpu.VMEM((rows, PROJ_W), F32),
            pltpu.VMEM((rows, MIX_W), BF16),
        ],
        compiler_params=pltpu.CompilerParams(
            dimension_semantics=("arbitrary",), vmem_limit_bytes=VMEM_LIMIT),
        name="sample_mixer",
    )(sinks, x.reshape(rows, d), tab, dmask, qdec, kdec, gt, cache_k, cache_v, state, w_in, gnw, w_out, ln1w, ln1b)


def _mlp_kernel(xp_ref, xs_ref, w_up_ref, w_down_ref, lnw_ref, lnb_ref, yp_ref, ys_ref, h_scr, *, prompt_steps):
    i = pl.program_id(0)

    def block(x_ref, y_ref):
        for r0 in range(0, x_ref.shape[0], MLP_UP_ROWS):
            n = min(MLP_UP_ROWS, x_ref.shape[0] - r0)
            h = _dot(x_ref[r0:r0 + n, :].astype(BF16), w_up_ref[...])
            h_scr[0:n, :] = jnp.square(jnp.maximum(h, 0.0)).astype(BF16)
            for r in range(0, n, MLP_LN_ROWS):
                rows = slice(r0 + r, r0 + r + MLP_LN_ROWS)
                z = ALPHA * x_ref[rows, :] + _dot(h_scr[r:r + MLP_LN_ROWS, :], w_down_ref[...])
                y_ref[rows, :] = _layer_norm(z, lnw_ref[...], lnb_ref[...])

    @pl.when(i < prompt_steps)
    def _():
        block(xp_ref, yp_ref)

    @pl.when(i == prompt_steps)
    def _():
        block(xs_ref, ys_ref)


def _mlp(xp, xs, w_up, w_down, lnw, lnb):
    rows, d = xp.shape
    rows_s = xs.shape[0]
    steps = rows // MLP_TILE
    const2 = lambda i: (0, 0)
    prompt_tile = lambda i: (jnp.minimum(i, steps - 1), 0)
    resident = dict(pipeline_mode=pl.Buffered(1))
    return pl.pallas_call(
        functools.partial(_mlp_kernel, prompt_steps=steps),
        grid=(steps + 1,),
        in_specs=[
            pl.BlockSpec((MLP_TILE, d), prompt_tile),
            pl.BlockSpec((rows_s, d), const2, **resident),
            pl.BlockSpec((d, D_FF), const2, **resident),
            pl.BlockSpec((D_FF, d), const2, **resident),
            pl.BlockSpec((1, d), const2),
            pl.BlockSpec((1, d), const2),
        ],
        out_specs=[
            pl.BlockSpec((MLP_TILE, d), prompt_tile),
            pl.BlockSpec((rows_s, d), const2),
        ],
        out_shape=[
            jax.ShapeDtypeStruct((rows, d), F32),
            jax.ShapeDtypeStruct((rows_s, d), F32),
        ],
        scratch_shapes=[pltpu.VMEM((MLP_UP_ROWS, D_FF), BF16)],
        compiler_params=pltpu.CompilerParams(
            dimension_semantics=("arbitrary",), vmem_limit_bytes=VMEM_LIMIT),
        name="mlp",
    )(xp, xs, w_up, w_down, lnw, lnb)


def _rope_tables(pos):
    posf = np.asarray(pos, np.float64)[:, None]
    half_r = RET_DK // 2
    ang = posf * (RET_THETA ** (-np.arange(half_r, dtype=np.float64) / half_r))[None, :]
    c, s = np.cos(ang), np.sin(ang)
    rc = np.concatenate([c, c], axis=-1)
    rs = np.concatenate([-s, s], axis=-1)
    k_scale = RET_DK ** -0.5
    half_s = SWA_ROT_DIM // 2
    ang = posf * (SWA_THETA ** (-np.arange(half_s, dtype=np.float64) / half_s))[None, :]
    c, s = np.cos(ang), np.sin(ang)
    t = posf.shape[0]
    rest = SWA_HD - SWA_ROT_DIM
    z = np.zeros((t, half_s))
    sc = np.tile(np.concatenate([c, c, np.ones((t, rest))], axis=-1), (1, LANES // SWA_HD))
    s1 = np.tile(np.concatenate([z, s, np.zeros((t, rest))], axis=-1), (1, LANES // SWA_HD))
    s2 = np.tile(np.concatenate([-s, z, np.zeros((t, rest))], axis=-1), (1, LANES // SWA_HD))
    q_scale = SWA_HD ** -0.5
    tab = np.concatenate([rc, rs, rc * k_scale, rs * k_scale,
                          sc * q_scale, s1 * q_scale, s2 * q_scale, sc, s1, s2], axis=-1)
    return jnp.asarray(tab.astype(np.float32))


def _decay_tables(t):
    lg = np.log1p(-(2.0 ** (-5.0 - np.arange(RET_HEADS, dtype=np.float64))))
    idx = np.arange(t, dtype=np.float64)
    diff = idx[:, None] - idx[None, :]
    dmask = np.where(diff >= 0, np.exp(lg[:, None, None] * np.maximum(diff, 0.0)), 0.0)
    qdec = np.repeat(np.exp(lg[None, :] * (idx[:, None] + 1.0)), RET_DV, axis=1)
    kdec = np.repeat(np.exp(lg[None, :] * (t - 1.0 - idx[:, None])), RET_DV, axis=1)
    gt = np.repeat(np.exp(lg * t)[None, :], RET_DV, axis=1)
    return tuple(jnp.asarray(a.astype(np.float32)) for a in (dmask, qdec, kdec, gt))


def kernel(x_prompt, x_sample, cache_swa_k, cache_swa_v, state_ret, w_in, ret_gn_w, swa_sinks,
           w_out, ln1_w, ln1_b, w_up, w_down, ln2_w, ln2_b):
    assert w_in.shape[0] == DEPTH == 1
    b, t, d = x_prompt.shape
    nb, n_new, _ = x_sample.shape
    win = cache_swa_k.shape[2]

    w_in_b = w_in[0].astype(BF16)
    w_out_b = w_out[0].astype(BF16)
    w_up_b = w_up[0].astype(BF16)
    w_down_b = w_down[0].astype(BF16)
    sinks = swa_sinks[0]

    tab_p = _rope_tables(np.arange(t))
    tab_s = _rope_tables(PAST_LEN + np.arange(n_new))
    dec_p = _decay_tables(PROMPT_TILE)
    dec_s = _decay_tables(n_new)

    x1_p, k_p, v_p, r_p = _prompt_mixer(x_prompt, tab_p, *dec_p, sinks, w_in_b, ret_gn_w, w_out_b, ln1_w, ln1_b)
    x1_s, k_s, v_s, r_s = _sample_mixer(
        x_sample, tab_s, *dec_s, cache_swa_k[0].reshape(nb, win, KV_W), cache_swa_v[0].reshape(nb, win, KV_W),
        state_ret[0], sinks, w_in_b, ret_gn_w, w_out_b, ln1_w, ln1_b)
    y_p, y_s = _mlp(x1_p.reshape(b * t, d), x1_s, w_up_b, w_down_b, ln2_w, ln2_b)

    return (y_p.reshape(b, t, d), y_s.reshape(nb, n_new, d),
            k_p.reshape(DEPTH, b, SWA_WINDOW, SWA_KV_HEADS, SWA_HD),
            v_p.reshape(DEPTH, b, SWA_WINDOW, SWA_KV_HEADS, SWA_HD),
            r_p.reshape(DEPTH, b, RET_HEADS, RET_DK, RET_DV),
            k_s.reshape(DEPTH, nb, n_new, SWA_KV_HEADS, SWA_HD),
            v_s.reshape(DEPTH, nb, n_new, SWA_KV_HEADS, SWA_HD),
            r_s.reshape(DEPTH, nb, RET_HEADS, RET_DK, RET_DV))
```

```python
import functools

import numpy as np
import jax
import jax.numpy as jnp
from jax import lax
from jax.experimental import pallas as pl
from jax.experimental.pallas import tpu as pltpu

D_MODEL = 1024
DEPTH = 1
PAST_LEN = 2048
CHUNK = 64
RET_HEADS = 4
RET_DK = 128
RET_DV = 128
RET_THETA = 10000.0
RET_W = RET_HEADS * RET_DV
SWA_HEADS = 8
SWA_KV_HEADS = 2
SWA_HD = 64
SWA_WINDOW = 128
SWA_ROT_DIM = SWA_HD // 4
SWA_THETA = 500000.0
SWA_W = SWA_HEADS * SWA_HD
KV_W = SWA_KV_HEADS * SWA_HD
MIX_W = RET_W + SWA_W
D_FF = 4 * D_MODEL
OFF_RQ, OFF_RK, OFF_RV, OFF_RG = 0, RET_W, 2 * RET_W, 3 * RET_W
OFF_SQ = 4 * RET_W
OFF_SK = OFF_SQ + SWA_W
OFF_SV = OFF_SK + KV_W
PROJ_W = OFF_SV + KV_W
ALPHA = (2.0 * DEPTH) ** 0.25
LN_EPS = 1e-5
GN_EPS = 1e-5
NEG_INF = -1e30

LANES = 128
MXU_COLS = 256
N_PAIRS = SWA_W // LANES
PAIRS_PER_KV = N_PAIRS // SWA_KV_HEADS
T_RQC, T_RQS, T_RKC, T_RKS, T_SQC, T_SQ1, T_SQ2, T_SKC, T_SK1, T_SK2 = range(10)
N_TABS = 10

PROMPT_TILE = 256
N_SLABS = PROJ_W // MXU_COLS
LN_BLOCKS = 4
SLABS_AFTER_RET_HEAD = (1, 1, 1, 0)
SLABS_AFTER_SOFTMAX_BLOCK = (1, 1, 1, 1, 1, 1, 0, 0)
SLABS_AFTER_LN_BLOCK = (1, 1, 0, 0)
SAMPLE_STREAMS = 4
MLP_TILE = 1024
MLP_UP_ROWS = 512
MLP_LN_ROWS = 128
VMEM_LIMIT = 56 * 1024 * 1024

F32 = jnp.float32
BF16 = jnp.bfloat16


def _tab(tab_ref, i):
    return tab_ref[:, i * LANES:(i + 1) * LANES]


def _dot(a, b):
    return jnp.dot(a, b, preferred_element_type=F32)


def _dot_nt(a, b):
    return lax.dot_general(a, b, (((1,), (1,)), ((), ())), preferred_element_type=F32)


def _dot_tn(a, b):
    return lax.dot_general(a, b, (((0,), (0,)), ((), ())), preferred_element_type=F32)


def _layer_norm(z, w, b):
    mu = jnp.mean(z, axis=-1, keepdims=True)
    zc = z - mu
    var = jnp.mean(zc * zc, axis=-1, keepdims=True)
    return zc * lax.rsqrt(var + LN_EPS) * w + b


def _rope_ret(x, c, s):
    return x * c + pltpu.roll(x, RET_DK // 2, axis=1) * s


def _rope_swa(x, c, s1, s2):
    half = SWA_ROT_DIM // 2
    return x * c + pltpu.roll(x, half, axis=1) * s1 + pltpu.roll(x, LANES - half, axis=1) * s2


def _retention_head(q, k, v, g, s_prev, dmask, qdec, kdec, gt, gnw):
    qb = q.astype(BF16)
    kb = k.astype(BF16)
    vb = v.astype(BF16)
    scores = _dot_nt(qb, kb) * dmask
    o = _dot(scores.astype(BF16), vb)
    o = o + _dot(qb, s_prev.astype(BF16)) * qdec
    s_new = gt * s_prev + _dot_tn((k * kdec).astype(BF16), vb)
    mu = jnp.mean(o, axis=-1, keepdims=True)
    oc = o - mu
    var = jnp.mean(oc * oc, axis=-1, keepdims=True)
    on = oc * lax.rsqrt(var + GN_EPS) * gnw
    return jax.nn.silu(g) * on, s_new


def _kv_variants(a):
    lane = lax.broadcasted_iota(jnp.int32, a.shape, 1)
    lo = lane < SWA_HD
    ar = pltpu.roll(a, SWA_HD, axis=1)
    zero = jnp.zeros_like(a)
    return (jnp.where(lo, a, zero).astype(BF16), jnp.where(lo, zero, ar).astype(BF16),
            jnp.where(lo, ar, zero).astype(BF16), jnp.where(lo, zero, a).astype(BF16))


def _pair_rows(a_e, a_o, r0, nk):
    return jnp.concatenate([a_e[r0:r0 + LANES], a_o[r0:r0 + LANES],
                            a_e[r0 + LANES:r0 + nk], a_o[r0 + LANES:r0 + nk]], axis=0)


def _swa_softmax(s, nk, sink_e, sink_o, valid):
    r = nk - LANES
    s_e, s_o, s_x = s[:, :LANES], s[:, LANES:2 * LANES], s[:, 2 * LANES:]
    if valid is not None:
        s_e = jnp.where(valid, s_e, NEG_INF)
        s_o = jnp.where(valid, s_o, NEG_INF)
    x_is_e = lax.broadcasted_iota(jnp.int32, s_x.shape, 1) < r
    rowmax = lambda a: jnp.max(a, axis=-1, keepdims=True)
    rowsum = lambda a: jnp.sum(a, axis=-1, keepdims=True)
    x_e = jnp.where(x_is_e, s_x, -jnp.inf)
    x_o = jnp.where(x_is_e, -jnp.inf, s_x)
    if 2 * r == LANES:
        m_e, m_o = rowmax(jnp.maximum(s_e, x_e)), rowmax(jnp.maximum(s_o, x_o))
    else:
        m_e, m_o = jnp.maximum(rowmax(s_e), rowmax(x_e)), jnp.maximum(rowmax(s_o), rowmax(x_o))
    m_e = jnp.maximum(m_e, sink_e)
    m_o = jnp.maximum(m_o, sink_o)
    e_e = jnp.exp(s_e - m_e)
    e_o = jnp.exp(s_o - m_o)
    e_x = jnp.exp(s_x - jnp.where(x_is_e, m_e, m_o))
    x_e = jnp.where(x_is_e, e_x, 0.0)
    x_o = jnp.where(x_is_e, 0.0, e_x)
    if 2 * r == LANES:
        d_e, d_o = rowsum(e_e + x_e), rowsum(e_o + x_o)
    else:
        d_e, d_o = rowsum(e_e) + rowsum(x_e), rowsum(e_o) + rowsum(x_o)
    d_e = d_e + jnp.exp(sink_e - m_e)
    d_o = d_o + jnp.exp(sink_o - m_o)
    lane = lax.broadcasted_iota(jnp.int32, (s.shape[0], LANES), 1)
    e = jnp.concatenate([e_e, e_o, e_x], axis=1).astype(BF16)
    return e, jnp.where(lane < SWA_HD, d_e, d_o)


def _prompt_mixer_kernel(sinks_ref, x_ref, xn_ref, tab_ref, dmask_ref, qdec_ref, kdec_ref, gt_ref, w_in_ref,
                         gnw_ref, w_out_ref, ln1w_ref, ln1b_ref,
                         x1_ref, kout_ref, vout_ref, sout_ref,
                         pa_scr, pb_scr, xb_scr, s_scr, kext_scr, vext_scr, mix_scr, sw_s_scr, sw_e_scr, sw_d_scr,
                         *, steps_per_seq):
    tt = PROMPT_TILE
    g = pl.program_id(0)
    seq_step = g % steps_per_seq

    @pl.when(seq_step == 0)
    def _():
        s_scr[...] = jnp.zeros_like(s_scr)
        kext_scr[0:SWA_WINDOW, :] = jnp.zeros((SWA_WINDOW, KV_W), F32)
        vext_scr[0:SWA_WINDOW, :] = jnp.zeros((SWA_WINDOW, KV_W), F32)

    @pl.when(g == 0)
    def _():
        pa_scr[...] = _dot(x_ref[0, 0:tt, :].astype(BF16), w_in_ref[...])

    def project_next_slabs(p_nxt, slabs, count):
        for _ in range(count):
            c = next(slabs)
            cols = slice(c * MXU_COLS, (c + 1) * MXU_COLS)
            p_nxt[:, cols] = _dot(xb_scr[...], w_in_ref[:, cols])

    def tile(r0x, p_cur, p_nxt, load_next_x, seq_start):
        rows_x = slice(r0x, r0x + tt)
        tab = lambda i: tab_ref[rows_x, i * LANES:(i + 1) * LANES]
        xb_scr[...] = load_next_x().astype(BF16)
        slabs = iter(range(N_SLABS))

        for h in range(RET_HEADS):
            sl = slice(h * LANES, (h + 1) * LANES)
            q = _rope_ret(p_cur[:, OFF_RQ + h * LANES:OFF_RQ + (h + 1) * LANES], tab(T_RQC), tab(T_RQS))
            k = _rope_ret(p_cur[:, OFF_RK + h * LANES:OFF_RK + (h + 1) * LANES], tab(T_RKC), tab(T_RKS))
            v = p_cur[:, OFF_RV + h * LANES:OFF_RV + (h + 1) * LANES]
            gate = p_cur[:, OFF_RG + h * LANES:OFF_RG + (h + 1) * LANES]
            out, s_new = _retention_head(q, k, v, gate, s_scr[h], dmask_ref[h], qdec_ref[:, sl], kdec_ref[:, sl],
                                         gt_ref[:, sl], gnw_ref[:, sl])
            s_scr[h] = s_new
            mix_scr[:, sl] = out.astype(BF16)
            project_next_slabs(p_nxt, slabs, SLABS_AFTER_RET_HEAD[h])

        k_new = _rope_swa(p_cur[:, OFF_SK:OFF_SK + KV_W], tab(T_SKC), tab(T_SK1), tab(T_SK2))
        v_new = p_cur[:, OFF_SV:OFF_SV + KV_W]
        kext_scr[SWA_WINDOW:SWA_WINDOW + tt, :] = k_new
        vext_scr[SWA_WINDOW:SWA_WINDOW + tt, :] = v_new
        k_var = _kv_variants(kext_scr[...])
        v_var = _kv_variants(vext_scr[...])
        nk = SWA_WINDOW + CHUNK
        n_chunks = tt // CHUNK
        qs = [_rope_swa(p_cur[:, OFF_SQ + m * LANES:OFF_SQ + (m + 1) * LANES],
                        tab(T_SQC), tab(T_SQ1), tab(T_SQ2)).astype(BF16)
              for m in range(N_PAIRS)]
        col = lax.broadcasted_iota(jnp.int32, (PAIRS_PER_KV * CHUNK, LANES), 1)
        row = lax.broadcasted_iota(jnp.int32, (PAIRS_PER_KV * CHUNK, 1), 0)
        blocks = [(grp, i) for grp in range(SWA_KV_HEADS) for i in range(n_chunks)]
        rows_of = lambda n: slice(n * PAIRS_PER_KV * CHUNK, (n + 1) * PAIRS_PER_KV * CHUNK)
        for n, (grp, i) in enumerate(blocks):
            r0 = i * CHUNK
            q_blk = jnp.concatenate([qs[grp * PAIRS_PER_KV + a][r0:r0 + CHUNK] for a in range(PAIRS_PER_KV)], axis=0)
            sw_s_scr[rows_of(n), :] = _dot_nt(q_blk, _pair_rows(k_var[2 * grp], k_var[2 * grp + 1], r0, nk))
        for n, (grp, i) in enumerate(blocks):
            sink_e = sinks_ref[2 * grp * PAIRS_PER_KV]
            sink_o = sinks_ref[2 * grp * PAIRS_PER_KV + 1]
            for a in range(1, PAIRS_PER_KV):
                sink_e = jnp.where(row >= a * CHUNK, sinks_ref[2 * (grp * PAIRS_PER_KV + a)], sink_e)
                sink_o = jnp.where(row >= a * CHUNK, sinks_ref[2 * (grp * PAIRS_PER_KV + a) + 1], sink_o)
            valid = None
            if seq_start is not None and i < SWA_WINDOW // CHUNK:
                valid = jnp.logical_or(col >= SWA_WINDOW - i * CHUNK, jnp.logical_not(seq_start))
            e, den = _swa_softmax(sw_s_scr[rows_of(n), :], nk, sink_e, sink_o, valid)
            sw_e_scr[rows_of(n), :] = e
            sw_d_scr[rows_of(n), :] = den
            project_next_slabs(p_nxt, slabs, SLABS_AFTER_SOFTMAX_BLOCK[n])
        for n, (grp, i) in enumerate(blocks):
            r0 = i * CHUNK
            vt = _pair_rows(v_var[2 * grp], v_var[2 * grp + 1], r0, nk)
            o = (_dot(sw_e_scr[rows_of(n), :], vt) / sw_d_scr[rows_of(n), :]).astype(BF16)
            for a in range(PAIRS_PER_KV):
                m = grp * PAIRS_PER_KV + a
                mix_scr[r0:r0 + CHUNK, RET_W + m * LANES:RET_W + (m + 1) * LANES] = o[a * CHUNK:(a + 1) * CHUNK]

        kext_scr[0:SWA_WINDOW, :] = k_new[tt - SWA_WINDOW:, :]
        vext_scr[0:SWA_WINDOW, :] = v_new[tt - SWA_WINDOW:, :]

        x1_ref[0, rows_x, :] = ALPHA * x_ref[0, rows_x, :] + _dot(mix_scr[...], w_out_ref[...])
        lb = tt // LN_BLOCKS
        for r in range(LN_BLOCKS):
            rows_ln = slice(r0x + r * lb, r0x + (r + 1) * lb)
            x1_ref[0, rows_ln, :] = _layer_norm(x1_ref[0, rows_ln, :], ln1w_ref[...], ln1b_ref[...])
            project_next_slabs(p_nxt, slabs, SLABS_AFTER_LN_BLOCK[r])
        assert next(slabs, None) is None
        return k_new, v_new

    tile(0, pa_scr, pb_scr, lambda: x_ref[0, tt:2 * tt, :], seq_step == 0)
    k_new, v_new = tile(tt, pb_scr, pa_scr, lambda: xn_ref[0], None)

    @pl.when(seq_step == steps_per_seq - 1)
    def _():
        kout_ref[0] = k_new[tt - SWA_WINDOW:, :]
        vout_ref[0] = v_new[tt - SWA_WINDOW:, :]
        sout_ref[0] = s_scr[...]


def _prompt_mixer(x, tab, dmask, qdec, kdec, gt, sinks, w_in, gnw, w_out, ln1w, ln1b):
    b, t, d = x.shape
    tt = PROMPT_TILE
    sps = t // (2 * tt)
    n_steps = b * sps
    const2 = lambda g: (0, 0)
    const3 = lambda g: (0, 0, 0)

    def next_tile(g):
        gn = jnp.minimum(g + 1, n_steps - 1)
        return (gn // sps, (gn % sps) * 2, 0)

    resident = dict(pipeline_mode=pl.Buffered(1))
    return pl.pallas_call(
        functools.partial(_prompt_mixer_kernel, steps_per_seq=sps),
        grid=(n_steps,),
        in_specs=[
            pl.BlockSpec(memory_space=pltpu.SMEM),
            pl.BlockSpec((1, 2 * tt, d), lambda g: (g // sps, g % sps, 0)),
            pl.BlockSpec((1, tt, d), next_tile),
            pl.BlockSpec((2 * tt, N_TABS * LANES), lambda g: (g % sps, 0)),
            pl.BlockSpec((RET_HEADS, tt, tt), const3, **resident),
            pl.BlockSpec((tt, RET_W), const2, **resident),
            pl.BlockSpec((tt, RET_W), const2, **resident),
            pl.BlockSpec((1, RET_W), const2),
            pl.BlockSpec((d, PROJ_W), const2, **resident),
            pl.BlockSpec((1, RET_W), const2),
            pl.BlockSpec((MIX_W, d), const2, **resident),
            pl.BlockSpec((1, d), const2),
            pl.BlockSpec((1, d), const2),
        ],
        out_specs=[
            pl.BlockSpec((1, 2 * tt, d), lambda g: (g // sps, g % sps, 0)),
            pl.BlockSpec((1, SWA_WINDOW, KV_W), lambda g: (g // sps, 0, 0)),
            pl.BlockSpec((1, SWA_WINDOW, KV_W), lambda g: (g // sps, 0, 0)),
            pl.BlockSpec((1, RET_HEADS, RET_DK, RET_DV), lambda g: (g // sps, 0, 0, 0)),
        ],
        out_shape=[
            jax.ShapeDtypeStruct((b, t, d), F32),
            jax.ShapeDtypeStruct((b, SWA_WINDOW, KV_W), F32),
            jax.ShapeDtypeStruct((b, SWA_WINDOW, KV_W), F32),
            jax.ShapeDtypeStruct((b, RET_HEADS, RET_DK, RET_DV), F32),
        ],
        scratch_shapes=[
            pltpu.VMEM((tt, PROJ_W), F32),
            pltpu.VMEM((tt, PROJ_W), F32),
            pltpu.VMEM((tt, d), BF16),
            pltpu.VMEM((RET_HEADS, RET_DK, RET_DV), F32),
            pltpu.VMEM((SWA_WINDOW + tt, KV_W), F32),
            pltpu.VMEM((SWA_WINDOW + tt, KV_W), F32),
            pltpu.VMEM((tt, MIX_W), BF16),
            pltpu.VMEM((N_PAIRS * tt, 2 * (SWA_WINDOW + CHUNK)), F32),
            pltpu.VMEM((N_PAIRS * tt, 2 * (SWA_WINDOW + CHUNK)), BF16),
            pltpu.VMEM((N_PAIRS * tt, LANES), F32),
        ],
        compiler_params=pltpu.CompilerParams(
            dimension_semantics=("arbitrary",), vmem_limit_bytes=VMEM_LIMIT),
        name="prompt_mixer",
    )(sinks, x, x, tab, dmask, qdec, kdec, gt, w_in, gnw, w_out, ln1w, ln1b)


def _sample_mixer_kernel(sinks_ref, x_ref, tab_ref, dmask_ref, qdec_ref, kdec_ref, gt_ref, ck_ref, cv_ref,
                         state_ref, w_in_ref, gnw_ref, w_out_ref, ln1w_ref, ln1b_ref,
                         x1_ref, kout_ref, vout_ref, sout_ref,
                         p_scr, mix_scr, *, n_new):
    s_idx = pl.program_id(0)
    last_s = pl.num_programs(0) - 1
    ns = SAMPLE_STREAMS
    t = n_new
    r_step = ns * t

    @pl.when(s_idx == 0)
    def _():
        p_scr[...] = _dot(x_ref[...].astype(BF16), w_in_ref[...])

    rows = pl.ds(pl.multiple_of(s_idx * r_step, r_step), r_step)
    of_stream = lambda a, si: a[si * t:(si + 1) * t]
    heads = range(RET_HEADS)
    streams = range(ns)
    groups = range(SWA_KV_HEADS)

    head_cols = lambda off, h: slice(off + h * LANES, off + (h + 1) * LANES)
    lanes_of = lambda h: slice(h * LANES, (h + 1) * LANES)
    q_r = [_rope_ret(p_scr[rows, head_cols(OFF_RQ, h)], _tab(tab_ref, T_RQC), _tab(tab_ref, T_RQS)) for h in heads]
    k_r = [_rope_ret(p_scr[rows, head_cols(OFF_RK, h)], _tab(tab_ref, T_RKC), _tab(tab_ref, T_RKS)) for h in heads]
    qb = [q.astype(BF16) for q in q_r]
    kb = [k.astype(BF16) for k in k_r]
    kd = [(k_r[h] * kdec_ref[:, lanes_of(h)]).astype(BF16) for h in heads]
    vb = [p_scr[rows, head_cols(OFF_RV, h)].astype(BF16) for h in heads]
    k_new = _rope_swa(p_scr[rows, OFF_SK:OFF_SK + KV_W], _tab(tab_ref, T_SKC), _tab(tab_ref, T_SK1), _tab(tab_ref, T_SK2))
    v_new = p_scr[rows, OFF_SV:OFF_SV + KV_W]
    kout_ref[...] = k_new
    vout_ref[...] = v_new
    qs = [_rope_swa(p_scr[rows, head_cols(OFF_SQ, m)],
                    _tab(tab_ref, T_SQC), _tab(tab_ref, T_SQ1), _tab(tab_ref, T_SQ2)).astype(BF16)
          for m in range(N_PAIRS)]
    nk = ck_ref.shape[1] + t
    k_var = [_kv_variants(jnp.concatenate([ck_ref[si], of_stream(k_new, si)], axis=0)) for si in streams]
    v_var = [_kv_variants(jnp.concatenate([cv_ref[si], of_stream(v_new, si)], axis=0)) for si in streams]

    scores = [_dot_nt(qb[h], kb[h]) for h in heads]
    inter = [[_dot(of_stream(qb[h], si), state_ref[si, h].astype(BF16)) for si in streams] for h in heads]
    ktv = [[_dot_tn(of_stream(kd[h], si), of_stream(vb[h], si)) for si in streams] for h in heads]
    sw_s = [[_dot_nt(jnp.concatenate([of_stream(qs[grp * PAIRS_PER_KV + a], si) for a in range(PAIRS_PER_KV)], axis=0),
                     _pair_rows(k_var[si][2 * grp], k_var[si][2 * grp + 1], 0, nk))
             for grp in groups] for si in streams]

    scores = [(scores[h] * dmask_ref[h]).astype(BF16) for h in heads]
    for h in heads:
        for si in streams:
            sout_ref[si, h] = gt_ref[:, lanes_of(h)] * state_ref[si, h] + ktv[h][si]
    row = lax.broadcasted_iota(jnp.int32, (PAIRS_PER_KV * t, 1), 0)
    sw_e = []
    for si in streams:
        per_group = []
        for grp in groups:
            sink_e = sinks_ref[2 * grp * PAIRS_PER_KV]
            sink_o = sinks_ref[2 * grp * PAIRS_PER_KV + 1]
            for a in range(1, PAIRS_PER_KV):
                sink_e = jnp.where(row >= a * t, sinks_ref[2 * (grp * PAIRS_PER_KV + a)], sink_e)
                sink_o = jnp.where(row >= a * t, sinks_ref[2 * (grp * PAIRS_PER_KV + a) + 1], sink_o)
            per_group.append(_swa_softmax(sw_s[si][grp], nk, sink_e, sink_o, None))
        sw_e.append(per_group)

    intra = [_dot(scores[h], vb[h]) for h in heads]
    sw_o = [[_dot(sw_e[si][grp][0], _pair_rows(v_var[si][2 * grp], v_var[si][2 * grp + 1], 0, nk))
             for grp in groups] for si in streams]

    for h in heads:
        o = intra[h] + jnp.concatenate(inter[h], axis=0) * qdec_ref[:, lanes_of(h)]
        mu = jnp.mean(o, axis=-1, keepdims=True)
        oc = o - mu
        var = jnp.mean(oc * oc, axis=-1, keepdims=True)
        on = oc * lax.rsqrt(var + GN_EPS) * gnw_ref[:, lanes_of(h)]
        mix_scr[rows, lanes_of(h)] = (jax.nn.silu(p_scr[rows, head_cols(OFF_RG, h)]) * on).astype(BF16)
    sw_o = [[(sw_o[si][grp] / sw_e[si][grp][1]).astype(BF16) for grp in groups] for si in streams]
    for m in range(N_PAIRS):
        grp, a = divmod(m, PAIRS_PER_KV)
        o = jnp.concatenate([sw_o[si][grp][a * t:(a + 1) * t] for si in streams], axis=0)
        mix_scr[rows, RET_W + m * LANES:RET_W + (m + 1) * LANES] = o

    @pl.when(s_idx == last_s)
    def _():
        z = ALPHA * x_ref[...] + _dot(mix_scr[...], w_out_ref[...])
        x1_ref[...] = _layer_norm(z, ln1w_ref[...], ln1b_ref[...])


def _sample_mixer(x, tab, dmask, qdec, kdec, gt, cache_k, cache_v, state, sinks, w_in, gnw, w_out, ln1w, ln1b):
    nb, n_new, d = x.shape
    rows = nb * n_new
    win = cache_k.shape[1]
    ns = SAMPLE_STREAMS
    r_step = ns * n_new
    const2 = lambda si: (0, 0)
    const3 = lambda si: (0, 0, 0)
    return pl.pallas_call(
        functools.partial(_sample_mixer_kernel, n_new=n_new),
        grid=(nb // ns,),
        in_specs=[
            pl.BlockSpec(memory_space=pltpu.SMEM),
            pl.BlockSpec((rows, d), const2),
            pl.BlockSpec((r_step, N_TABS * LANES), const2),
            pl.BlockSpec((RET_HEADS, r_step, r_step), const3),
            pl.BlockSpec((r_step, RET_W), const2),
            pl.BlockSpec((r_step, RET_W), const2),
            pl.BlockSpec((1, RET_W), const2),
            pl.BlockSpec((ns, win, KV_W), lambda si: (si, 0, 0)),
            pl.BlockSpec((ns, win, KV_W), lambda si: (si, 0, 0)),
            pl.BlockSpec((ns, RET_HEADS, RET_DK, RET_DV), lambda si: (si, 0, 0, 0)),
            pl.BlockSpec((d, PROJ_W), const2),
            pl.BlockSpec((1, RET_W), const2),
            pl.BlockSpec((MIX_W, d), const2),
            pl.BlockSpec((1, d), const2),
            pl.BlockSpec((1, d), const2),
        ],
        out_specs=[
            pl.BlockSpec((rows, d), const2),
            pl.BlockSpec((r_step, KV_W), lambda si: (si, 0)),
            pl.BlockSpec((r_step, KV_W), lambda si: (si, 0)),
            pl.BlockSpec((ns, RET_HEADS, RET_DK, RET_DV), lambda si: (si, 0, 0, 0)),
        ],
        out_shape=[
            jax.ShapeDtypeStruct((rows, d), F32),
            jax.ShapeDtypeStruct((rows, KV_W), F32),
            jax.ShapeDtypeStruct((rows, KV_W), F32),
            jax.ShapeDtypeStruct((nb, RET_HEADS, RET_DK, RET_DV), F32),
        ],
        scratch_shapes=[
            pltpu.VMEM((rows, PROJ_W), F32),
            pltpu.VMEM((rows, MIX_W), BF16),
        ],
        compiler_params=pltpu.CompilerParams(
            dimension_semantics=("arbitrary",), vmem_limit_bytes=VMEM_LIMIT),
        name="sample_mixer",
    )(sinks, x.reshape(rows, d), tab, dmask, qdec, kdec, gt, cache_k, cache_v, state, w_in, gnw, w_out, ln1w, ln1b)


def _mlp_kernel(xp_ref, xs_ref, w_up_ref, w_down_ref, lnw_ref, lnb_ref, yp_ref, ys_ref, h_scr, *, prompt_steps):
    i = pl.program_id(0)

    def block(x_ref, y_ref):
        for r0 in range(0, x_ref.shape[0], MLP_UP_ROWS):
            n = min(MLP_UP_ROWS, x_ref.shape[0] - r0)
            h = _dot(x_ref[r0:r0 + n, :].astype(BF16), w_up_ref[...])
            h_scr[0:n, :] = jnp.square(jnp.maximum(h, 0.0)).astype(BF16)
            for r in range(0, n, MLP_LN_ROWS):
                rows = slice(r0 + r, r0 + r + MLP_LN_ROWS)
                z = ALPHA * x_ref[rows, :] + _dot(h_scr[r:r + MLP_LN_ROWS, :], w_down_ref[...])
                y_ref[rows, :] = _layer_norm(z, lnw_ref[...], lnb_ref[...])

    @pl.when(i < prompt_steps)
    def _():
        block(xp_ref, yp_ref)

    @pl.when(i == prompt_steps)
    def _():
        block(xs_ref, ys_ref)


def _mlp(xp, xs, w_up, w_down, lnw, lnb):
    rows, d = xp.shape
    rows_s = xs.shape[0]
    steps = rows // MLP_TILE
    const2 = lambda i: (0, 0)
    prompt_tile = lambda i: (jnp.minimum(i, steps - 1), 0)
    resident = dict(pipeline_mode=pl.Buffered(1))
    return pl.pallas_call(
        functools.partial(_mlp_kernel, prompt_steps=steps),
        grid=(steps + 1,),
        in_specs=[
            pl.BlockSpec((MLP_TILE, d), prompt_tile),
            pl.BlockSpec((rows_s, d), const2, **resident),
            pl.BlockSpec((d, D_FF), const2, **resident),
            pl.BlockSpec((D_FF, d), const2, **resident),
            pl.BlockSpec((1, d), const2),
            pl.BlockSpec((1, d), const2),
        ],
        out_specs=[
            pl.BlockSpec((MLP_TILE, d), prompt_tile),
            pl.BlockSpec((rows_s, d), const2),
        ],
        out_shape=[
            jax.ShapeDtypeStruct((rows, d), F32),
            jax.ShapeDtypeStruct((rows_s, d), F32),
        ],
        scratch_shapes=[pltpu.VMEM((MLP_UP_ROWS, D_FF), BF16)],
        compiler_params=pltpu.CompilerParams(
            dimension_semantics=("arbitrary",), vmem_limit_bytes=VMEM_LIMIT),
        name="mlp",
    )(xp, xs, w_up, w_down, lnw, lnb)


def _rope_tables(pos):
    posf = np.asarray(pos, np.float64)[:, None]
    half_r = RET_DK // 2
    ang = posf * (RET_THETA ** (-np.arange(half_r, dtype=np.float64) / half_r))[None, :]
    c, s = np.cos(ang), np.sin(ang)
    rc = np.concatenate([c, c], axis=-1)
    rs = np.concatenate([-s, s], axis=-1)
    k_scale = RET_DK ** -0.5
    half_s = SWA_ROT_DIM // 2
    ang = posf * (SWA_THETA ** (-np.arange(half_s, dtype=np.float64) / half_s))[None, :]
    c, s = np.cos(ang), np.sin(ang)
    t = posf.shape[0]
    rest = SWA_HD - SWA_ROT_DIM
    z = np.zeros((t, half_s))
    sc = np.tile(np.concatenate([c, c, np.ones((t, rest))], axis=-1), (1, LANES // SWA_HD))
    s1 = np.tile(np.concatenate([z, s, np.zeros((t, rest))], axis=-1), (1, LANES // SWA_HD))
    s2 = np.tile(np.concatenate([-s, z, np.zeros((t, rest))], axis=-1), (1, LANES // SWA_HD))
    q_scale = SWA_HD ** -0.5
    tab = np.concatenate([rc, rs, rc * k_scale, rs * k_scale,
                          sc * q_scale, s1 * q_scale, s2 * q_scale, sc, s1, s2], axis=-1)
    return jnp.asarray(tab.astype(np.float32))


def _decay_tables(t, copies=1):
    lg = np.log1p(-(2.0 ** (-5.0 - np.arange(RET_HEADS, dtype=np.float64))))
    idx = np.arange(t, dtype=np.float64)
    diff = idx[:, None] - idx[None, :]
    dmask = np.where(diff >= 0, np.exp(lg[:, None, None] * np.maximum(diff, 0.0)), 0.0)
    qdec = np.repeat(np.exp(lg[None, :] * (idx[:, None] + 1.0)), RET_DV, axis=1)
    kdec = np.repeat(np.exp(lg[None, :] * (t - 1.0 - idx[:, None])), RET_DV, axis=1)
    gt = np.repeat(np.exp(lg * t)[None, :], RET_DV, axis=1)
    if copies > 1:
        dmask = np.stack([np.kron(np.eye(copies), m) for m in dmask])
        qdec, kdec = np.tile(qdec, (copies, 1)), np.tile(kdec, (copies, 1))
    return tuple(jnp.asarray(a.astype(np.float32)) for a in (dmask, qdec, kdec, gt))


def kernel(x_prompt, x_sample, cache_swa_k, cache_swa_v, state_ret, w_in, ret_gn_w, swa_sinks,
           w_out, ln1_w, ln1_b, w_up, w_down, ln2_w, ln2_b):
    assert w_in.shape[0] == DEPTH == 1
    b, t, d = x_prompt.shape
    nb, n_new, _ = x_sample.shape
    win = cache_swa_k.shape[2]

    w_in_b = w_in[0].astype(BF16)
    w_out_b = w_out[0].astype(BF16)
    w_up_b = w_up[0].astype(BF16)
    w_down_b = w_down[0].astype(BF16)
    sinks = swa_sinks[0]

    tab_p = _rope_tables(np.arange(t))
    tab_s = _rope_tables(np.tile(PAST_LEN + np.arange(n_new), SAMPLE_STREAMS))
    dec_p = _decay_tables(PROMPT_TILE)
    dec_s = _decay_tables(n_new, SAMPLE_STREAMS)

    x1_p, k_p, v_p, r_p = _prompt_mixer(x_prompt, tab_p, *dec_p, sinks, w_in_b, ret_gn_w, w_out_b, ln1_w, ln1_b)
    x1_s, k_s, v_s, r_s = _sample_mixer(
        x_sample, tab_s, *dec_s, cache_swa_k[0].reshape(nb, win, KV_W), cache_swa_v[0].reshape(nb, win, KV_W),
        state_ret[0], sinks, w_in_b, ret_gn_w, w_out_b, ln1_w, ln1_b)
    y_p, y_s = _mlp(x1_p.reshape(b * t, d), x1_s, w_up_b, w_down_b, ln2_w, ln2_b)

    return (y_p.reshape(b, t, d), y_s.reshape(nb, n_new, d),
            k_p.reshape(DEPTH, b, SWA_WINDOW, SWA_KV_HEADS, SWA_HD),
            v_p.reshape(DEPTH, b, SWA_WINDOW, SWA_KV_HEADS, SWA_HD),
            r_p.reshape(DEPTH, b, RET_HEADS, RET_DK, RET_DV),
            k_s.reshape(DEPTH, nb, n_new, SWA_KV_HEADS, SWA_HD),
            v_s.reshape(DEPTH, nb, n_new, SWA_KV_HEADS, SWA_HD),
            r_s.reshape(DEPTH, nb, RET_HEADS, RET_DK, RET_DV))
```

```python
import functools

import numpy as np
import jax
import jax.numpy as jnp
from jax import lax
from jax.experimental import pallas as pl
from jax.experimental.pallas import tpu as pltpu

D_MODEL = 1024
DEPTH = 1
PAST_LEN = 2048
CHUNK = 64
RET_HEADS = 4
RET_DK = 128
RET_DV = 128
RET_THETA = 10000.0
RET_W = RET_HEADS * RET_DV
SWA_HEADS = 8
SWA_KV_HEADS = 2
SWA_HD = 64
SWA_WINDOW = 128
SWA_ROT_DIM = SWA_HD // 4
SWA_THETA = 500000.0
SWA_W = SWA_HEADS * SWA_HD
KV_W = SWA_KV_HEADS * SWA_HD
MIX_W = RET_W + SWA_W
D_FF = 4 * D_MODEL
OFF_RQ, OFF_RK, OFF_RV, OFF_RG = 0, RET_W, 2 * RET_W, 3 * RET_W
OFF_SQ = 4 * RET_W
OFF_SK = OFF_SQ + SWA_W
OFF_SV = OFF_SK + KV_W
PROJ_W = OFF_SV + KV_W
ALPHA = (2.0 * DEPTH) ** 0.25
LN_EPS = 1e-5
GN_EPS = 1e-5
NEG_INF = -1e30

LANES = 128
MXU_COLS = 256
N_PAIRS = SWA_W // LANES
PAIRS_PER_KV = N_PAIRS // SWA_KV_HEADS
T_RQC, T_RQS, T_RKC, T_RKS, T_SQC, T_SQ1, T_SQ2, T_SKC, T_SK1, T_SK2 = range(10)
N_TABS = 10

PROMPT_TILE = 256
N_SLABS = PROJ_W // MXU_COLS
LN_BLOCKS = 4
SLABS_AFTER_RET_HEAD = (1, 1, 1, 0)
SLABS_AFTER_SOFTMAX_BLOCK = (1, 1, 1, 1, 1, 1, 0, 0)
SLABS_AFTER_LN_BLOCK = (1, 1, 0, 0)
SAMPLE_STREAMS = 4
MLP_TILE = 1024
MLP_UP_ROWS = 512
MLP_LN_ROWS = 128
VMEM_LIMIT = 56 * 1024 * 1024

F32 = jnp.float32
BF16 = jnp.bfloat16


def _tab(tab_ref, i):
    return tab_ref[:, i * LANES:(i + 1) * LANES]


def _dot(a, b):
    return jnp.dot(a, b, preferred_element_type=F32)


def _dot_nt(a, b):
    return lax.dot_general(a, b, (((1,), (1,)), ((), ())), preferred_element_type=F32)


def _dot_tn(a, b):
    return lax.dot_general(a, b, (((0,), (0,)), ((), ())), preferred_element_type=F32)


def _layer_norm(z, w, b):
    mu = jnp.mean(z, axis=-1, keepdims=True)
    zc = z - mu
    var = jnp.mean(zc * zc, axis=-1, keepdims=True)
    return zc * lax.rsqrt(var + LN_EPS) * w + b


def _rope_ret(x, c, s):
    return x * c + pltpu.roll(x, RET_DK // 2, axis=1) * s


def _rope_swa(x, c, s1, s2):
    half = SWA_ROT_DIM // 2
    return x * c + pltpu.roll(x, half, axis=1) * s1 + pltpu.roll(x, LANES - half, axis=1) * s2


def _retention_head(q, k, v, g, s_prev, dmask, qdec, kdec, gt, gnw):
    qb = q.astype(BF16)
    kb = k.astype(BF16)
    vb = v.astype(BF16)
    scores = _dot_nt(qb, kb) * dmask
    o = _dot(scores.astype(BF16), vb)
    o = o + _dot(qb, s_prev.astype(BF16)) * qdec
    s_new = gt * s_prev + _dot_tn((k * kdec).astype(BF16), vb)
    mu = jnp.mean(o, axis=-1, keepdims=True)
    oc = o - mu
    var = jnp.mean(oc * oc, axis=-1, keepdims=True)
    on = oc * lax.rsqrt(var + GN_EPS) * gnw
    return jax.nn.silu(g) * on, s_new


def _kv_variants(a):
    lane = lax.broadcasted_iota(jnp.int32, a.shape, 1)
    lo = lane < SWA_HD
    ar = pltpu.roll(a, SWA_HD, axis=1)
    zero = jnp.zeros_like(a)
    return (jnp.where(lo, a, zero).astype(BF16), jnp.where(lo, zero, ar).astype(BF16),
            jnp.where(lo, ar, zero).astype(BF16), jnp.where(lo, zero, a).astype(BF16))


def _pair_rows(a_e, a_o, r0, nk):
    return jnp.concatenate([a_e[r0:r0 + LANES], a_o[r0:r0 + LANES],
                            a_e[r0 + LANES:r0 + nk], a_o[r0 + LANES:r0 + nk]], axis=0)


def _swa_softmax(s, nk, sink_e, sink_o, valid):
    r = nk - LANES
    s_e, s_o, s_x = s[:, :LANES], s[:, LANES:2 * LANES], s[:, 2 * LANES:]
    if valid is not None:
        s_e = jnp.where(valid, s_e, NEG_INF)
        s_o = jnp.where(valid, s_o, NEG_INF)
    x_is_e = lax.broadcasted_iota(jnp.int32, s_x.shape, 1) < r
    rowmax = lambda a: jnp.max(a, axis=-1, keepdims=True)
    rowsum = lambda a: jnp.sum(a, axis=-1, keepdims=True)
    x_e = jnp.where(x_is_e, s_x, -jnp.inf)
    x_o = jnp.where(x_is_e, -jnp.inf, s_x)
    if 2 * r == LANES:
        m_e, m_o = rowmax(jnp.maximum(s_e, x_e)), rowmax(jnp.maximum(s_o, x_o))
    else:
        m_e, m_o = jnp.maximum(rowmax(s_e), rowmax(x_e)), jnp.maximum(rowmax(s_o), rowmax(x_o))
    m_e = jnp.maximum(m_e, sink_e)
    m_o = jnp.maximum(m_o, sink_o)
    e_e = jnp.exp(s_e - m_e)
    e_o = jnp.exp(s_o - m_o)
    e_x = jnp.exp(s_x - jnp.where(x_is_e, m_e, m_o))
    x_e = jnp.where(x_is_e, e_x, 0.0)
    x_o = jnp.where(x_is_e, 0.0, e_x)
    if 2 * r == LANES:
        d_e, d_o = rowsum(e_e + x_e), rowsum(e_o + x_o)
    else:
        d_e, d_o = rowsum(e_e) + rowsum(x_e), rowsum(e_o) + rowsum(x_o)
    d_e = d_e + jnp.exp(sink_e - m_e)
    d_o = d_o + jnp.exp(sink_o - m_o)
    lane = lax.broadcasted_iota(jnp.int32, (s.shape[0], LANES), 1)
    e = jnp.concatenate([e_e, e_o, e_x], axis=1).astype(BF16)
    return e, jnp.where(lane < SWA_HD, d_e, d_o)


def _prompt_mixer_kernel(sinks_ref, x_ref, xn_ref, tab_ref, dmask_ref, qdec_ref, kdec_ref, gt_ref, w_in_ref,
                         gnw_ref, w_out_ref, ln1w_ref, ln1b_ref, w_up_f32_ref, w_down_f32_ref,
                         x1_ref, kout_ref, vout_ref, sout_ref, w_up_ref, w_down_ref,
                         pa_scr, pb_scr, xb_scr, s_scr, kext_scr, vext_scr, mix_scr, sw_s_scr, sw_e_scr, sw_d_scr,
                         *, steps_per_seq):
    tt = PROMPT_TILE
    g = pl.program_id(0)
    seq_step = g % steps_per_seq
    w_up_ref[...] = w_up_f32_ref[...].astype(BF16)
    w_down_ref[...] = w_down_f32_ref[...].astype(BF16)

    @pl.when(seq_step == 0)
    def _():
        s_scr[...] = jnp.zeros_like(s_scr)
        kext_scr[0:SWA_WINDOW, :] = jnp.zeros((SWA_WINDOW, KV_W), F32)
        vext_scr[0:SWA_WINDOW, :] = jnp.zeros((SWA_WINDOW, KV_W), F32)

    @pl.when(g == 0)
    def _():
        pa_scr[...] = _dot(x_ref[0, 0:tt, :].astype(BF16), w_in_ref[...])

    def project_next_slabs(p_nxt, slabs, count):
        for _ in range(count):
            c = next(slabs)
            cols = slice(c * MXU_COLS, (c + 1) * MXU_COLS)
            p_nxt[:, cols] = _dot(xb_scr[...], w_in_ref[:, cols])

    def tile(r0x, p_cur, p_nxt, load_next_x, seq_start):
        rows_x = slice(r0x, r0x + tt)
        tab = lambda i: tab_ref[rows_x, i * LANES:(i + 1) * LANES]
        xb_scr[...] = load_next_x().astype(BF16)
        slabs = iter(range(N_SLABS))

        for h in range(RET_HEADS):
            sl = slice(h * LANES, (h + 1) * LANES)
            q = _rope_ret(p_cur[:, OFF_RQ + h * LANES:OFF_RQ + (h + 1) * LANES], tab(T_RQC), tab(T_RQS))
            k = _rope_ret(p_cur[:, OFF_RK + h * LANES:OFF_RK + (h + 1) * LANES], tab(T_RKC), tab(T_RKS))
            v = p_cur[:, OFF_RV + h * LANES:OFF_RV + (h + 1) * LANES]
            gate = p_cur[:, OFF_RG + h * LANES:OFF_RG + (h + 1) * LANES]
            out, s_new = _retention_head(q, k, v, gate, s_scr[h], dmask_ref[h], qdec_ref[:, sl], kdec_ref[:, sl],
                                         gt_ref[:, sl], gnw_ref[:, sl])
            s_scr[h] = s_new
            mix_scr[:, sl] = out.astype(BF16)
            project_next_slabs(p_nxt, slabs, SLABS_AFTER_RET_HEAD[h])

        k_new = _rope_swa(p_cur[:, OFF_SK:OFF_SK + KV_W], tab(T_SKC), tab(T_SK1), tab(T_SK2))
        v_new = p_cur[:, OFF_SV:OFF_SV + KV_W]
        kext_scr[SWA_WINDOW:SWA_WINDOW + tt, :] = k_new
        vext_scr[SWA_WINDOW:SWA_WINDOW + tt, :] = v_new
        k_var = _kv_variants(kext_scr[...])
        v_var = _kv_variants(vext_scr[...])
        nk = SWA_WINDOW + CHUNK
        n_chunks = tt // CHUNK
        qs = [_rope_swa(p_cur[:, OFF_SQ + m * LANES:OFF_SQ + (m + 1) * LANES],
                        tab(T_SQC), tab(T_SQ1), tab(T_SQ2)).astype(BF16)
              for m in range(N_PAIRS)]
        col = lax.broadcasted_iota(jnp.int32, (PAIRS_PER_KV * CHUNK, LANES), 1)
        row = lax.broadcasted_iota(jnp.int32, (PAIRS_PER_KV * CHUNK, 1), 0)
        blocks = [(grp, i) for grp in range(SWA_KV_HEADS) for i in range(n_chunks)]
        rows_of = lambda n: slice(n * PAIRS_PER_KV * CHUNK, (n + 1) * PAIRS_PER_KV * CHUNK)
        for n, (grp, i) in enumerate(blocks):
            r0 = i * CHUNK
            q_blk = jnp.concatenate([qs[grp * PAIRS_PER_KV + a][r0:r0 + CHUNK] for a in range(PAIRS_PER_KV)], axis=0)
            sw_s_scr[rows_of(n), :] = _dot_nt(q_blk, _pair_rows(k_var[2 * grp], k_var[2 * grp + 1], r0, nk))
        for n, (grp, i) in enumerate(blocks):
            sink_e = sinks_ref[2 * grp * PAIRS_PER_KV]
            sink_o = sinks_ref[2 * grp * PAIRS_PER_KV + 1]
            for a in range(1, PAIRS_PER_KV):
                sink_e = jnp.where(row >= a * CHUNK, sinks_ref[2 * (grp * PAIRS_PER_KV + a)], sink_e)
                sink_o = jnp.where(row >= a * CHUNK, sinks_ref[2 * (grp * PAIRS_PER_KV + a) + 1], sink_o)
            valid = None
            if seq_start is not None and i < SWA_WINDOW // CHUNK:
                valid = jnp.logical_or(col >= SWA_WINDOW - i * CHUNK, jnp.logical_not(seq_start))
            e, den = _swa_softmax(sw_s_scr[rows_of(n), :], nk, sink_e, sink_o, valid)
            sw_e_scr[rows_of(n), :] = e
            sw_d_scr[rows_of(n), :] = den
            project_next_slabs(p_nxt, slabs, SLABS_AFTER_SOFTMAX_BLOCK[n])
        for n, (grp, i) in enumerate(blocks):
            r0 = i * CHUNK
            vt = _pair_rows(v_var[2 * grp], v_var[2 * grp + 1], r0, nk)
            o = (_dot(sw_e_scr[rows_of(n), :], vt) / sw_d_scr[rows_of(n), :]).astype(BF16)
            for a in range(PAIRS_PER_KV):
                m = grp * PAIRS_PER_KV + a
                mix_scr[r0:r0 + CHUNK, RET_W + m * LANES:RET_W + (m + 1) * LANES] = o[a * CHUNK:(a + 1) * CHUNK]

        kext_scr[0:SWA_WINDOW, :] = k_new[tt - SWA_WINDOW:, :]
        vext_scr[0:SWA_WINDOW, :] = v_new[tt - SWA_WINDOW:, :]

        x1_ref[0, rows_x, :] = ALPHA * x_ref[0, rows_x, :] + _dot(mix_scr[...], w_out_ref[...])
        lb = tt // LN_BLOCKS
        for r in range(LN_BLOCKS):
            rows_ln = slice(r0x + r * lb, r0x + (r + 1) * lb)
            x1_ref[0, rows_ln, :] = _layer_norm(x1_ref[0, rows_ln, :], ln1w_ref[...], ln1b_ref[...])
            project_next_slabs(p_nxt, slabs, SLABS_AFTER_LN_BLOCK[r])
        assert next(slabs, None) is None
        return k_new, v_new

    tile(0, pa_scr, pb_scr, lambda: x_ref[0, tt:2 * tt, :], seq_step == 0)
    k_new, v_new = tile(tt, pb_scr, pa_scr, lambda: xn_ref[0], None)

    @pl.when(seq_step == steps_per_seq - 1)
    def _():
        kout_ref[0] = k_new[tt - SWA_WINDOW:, :]
        vout_ref[0] = v_new[tt - SWA_WINDOW:, :]
        sout_ref[0] = s_scr[...]


def _prompt_mixer(x, tab, dmask, qdec, kdec, gt, sinks, w_in, gnw, w_out, ln1w, ln1b, w_up, w_down):
    b, t, d = x.shape
    tt = PROMPT_TILE
    sps = t // (2 * tt)
    n_steps = b * sps
    up_rows, down_rows = w_up.shape[0] // n_steps, w_down.shape[0] // n_steps
    row_slice = lambda g: (g, 0)
    const2 = lambda g: (0, 0)
    const3 = lambda g: (0, 0, 0)

    def next_tile(g):
        gn = jnp.minimum(g + 1, n_steps - 1)
        return (gn // sps, (gn % sps) * 2, 0)

    resident = dict(pipeline_mode=pl.Buffered(1))
    return pl.pallas_call(
        functools.partial(_prompt_mixer_kernel, steps_per_seq=sps),
        grid=(n_steps,),
        in_specs=[
            pl.BlockSpec(memory_space=pltpu.SMEM),
            pl.BlockSpec((1, 2 * tt, d), lambda g: (g // sps, g % sps, 0)),
            pl.BlockSpec((1, tt, d), next_tile),
            pl.BlockSpec((2 * tt, N_TABS * LANES), lambda g: (g % sps, 0)),
            pl.BlockSpec((RET_HEADS, tt, tt), const3, **resident),
            pl.BlockSpec((tt, RET_W), const2, **resident),
            pl.BlockSpec((tt, RET_W), const2, **resident),
            pl.BlockSpec((1, RET_W), const2),
            pl.BlockSpec((d, PROJ_W), const2, **resident),
            pl.BlockSpec((1, RET_W), const2),
            pl.BlockSpec((MIX_W, d), const2, **resident),
            pl.BlockSpec((1, d), const2),
            pl.BlockSpec((1, d), const2),
            pl.BlockSpec((up_rows, w_up.shape[1]), row_slice),
            pl.BlockSpec((down_rows, w_down.shape[1]), row_slice),
        ],
        out_specs=[
            pl.BlockSpec((1, 2 * tt, d), lambda g: (g // sps, g % sps, 0)),
            pl.BlockSpec((1, SWA_WINDOW, KV_W), lambda g: (g // sps, 0, 0)),
            pl.BlockSpec((1, SWA_WINDOW, KV_W), lambda g: (g // sps, 0, 0)),
            pl.BlockSpec((1, RET_HEADS, RET_DK, RET_DV), lambda g: (g // sps, 0, 0, 0)),
            pl.BlockSpec((up_rows, w_up.shape[1]), row_slice),
            pl.BlockSpec((down_rows, w_down.shape[1]), row_slice),
        ],
        out_shape=[
            jax.ShapeDtypeStruct((b, t, d), F32),
            jax.ShapeDtypeStruct((b, SWA_WINDOW, KV_W), F32),
            jax.ShapeDtypeStruct((b, SWA_WINDOW, KV_W), F32),
            jax.ShapeDtypeStruct((b, RET_HEADS, RET_DK, RET_DV), F32),
            jax.ShapeDtypeStruct(w_up.shape, BF16),
            jax.ShapeDtypeStruct(w_down.shape, BF16),
        ],
        scratch_shapes=[
            pltpu.VMEM((tt, PROJ_W), F32),
            pltpu.VMEM((tt, PROJ_W), F32),
            pltpu.VMEM((tt, d), BF16),
            pltpu.VMEM((RET_HEADS, RET_DK, RET_DV), F32),
            pltpu.VMEM((SWA_WINDOW + tt, KV_W), F32),
            pltpu.VMEM((SWA_WINDOW + tt, KV_W), F32),
            pltpu.VMEM((tt, MIX_W), BF16),
            pltpu.VMEM((N_PAIRS * tt, 2 * (SWA_WINDOW + CHUNK)), F32),
            pltpu.VMEM((N_PAIRS * tt, 2 * (SWA_WINDOW + CHUNK)), BF16),
            pltpu.VMEM((N_PAIRS * tt, LANES), F32),
        ],
        compiler_params=pltpu.CompilerParams(
            dimension_semantics=("arbitrary",), vmem_limit_bytes=VMEM_LIMIT),
        name="prompt_mixer",
    )(sinks, x, x, tab, dmask, qdec, kdec, gt, w_in, gnw, w_out, ln1w, ln1b, w_up, w_down)


def _sample_mixer_kernel(sinks_ref, x_ref, tab_ref, dmask_ref, qdec_ref, kdec_ref, gt_ref, ck_ref, cv_ref,
                         state_ref, w_in_f32_ref, gnw_ref, w_out_f32_ref, ln1w_ref, ln1b_ref,
                         x1_ref, kout_ref, vout_ref, sout_ref, w_in_ref, w_out_ref,
                         p_scr, mix_scr, *, n_new):
    s_idx = pl.program_id(0)
    last_s = pl.num_programs(0) - 1
    ns = SAMPLE_STREAMS
    t = n_new
    r_step = ns * t

    @pl.when(s_idx == 0)
    def _():
        w_in_ref[...] = w_in_f32_ref[...].astype(BF16)
        w_out_ref[...] = w_out_f32_ref[...].astype(BF16)
        p_scr[...] = _dot(x_ref[...].astype(BF16), w_in_ref[...])

    rows = pl.ds(pl.multiple_of(s_idx * r_step, r_step), r_step)
    of_stream = lambda a, si: a[si * t:(si + 1) * t]
    heads = range(RET_HEADS)
    streams = range(ns)
    groups = range(SWA_KV_HEADS)

    head_cols = lambda off, h: slice(off + h * LANES, off + (h + 1) * LANES)
    lanes_of = lambda h: slice(h * LANES, (h + 1) * LANES)
    q_r = [_rope_ret(p_scr[rows, head_cols(OFF_RQ, h)], _tab(tab_ref, T_RQC), _tab(tab_ref, T_RQS)) for h in heads]
    k_r = [_rope_ret(p_scr[rows, head_cols(OFF_RK, h)], _tab(tab_ref, T_RKC), _tab(tab_ref, T_RKS)) for h in heads]
    qb = [q.astype(BF16) for q in q_r]
    kb = [k.astype(BF16) for k in k_r]
    kd = [(k_r[h] * kdec_ref[:, lanes_of(h)]).astype(BF16) for h in heads]
    vb = [p_scr[rows, head_cols(OFF_RV, h)].astype(BF16) for h in heads]
    k_new = _rope_swa(p_scr[rows, OFF_SK:OFF_SK + KV_W], _tab(tab_ref, T_SKC), _tab(tab_ref, T_SK1), _tab(tab_ref, T_SK2))
    v_new = p_scr[rows, OFF_SV:OFF_SV + KV_W]
    kout_ref[...] = k_new
    vout_ref[...] = v_new
    qs = [_rope_swa(p_scr[rows, head_cols(OFF_SQ, m)],
                    _tab(tab_ref, T_SQC), _tab(tab_ref, T_SQ1), _tab(tab_ref, T_SQ2)).astype(BF16)
          for m in range(N_PAIRS)]
    nk = ck_ref.shape[1] + t
    k_var = [_kv_variants(jnp.concatenate([ck_ref[si], of_stream(k_new, si)], axis=0)) for si in streams]
    v_var = [_kv_variants(jnp.concatenate([cv_ref[si], of_stream(v_new, si)], axis=0)) for si in streams]

    scores = [_dot_nt(qb[h], kb[h]) for h in heads]
    inter = [[_dot(of_stream(qb[h], si), state_ref[si, h].astype(BF16)) for si in streams] for h in heads]
    ktv = [[_dot_tn(of_stream(kd[h], si), of_stream(vb[h], si)) for si in streams] for h in heads]
    sw_s = [[_dot_nt(jnp.concatenate([of_stream(qs[grp * PAIRS_PER_KV + a], si) for a in range(PAIRS_PER_KV)], axis=0),
                     _pair_rows(k_var[si][2 * grp], k_var[si][2 * grp + 1], 0, nk))
             for grp in groups] for si in streams]

    scores = [(scores[h] * dmask_ref[h]).astype(BF16) for h in heads]
    for h in heads:
        for si in streams:
            sout_ref[si, h] = gt_ref[:, lanes_of(h)] * state_ref[si, h] + ktv[h][si]
    row = lax.broadcasted_iota(jnp.int32, (PAIRS_PER_KV * t, 1), 0)
    sw_e = []
    for si in streams:
        per_group = []
        for grp in groups:
            sink_e = sinks_ref[2 * grp * PAIRS_PER_KV]
            sink_o = sinks_ref[2 * grp * PAIRS_PER_KV + 1]
            for a in range(1, PAIRS_PER_KV):
                sink_e = jnp.where(row >= a * t, sinks_ref[2 * (grp * PAIRS_PER_KV + a)], sink_e)
                sink_o = jnp.where(row >= a * t, sinks_ref[2 * (grp * PAIRS_PER_KV + a) + 1], sink_o)
            per_group.append(_swa_softmax(sw_s[si][grp], nk, sink_e, sink_o, None))
        sw_e.append(per_group)

    intra = [_dot(scores[h], vb[h]) for h in heads]
    sw_o = [[_dot(sw_e[si][grp][0], _pair_rows(v_var[si][2 * grp], v_var[si][2 * grp + 1], 0, nk))
             for grp in groups] for si in streams]

    for h in heads:
        o = intra[h] + jnp.concatenate(inter[h], axis=0) * qdec_ref[:, lanes_of(h)]
        mu = jnp.mean(o, axis=-1, keepdims=True)
        oc = o - mu
        var = jnp.mean(oc * oc, axis=-1, keepdims=True)
        on = oc * lax.rsqrt(var + GN_EPS) * gnw_ref[:, lanes_of(h)]
        mix_scr[rows, lanes_of(h)] = (jax.nn.silu(p_scr[rows, head_cols(OFF_RG, h)]) * on).astype(BF16)
    sw_o = [[(sw_o[si][grp] / sw_e[si][grp][1]).astype(BF16) for grp in groups] for si in streams]
    for m in range(N_PAIRS):
        grp, a = divmod(m, PAIRS_PER_KV)
        o = jnp.concatenate([sw_o[si][grp][a * t:(a + 1) * t] for si in streams], axis=0)
        mix_scr[rows, RET_W + m * LANES:RET_W + (m + 1) * LANES] = o

    @pl.when(s_idx == last_s)
    def _():
        z = ALPHA * x_ref[...] + _dot(mix_scr[...], w_out_ref[...])
        x1_ref[...] = _layer_norm(z, ln1w_ref[...], ln1b_ref[...])


def _sample_mixer(x, tab, dmask, qdec, kdec, gt, cache_k, cache_v, state, sinks, w_in, gnw, w_out, ln1w, ln1b):
    nb, n_new, d = x.shape
    rows = nb * n_new
    win = cache_k.shape[1]
    ns = SAMPLE_STREAMS
    r_step = ns * n_new
    const2 = lambda si: (0, 0)
    const3 = lambda si: (0, 0, 0)
    return pl.pallas_call(
        functools.partial(_sample_mixer_kernel, n_new=n_new),
        grid=(nb // ns,),
        in_specs=[
            pl.BlockSpec(memory_space=pltpu.SMEM),
            pl.BlockSpec((rows, d), const2),
            pl.BlockSpec((r_step, N_TABS * LANES), const2),
            pl.BlockSpec((RET_HEADS, r_step, r_step), const3),
            pl.BlockSpec((r_step, RET_W), const2),
            pl.BlockSpec((r_step, RET_W), const2),
            pl.BlockSpec((1, RET_W), const2),
            pl.BlockSpec((ns, win, KV_W), lambda si: (si, 0, 0)),
            pl.BlockSpec((ns, win, KV_W), lambda si: (si, 0, 0)),
            pl.BlockSpec((ns, RET_HEADS, RET_DK, RET_DV), lambda si: (si, 0, 0, 0)),
            pl.BlockSpec((d, PROJ_W), const2, pipeline_mode=pl.Buffered(1)),
            pl.BlockSpec((1, RET_W), const2),
            pl.BlockSpec((MIX_W, d), const2, pipeline_mode=pl.Buffered(1)),
            pl.BlockSpec((1, d), const2),
            pl.BlockSpec((1, d), const2),
        ],
        out_specs=[
            pl.BlockSpec((rows, d), const2),
            pl.BlockSpec((r_step, KV_W), lambda si: (si, 0)),
            pl.BlockSpec((r_step, KV_W), lambda si: (si, 0)),
            pl.BlockSpec((ns, RET_HEADS, RET_DK, RET_DV), lambda si: (si, 0, 0, 0)),
            pl.BlockSpec((d, PROJ_W), const2),
            pl.BlockSpec((MIX_W, d), const2),
        ],
        out_shape=[
            jax.ShapeDtypeStruct((rows, d), F32),
            jax.ShapeDtypeStruct((rows, KV_W), F32),
            jax.ShapeDtypeStruct((rows, KV_W), F32),
            jax.ShapeDtypeStruct((nb, RET_HEADS, RET_DK, RET_DV), F32),
            jax.ShapeDtypeStruct((d, PROJ_W), BF16),
            jax.ShapeDtypeStruct((MIX_W, d), BF16),
        ],
        scratch_shapes=[
            pltpu.VMEM((rows, PROJ_W), F32),
            pltpu.VMEM((rows, MIX_W), BF16),
        ],
        compiler_params=pltpu.CompilerParams(
            dimension_semantics=("arbitrary",), vmem_limit_bytes=VMEM_LIMIT),
        name="sample_mixer",
    )(sinks, x.reshape(rows, d), tab, dmask, qdec, kdec, gt, cache_k, cache_v, state, w_in, gnw, w_out, ln1w, ln1b)


def _mlp_kernel(xp_ref, xs_ref, w_up_ref, w_down_ref, lnw_ref, lnb_ref, yp_ref, ys_ref, h_scr, *, prompt_steps):
    i = pl.program_id(0)

    def block(x_ref, y_ref):
        for r0 in range(0, x_ref.shape[0], MLP_UP_ROWS):
            n = min(MLP_UP_ROWS, x_ref.shape[0] - r0)
            h = _dot(x_ref[r0:r0 + n, :].astype(BF16), w_up_ref[...])
            h_scr[0:n, :] = jnp.square(jnp.maximum(h, 0.0)).astype(BF16)
            for r in range(0, n, MLP_LN_ROWS):
                rows = slice(r0 + r, r0 + r + MLP_LN_ROWS)
                z = ALPHA * x_ref[rows, :] + _dot(h_scr[r:r + MLP_LN_ROWS, :], w_down_ref[...])
                y_ref[rows, :] = _layer_norm(z, lnw_ref[...], lnb_ref[...])

    @pl.when(i < prompt_steps)
    def _():
        block(xp_ref, yp_ref)

    @pl.when(i == prompt_steps)
    def _():
        block(xs_ref, ys_ref)


def _mlp(xp, xs, w_up, w_down, lnw, lnb):
    rows, d = xp.shape
    rows_s = xs.shape[0]
    steps = rows // MLP_TILE
    const2 = lambda i: (0, 0)
    prompt_tile = lambda i: (jnp.minimum(i, steps - 1), 0)
    resident = dict(pipeline_mode=pl.Buffered(1))
    return pl.pallas_call(
        functools.partial(_mlp_kernel, prompt_steps=steps),
        grid=(steps + 1,),
        in_specs=[
            pl.BlockSpec((MLP_TILE, d), prompt_tile),
            pl.BlockSpec((rows_s, d), const2, **resident),
            pl.BlockSpec((d, D_FF), const2, **resident),
            pl.BlockSpec((D_FF, d), const2, **resident),
            pl.BlockSpec((1, d), const2),
            pl.BlockSpec((1, d), const2),
        ],
        out_specs=[
            pl.BlockSpec((MLP_TILE, d), prompt_tile),
            pl.BlockSpec((rows_s, d), const2),
        ],
        out_shape=[
            jax.ShapeDtypeStruct((rows, d), F32),
            jax.ShapeDtypeStruct((rows_s, d), F32),
        ],
        scratch_shapes=[pltpu.VMEM((MLP_UP_ROWS, D_FF), BF16)],
        compiler_params=pltpu.CompilerParams(
            dimension_semantics=("arbitrary",), vmem_limit_bytes=VMEM_LIMIT),
        name="mlp",
    )(xp, xs, w_up, w_down, lnw, lnb)


def _rope_tables(pos):
    posf = np.asarray(pos, np.float64)[:, None]
    half_r = RET_DK // 2
    ang = posf * (RET_THETA ** (-np.arange(half_r, dtype=np.float64) / half_r))[None, :]
    c, s = np.cos(ang), np.sin(ang)
    rc = np.concatenate([c, c], axis=-1)
    rs = np.concatenate([-s, s], axis=-1)
    k_scale = RET_DK ** -0.5
    half_s = SWA_ROT_DIM // 2
    ang = posf * (SWA_THETA ** (-np.arange(half_s, dtype=np.float64) / half_s))[None, :]
    c, s = np.cos(ang), np.sin(ang)
    t = posf.shape[0]
    rest = SWA_HD - SWA_ROT_DIM
    z = np.zeros((t, half_s))
    sc = np.tile(np.concatenate([c, c, np.ones((t, rest))], axis=-1), (1, LANES // SWA_HD))
    s1 = np.tile(np.concatenate([z, s, np.zeros((t, rest))], axis=-1), (1, LANES // SWA_HD))
    s2 = np.tile(np.concatenate([-s, z, np.zeros((t, rest))], axis=-1), (1, LANES // SWA_HD))
    q_scale = SWA_HD ** -0.5
    tab = np.concatenate([rc, rs, rc * k_scale, rs * k_scale,
                          sc * q_scale, s1 * q_scale, s2 * q_scale, sc, s1, s2], axis=-1)
    return jnp.asarray(tab.astype(np.float32))


def _decay_tables(t, copies=1):
    lg = np.log1p(-(2.0 ** (-5.0 - np.arange(RET_HEADS, dtype=np.float64))))
    idx = np.arange(t, dtype=np.float64)
    diff = idx[:, None] - idx[None, :]
    dmask = np.where(diff >= 0, np.exp(lg[:, None, None] * np.maximum(diff, 0.0)), 0.0)
    qdec = np.repeat(np.exp(lg[None, :] * (idx[:, None] + 1.0)), RET_DV, axis=1)
    kdec = np.repeat(np.exp(lg[None, :] * (t - 1.0 - idx[:, None])), RET_DV, axis=1)
    gt = np.repeat(np.exp(lg * t)[None, :], RET_DV, axis=1)
    if copies > 1:
        dmask = np.stack([np.kron(np.eye(copies), m) for m in dmask])
        qdec, kdec = np.tile(qdec, (copies, 1)), np.tile(kdec, (copies, 1))
    return tuple(jnp.asarray(a.astype(np.float32)) for a in (dmask, qdec, kdec, gt))


def kernel(x_prompt, x_sample, cache_swa_k, cache_swa_v, state_ret, w_in, ret_gn_w, swa_sinks,
           w_out, ln1_w, ln1_b, w_up, w_down, ln2_w, ln2_b):
    assert w_in.shape[0] == DEPTH == 1
    b, t, d = x_prompt.shape
    nb, n_new, _ = x_sample.shape
    win = cache_swa_k.shape[2]

    sinks = swa_sinks[0]

    tab_p = _rope_tables(np.arange(t))
    tab_s = _rope_tables(np.tile(PAST_LEN + np.arange(n_new), SAMPLE_STREAMS))
    dec_p = _decay_tables(PROMPT_TILE)
    dec_s = _decay_tables(n_new, SAMPLE_STREAMS)

    x1_s, k_s, v_s, r_s, w_in_b, w_out_b = _sample_mixer(
        x_sample, tab_s, *dec_s, cache_swa_k[0].reshape(nb, win, KV_W), cache_swa_v[0].reshape(nb, win, KV_W),
        state_ret[0], sinks, w_in[0], ret_gn_w, w_out[0], ln1_w, ln1_b)
    x1_p, k_p, v_p, r_p, w_up_b, w_down_b = _prompt_mixer(
        x_prompt, tab_p, *dec_p, sinks, w_in_b, ret_gn_w, w_out_b, ln1_w, ln1_b, w_up[0], w_down[0])
    y_p, y_s = _mlp(x1_p.reshape(b * t, d), x1_s, w_up_b, w_down_b, ln2_w, ln2_b)

    return (y_p.reshape(b, t, d), y_s.reshape(nb, n_new, d),
            k_p.reshape(DEPTH, b, SWA_WINDOW, SWA_KV_HEADS, SWA_HD),
            v_p.reshape(DEPTH, b, SWA_WINDOW, SWA_KV_HEADS, SWA_HD),
            r_p.reshape(DEPTH, b, RET_HEADS, RET_DK, RET_DV),
            k_s.reshape(DEPTH, nb, n_new, SWA_KV_HEADS, SWA_HD),
            v_s.reshape(DEPTH, nb, n_new, SWA_KV_HEADS, SWA_HD),
            r_s.reshape(DEPTH, nb, RET_HEADS, RET_DK, RET_DV))
```

```python
import functools

import numpy as np
import jax
import jax.numpy as jnp
from jax import lax
from jax.experimental import pallas as pl
from jax.experimental.pallas import tpu as pltpu

D_MODEL = 1024
DEPTH = 1
PAST_LEN = 2048
CHUNK = 64
RET_HEADS = 4
RET_DK = 128
RET_DV = 128
RET_THETA = 10000.0
RET_W = RET_HEADS * RET_DV
SWA_HEADS = 8
SWA_KV_HEADS = 2
SWA_HD = 64
SWA_WINDOW = 128
SWA_ROT_DIM = SWA_HD // 4
SWA_THETA = 500000.0
SWA_W = SWA_HEADS * SWA_HD
KV_W = SWA_KV_HEADS * SWA_HD
MIX_W = RET_W + SWA_W
D_FF = 4 * D_MODEL
OFF_RQ, OFF_RK, OFF_RV, OFF_RG = 0, RET_W, 2 * RET_W, 3 * RET_W
OFF_SQ = 4 * RET_W
OFF_SK = OFF_SQ + SWA_W
OFF_SV = OFF_SK + KV_W
PROJ_W = OFF_SV + KV_W
ALPHA = (2.0 * DEPTH) ** 0.25
LN_EPS = 1e-5
GN_EPS = 1e-5
NEG_INF = -1e30

LANES = 128
MXU_COLS = 256
N_PAIRS = SWA_W // LANES
PAIRS_PER_KV = N_PAIRS // SWA_KV_HEADS
T_RQC, T_RQS, T_RKC, T_RKS, T_SQC, T_SQ1, T_SQ2, T_SKC, T_SK1, T_SK2 = range(10)
N_TABS = 10

PROMPT_TILE = 256
N_SLABS = PROJ_W // MXU_COLS
LN_BLOCKS = 4
SLABS_AFTER_RET_HEAD = (1, 1, 1, 0)
SLABS_AFTER_SOFTMAX_BLOCK = (1, 1, 1, 1, 1, 1, 0, 0)
SLABS_AFTER_LN_BLOCK = (1, 1, 0, 0)
MLP_CAST_STEPS = 8
SAMPLE_STREAMS = 4
MLP_TILE = 1024
MLP_UP_ROWS = 512
MLP_LN_ROWS = 128
VMEM_LIMIT = 56 * 1024 * 1024

F32 = jnp.float32
BF16 = jnp.bfloat16


def _tab(tab_ref, i):
    return tab_ref[:, i * LANES:(i + 1) * LANES]


def _dot(a, b):
    return jnp.dot(a, b, preferred_element_type=F32)


def _dot_nt(a, b):
    return lax.dot_general(a, b, (((1,), (1,)), ((), ())), preferred_element_type=F32)


def _dot_tn(a, b):
    return lax.dot_general(a, b, (((0,), (0,)), ((), ())), preferred_element_type=F32)


def _layer_norm(z, w, b):
    mu = jnp.mean(z, axis=-1, keepdims=True)
    zc = z - mu
    var = jnp.mean(zc * zc, axis=-1, keepdims=True)
    return zc * lax.rsqrt(var + LN_EPS) * w + b


def _rope_ret(x, c, s):
    return x * c + pltpu.roll(x, RET_DK // 2, axis=1) * s


def _rope_swa(x, c, s1, s2):
    half = SWA_ROT_DIM // 2
    return x * c + pltpu.roll(x, half, axis=1) * s1 + pltpu.roll(x, LANES - half, axis=1) * s2


def _retention_head(q, k, v, g, s_prev, dmask, qdec, kdec, gt, gnw):
    qb = q.astype(BF16)
    kb = k.astype(BF16)
    vb = v.astype(BF16)
    scores = _dot_nt(qb, kb) * dmask
    o = _dot(scores.astype(BF16), vb)
    o = o + _dot(qb, s_prev.astype(BF16)) * qdec
    s_new = gt * s_prev + _dot_tn((k * kdec).astype(BF16), vb)
    mu = jnp.mean(o, axis=-1, keepdims=True)
    oc = o - mu
    var = jnp.mean(oc * oc, axis=-1, keepdims=True)
    on = oc * lax.rsqrt(var + GN_EPS) * gnw
    return jax.nn.silu(g) * on, s_new


def _kv_variants(a):
    lane = lax.broadcasted_iota(jnp.int32, a.shape, 1)
    lo = lane < SWA_HD
    ar = pltpu.roll(a, SWA_HD, axis=1)
    zero = jnp.zeros_like(a)
    return (jnp.where(lo, a, zero).astype(BF16), jnp.where(lo, zero, ar).astype(BF16),
            jnp.where(lo, ar, zero).astype(BF16), jnp.where(lo, zero, a).astype(BF16))


def _pair_rows(a_e, a_o, r0, nk):
    return jnp.concatenate([a_e[r0:r0 + LANES], a_o[r0:r0 + LANES],
                            a_e[r0 + LANES:r0 + nk], a_o[r0 + LANES:r0 + nk]], axis=0)


def _swa_softmax(s, nk, sink_e, sink_o, valid):
    r = nk - LANES
    s_e, s_o, s_x = s[:, :LANES], s[:, LANES:2 * LANES], s[:, 2 * LANES:]
    if valid is not None:
        s_e = jnp.where(valid, s_e, NEG_INF)
        s_o = jnp.where(valid, s_o, NEG_INF)
    x_is_e = lax.broadcasted_iota(jnp.int32, s_x.shape, 1) < r
    rowmax = lambda a: jnp.max(a, axis=-1, keepdims=True)
    x_e = jnp.where(x_is_e, s_x, -jnp.inf)
    x_o = jnp.where(x_is_e, -jnp.inf, s_x)
    if 2 * r == LANES:
        m_e, m_o = rowmax(jnp.maximum(s_e, x_e)), rowmax(jnp.maximum(s_o, x_o))
    else:
        m_e, m_o = jnp.maximum(rowmax(s_e), rowmax(x_e)), jnp.maximum(rowmax(s_o), rowmax(x_o))
    m_e = jnp.maximum(m_e, sink_e)
    m_o = jnp.maximum(m_o, sink_o)
    e_e = jnp.exp(s_e - m_e)
    e_o = jnp.exp(s_o - m_o)
    e_x = jnp.exp(s_x - jnp.where(x_is_e, m_e, m_o))
    lane = lax.broadcasted_iota(jnp.int32, (s.shape[0], LANES), 1)
    e = jnp.concatenate([e_e, e_o, e_x], axis=1).astype(BF16)
    return e, jnp.where(lane < SWA_HD, jnp.exp(sink_e - m_e), jnp.exp(sink_o - m_o))


def _head_indicator(nk):
    row = lax.broadcasted_iota(jnp.int32, (2 * nk, LANES), 0)
    lane = lax.broadcasted_iota(jnp.int32, (2 * nk, LANES), 1)
    even_row = jnp.logical_or(row < LANES, jnp.logical_and(row >= 2 * LANES, row < LANES + nk))
    return jnp.where(even_row == (lane < SWA_HD), 1.0, 0.0).astype(BF16)


def _swa_out(e, vt, ind, sink_term):
    res = _dot(e, jnp.concatenate([vt, ind], axis=1))
    return res[:, :LANES] / (res[:, LANES:] + sink_term)


def _prompt_mixer_kernel(sinks_ref, x_ref, xn_ref, tab_ref, dmask_ref, qdec_ref, kdec_ref, gt_ref, w_in_ref,
                         gnw_ref, w_out_ref, ln1w_ref, ln1b_ref, w_up_f32_ref, w_down_f32_ref,
                         x1_ref, kout_ref, vout_ref, sout_ref, w_up_ref, w_down_ref,
                         pa_scr, pb_scr, xb_scr, s_scr, kext_scr, vext_scr, mix_scr, sw_s_scr, sw_e_scr, sw_d_scr,
                         *, steps_per_seq):
    tt = PROMPT_TILE
    g = pl.program_id(0)
    seq_step = g % steps_per_seq

    @pl.when(g < MLP_CAST_STEPS)
    def _():
        w_up_ref[...] = w_up_f32_ref[...].astype(BF16)
        w_down_ref[...] = w_down_f32_ref[...].astype(BF16)

    @pl.when(seq_step == 0)
    def _():
        s_scr[...] = jnp.zeros_like(s_scr)
        kext_scr[0:SWA_WINDOW, :] = jnp.zeros((SWA_WINDOW, KV_W), F32)
        vext_scr[0:SWA_WINDOW, :] = jnp.zeros((SWA_WINDOW, KV_W), F32)

    @pl.when(g == 0)
    def _():
        pa_scr[...] = _dot(x_ref[0, 0:tt, :].astype(BF16), w_in_ref[...])

    def project_next_slabs(p_nxt, slabs, count):
        for _ in range(count):
            c = next(slabs)
            cols = slice(c * MXU_COLS, (c + 1) * MXU_COLS)
            p_nxt[:, cols] = _dot(xb_scr[...], w_in_ref[:, cols])

    def tile(r0x, p_cur, p_nxt, load_next_x, seq_start):
        rows_x = slice(r0x, r0x + tt)
        tab = lambda i: tab_ref[rows_x, i * LANES:(i + 1) * LANES]
        xb_scr[...] = load_next_x().astype(BF16)
        slabs = iter(range(N_SLABS))

        for h in range(RET_HEADS):
            sl = slice(h * LANES, (h + 1) * LANES)
            q = _rope_ret(p_cur[:, OFF_RQ + h * LANES:OFF_RQ + (h + 1) * LANES], tab(T_RQC), tab(T_RQS))
            k = _rope_ret(p_cur[:, OFF_RK + h * LANES:OFF_RK + (h + 1) * LANES], tab(T_RKC), tab(T_RKS))
            v = p_cur[:, OFF_RV + h * LANES:OFF_RV + (h + 1) * LANES]
            gate = p_cur[:, OFF_RG + h * LANES:OFF_RG + (h + 1) * LANES]
            out, s_new = _retention_head(q, k, v, gate, s_scr[h], dmask_ref[h], qdec_ref[:, sl], kdec_ref[:, sl],
                                         gt_ref[:, sl], gnw_ref[:, sl])
            s_scr[h] = s_new
            mix_scr[:, sl] = out.astype(BF16)
            project_next_slabs(p_nxt, slabs, SLABS_AFTER_RET_HEAD[h])

        k_new = _rope_swa(p_cur[:, OFF_SK:OFF_SK + KV_W], tab(T_SKC), tab(T_SK1), tab(T_SK2))
        v_new = p_cur[:, OFF_SV:OFF_SV + KV_W]
        kext_scr[SWA_WINDOW:SWA_WINDOW + tt, :] = k_new
        vext_scr[SWA_WINDOW:SWA_WINDOW + tt, :] = v_new
        k_var = _kv_variants(kext_scr[...])
        v_var = _kv_variants(vext_scr[...])
        nk = SWA_WINDOW + CHUNK
        ind = _head_indicator(nk)
        n_chunks = tt // CHUNK
        qs = [_rope_swa(p_cur[:, OFF_SQ + m * LANES:OFF_SQ + (m + 1) * LANES],
                        tab(T_SQC), tab(T_SQ1), tab(T_SQ2)).astype(BF16)
              for m in range(N_PAIRS)]
        col = lax.broadcasted_iota(jnp.int32, (PAIRS_PER_KV * CHUNK, LANES), 1)
        row = lax.broadcasted_iota(jnp.int32, (PAIRS_PER_KV * CHUNK, 1), 0)
        blocks = [(grp, i) for grp in range(SWA_KV_HEADS) for i in range(n_chunks)]
        rows_of = lambda n: slice(n * PAIRS_PER_KV * CHUNK, (n + 1) * PAIRS_PER_KV * CHUNK)
        for n, (grp, i) in enumerate(blocks):
            r0 = i * CHUNK
            q_blk = jnp.concatenate([qs[grp * PAIRS_PER_KV + a][r0:r0 + CHUNK] for a in range(PAIRS_PER_KV)], axis=0)
            sw_s_scr[rows_of(n), :] = _dot_nt(q_blk, _pair_rows(k_var[2 * grp], k_var[2 * grp + 1], r0, nk))
        for n, (grp, i) in enumerate(blocks):
            sink_e = sinks_ref[2 * grp * PAIRS_PER_KV]
            sink_o = sinks_ref[2 * grp * PAIRS_PER_KV + 1]
            for a in range(1, PAIRS_PER_KV):
                sink_e = jnp.where(row >= a * CHUNK, sinks_ref[2 * (grp * PAIRS_PER_KV + a)], sink_e)
                sink_o = jnp.where(row >= a * CHUNK, sinks_ref[2 * (grp * PAIRS_PER_KV + a) + 1], sink_o)
            valid = None
            if seq_start is not None and i < SWA_WINDOW // CHUNK:
                valid = jnp.logical_or(col >= SWA_WINDOW - i * CHUNK, jnp.logical_not(seq_start))
            e, sink_term = _swa_softmax(sw_s_scr[rows_of(n), :], nk, sink_e, sink_o, valid)
            sw_e_scr[rows_of(n), :] = e
            sw_d_scr[rows_of(n), :] = sink_term
            project_next_slabs(p_nxt, slabs, SLABS_AFTER_SOFTMAX_BLOCK[n])
        for n, (grp, i) in enumerate(blocks):
            r0 = i * CHUNK
            vt = _pair_rows(v_var[2 * grp], v_var[2 * grp + 1], r0, nk)
            o = _swa_out(sw_e_scr[rows_of(n), :], vt, ind, sw_d_scr[rows_of(n), :]).astype(BF16)
            for a in range(PAIRS_PER_KV):
                m = grp * PAIRS_PER_KV + a
                mix_scr[r0:r0 + CHUNK, RET_W + m * LANES:RET_W + (m + 1) * LANES] = o[a * CHUNK:(a + 1) * CHUNK]

        kext_scr[0:SWA_WINDOW, :] = k_new[tt - SWA_WINDOW:, :]
        vext_scr[0:SWA_WINDOW, :] = v_new[tt - SWA_WINDOW:, :]

        x1_ref[0, rows_x, :] = ALPHA * x_ref[0, rows_x, :] + _dot(mix_scr[...], w_out_ref[...])
        lb = tt // LN_BLOCKS
        for r in range(LN_BLOCKS):
            rows_ln = slice(r0x + r * lb, r0x + (r + 1) * lb)
            x1_ref[0, rows_ln, :] = _layer_norm(x1_ref[0, rows_ln, :], ln1w_ref[...], ln1b_ref[...])
            project_next_slabs(p_nxt, slabs, SLABS_AFTER_LN_BLOCK[r])
        assert next(slabs, None) is None
        return k_new, v_new

    tile(0, pa_scr, pb_scr, lambda: x_ref[0, tt:2 * tt, :], seq_step == 0)
    k_new, v_new = tile(tt, pb_scr, pa_scr, lambda: xn_ref[0], None)

    @pl.when(seq_step == steps_per_seq - 1)
    def _():
        kout_ref[0] = k_new[tt - SWA_WINDOW:, :]
        vout_ref[0] = v_new[tt - SWA_WINDOW:, :]
        sout_ref[0] = s_scr[...]


def _prompt_mixer(x, tab, dmask, qdec, kdec, gt, sinks, w_in, gnw, w_out, ln1w, ln1b, w_up, w_down):
    b, t, d = x.shape
    tt = PROMPT_TILE
    sps = t // (2 * tt)
    n_steps = b * sps
    up_rows, down_rows = w_up.shape[0] // MLP_CAST_STEPS, w_down.shape[0] // MLP_CAST_STEPS
    row_slice = lambda g: (jnp.minimum(g, MLP_CAST_STEPS - 1), 0)
    const2 = lambda g: (0, 0)
    const3 = lambda g: (0, 0, 0)

    def next_tile(g):
        gn = jnp.minimum(g + 1, n_steps - 1)
        return (gn // sps, (gn % sps) * 2, 0)

    resident = dict(pipeline_mode=pl.Buffered(1))
    return pl.pallas_call(
        functools.partial(_prompt_mixer_kernel, steps_per_seq=sps),
        grid=(n_steps,),
        in_specs=[
            pl.BlockSpec(memory_space=pltpu.SMEM),
            pl.BlockSpec((1, 2 * tt, d), lambda g: (g // sps, g % sps, 0)),
            pl.BlockSpec((1, tt, d), next_tile),
            pl.BlockSpec((2 * tt, N_TABS * LANES), lambda g: (g % sps, 0)),
            pl.BlockSpec((RET_HEADS, tt, tt), const3, **resident),
            pl.BlockSpec((tt, RET_W), const2, **resident),
            pl.BlockSpec((tt, RET_W), const2, **resident),
            pl.BlockSpec((1, RET_W), const2),
            pl.BlockSpec((d, PROJ_W), const2, **resident),
            pl.BlockSpec((1, RET_W), const2),
            pl.BlockSpec((MIX_W, d), const2, **resident),
            pl.BlockSpec((1, d), const2),
            pl.BlockSpec((1, d), const2),
            pl.BlockSpec((up_rows, w_up.shape[1]), row_slice),
            pl.BlockSpec((down_rows, w_down.shape[1]), row_slice),
        ],
        out_specs=[
            pl.BlockSpec((1, 2 * tt, d), lambda g: (g // sps, g % sps, 0)),
            pl.BlockSpec((1, SWA_WINDOW, KV_W), lambda g: (g // sps, 0, 0)),
            pl.BlockSpec((1, SWA_WINDOW, KV_W), lambda g: (g // sps, 0, 0)),
            pl.BlockSpec((1, RET_HEADS, RET_DK, RET_DV), lambda g: (g // sps, 0, 0, 0)),
            pl.BlockSpec((up_rows, w_up.shape[1]), row_slice),
            pl.BlockSpec((down_rows, w_down.shape[1]), row_slice),
        ],
        out_shape=[
            jax.ShapeDtypeStruct((b, t, d), F32),
            jax.ShapeDtypeStruct((b, SWA_WINDOW, KV_W), F32),
            jax.ShapeDtypeStruct((b, SWA_WINDOW, KV_W), F32),
            jax.ShapeDtypeStruct((b, RET_HEADS, RET_DK, RET_DV), F32),
            jax.ShapeDtypeStruct(w_up.shape, BF16),
            jax.ShapeDtypeStruct(w_down.shape, BF16),
        ],
        scratch_shapes=[
            pltpu.VMEM((tt, PROJ_W), F32),
            pltpu.VMEM((tt, PROJ_W), F32),
            pltpu.VMEM((tt, d), BF16),
            pltpu.VMEM((RET_HEADS, RET_DK, RET_DV), F32),
            pltpu.VMEM((SWA_WINDOW + tt, KV_W), F32),
            pltpu.VMEM((SWA_WINDOW + tt, KV_W), F32),
            pltpu.VMEM((tt, MIX_W), BF16),
            pltpu.VMEM((N_PAIRS * tt, 2 * (SWA_WINDOW + CHUNK)), F32),
            pltpu.VMEM((N_PAIRS * tt, 2 * (SWA_WINDOW + CHUNK)), BF16),
            pltpu.VMEM((N_PAIRS * tt, LANES), F32),
        ],
        compiler_params=pltpu.CompilerParams(
            dimension_semantics=("arbitrary",), vmem_limit_bytes=VMEM_LIMIT),
        name="prompt_mixer",
    )(sinks, x, x, tab, dmask, qdec, kdec, gt, w_in, gnw, w_out, ln1w, ln1b, w_up, w_down)


def _sample_mixer_kernel(sinks_ref, x_ref, tab_ref, dmask_ref, qdec_ref, kdec_ref, gt_ref, ck_ref, cv_ref,
                         state_ref, w_in_f32_ref, gnw_ref, w_out_f32_ref, ln1w_ref, ln1b_ref,
                         x1_ref, kout_ref, vout_ref, sout_ref, w_in_ref, w_out_ref,
                         p_scr, mix_scr, *, n_new):
    s_idx = pl.program_id(0)
    last_s = pl.num_programs(0) - 1
    ns = SAMPLE_STREAMS
    t = n_new
    r_step = ns * t

    @pl.when(s_idx == 0)
    def _():
        w_in_ref[...] = w_in_f32_ref[...].astype(BF16)
        w_out_ref[...] = w_out_f32_ref[...].astype(BF16)
        p_scr[...] = _dot(x_ref[...].astype(BF16), w_in_ref[...])

    rows = pl.ds(pl.multiple_of(s_idx * r_step, r_step), r_step)
    of_stream = lambda a, si: a[si * t:(si + 1) * t]
    heads = range(RET_HEADS)
    streams = range(ns)
    groups = range(SWA_KV_HEADS)

    head_cols = lambda off, h: slice(off + h * LANES, off + (h + 1) * LANES)
    lanes_of = lambda h: slice(h * LANES, (h + 1) * LANES)
    q_r = [_rope_ret(p_scr[rows, head_cols(OFF_RQ, h)], _tab(tab_ref, T_RQC), _tab(tab_ref, T_RQS)) for h in heads]
    k_r = [_rope_ret(p_scr[rows, head_cols(OFF_RK, h)], _tab(tab_ref, T_RKC), _tab(tab_ref, T_RKS)) for h in heads]
    qb = [q.astype(BF16) for q in q_r]
    kb = [k.astype(BF16) for k in k_r]
    kd = [(k_r[h] * kdec_ref[:, lanes_of(h)]).astype(BF16) for h in heads]
    vb = [p_scr[rows, head_cols(OFF_RV, h)].astype(BF16) for h in heads]
    k_new = _rope_swa(p_scr[rows, OFF_SK:OFF_SK + KV_W], _tab(tab_ref, T_SKC), _tab(tab_ref, T_SK1), _tab(tab_ref, T_SK2))
    v_new = p_scr[rows, OFF_SV:OFF_SV + KV_W]
    kout_ref[...] = k_new
    vout_ref[...] = v_new
    qs = [_rope_swa(p_scr[rows, head_cols(OFF_SQ, m)],
                    _tab(tab_ref, T_SQC), _tab(tab_ref, T_SQ1), _tab(tab_ref, T_SQ2)).astype(BF16)
          for m in range(N_PAIRS)]
    nk = ck_ref.shape[1] + t
    k_var = [_kv_variants(jnp.concatenate([ck_ref[si], of_stream(k_new, si)], axis=0)) for si in streams]
    v_var = [_kv_variants(jnp.concatenate([cv_ref[si], of_stream(v_new, si)], axis=0)) for si in streams]

    scores = [_dot_nt(qb[h], kb[h]) for h in heads]
    inter = [[_dot(of_stream(qb[h], si), state_ref[si, h].astype(BF16)) for si in streams] for h in heads]
    ktv = [[_dot_tn(of_stream(kd[h], si), of_stream(vb[h], si)) for si in streams] for h in heads]
    sw_s = [[_dot_nt(jnp.concatenate([of_stream(qs[grp * PAIRS_PER_KV + a], si) for a in range(PAIRS_PER_KV)], axis=0),
                     _pair_rows(k_var[si][2 * grp], k_var[si][2 * grp + 1], 0, nk))
             for grp in groups] for si in streams]

    scores = [(scores[h] * dmask_ref[h]).astype(BF16) for h in heads]
    for h in heads:
        for si in streams:
            sout_ref[si, h] = gt_ref[:, lanes_of(h)] * state_ref[si, h] + ktv[h][si]
    row = lax.broadcasted_iota(jnp.int32, (PAIRS_PER_KV * t, 1), 0)
    sw_e = []
    for si in streams:
        per_group = []
        for grp in groups:
            sink_e = sinks_ref[2 * grp * PAIRS_PER_KV]
            sink_o = sinks_ref[2 * grp * PAIRS_PER_KV + 1]
            for a in range(1, PAIRS_PER_KV):
                sink_e = jnp.where(row >= a * t, sinks_ref[2 * (grp * PAIRS_PER_KV + a)], sink_e)
                sink_o = jnp.where(row >= a * t, sinks_ref[2 * (grp * PAIRS_PER_KV + a) + 1], sink_o)
            per_group.append(_swa_softmax(sw_s[si][grp], nk, sink_e, sink_o, None))
        sw_e.append(per_group)

    intra = [_dot(scores[h], vb[h]) for h in heads]
    ind = _head_indicator(nk)
    sw_o = [[_swa_out(sw_e[si][grp][0], _pair_rows(v_var[si][2 * grp], v_var[si][2 * grp + 1], 0, nk), ind,
                      sw_e[si][grp][1]).astype(BF16)
             for grp in groups] for si in streams]

    for h in heads:
        o = intra[h] + jnp.concatenate(inter[h], axis=0) * qdec_ref[:, lanes_of(h)]
        mu = jnp.mean(o, axis=-1, keepdims=True)
        oc = o - mu
        var = jnp.mean(oc * oc, axis=-1, keepdims=True)
        on = oc * lax.rsqrt(var + GN_EPS) * gnw_ref[:, lanes_of(h)]
        mix_scr[rows, lanes_of(h)] = (jax.nn.silu(p_scr[rows, head_cols(OFF_RG, h)]) * on).astype(BF16)
    for m in range(N_PAIRS):
        grp, a = divmod(m, PAIRS_PER_KV)
        o = jnp.concatenate([sw_o[si][grp][a * t:(a + 1) * t] for si in streams], axis=0)
        mix_scr[rows, RET_W + m * LANES:RET_W + (m + 1) * LANES] = o

    @pl.when(s_idx == last_s)
    def _():
        z = ALPHA * x_ref[...] + _dot(mix_scr[...], w_out_ref[...])
        x1_ref[...] = _layer_norm(z, ln1w_ref[...], ln1b_ref[...])


def _sample_mixer(x, tab, dmask, qdec, kdec, gt, cache_k, cache_v, state, sinks, w_in, gnw, w_out, ln1w, ln1b):
    nb, n_new, d = x.shape
    rows = nb * n_new
    win = cache_k.shape[1]
    ns = SAMPLE_STREAMS
    r_step = ns * n_new
    const2 = lambda si: (0, 0)
    const3 = lambda si: (0, 0, 0)
    return pl.pallas_call(
        functools.partial(_sample_mixer_kernel, n_new=n_new),
        grid=(nb // ns,),
        in_specs=[
            pl.BlockSpec(memory_space=pltpu.SMEM),
            pl.BlockSpec((rows, d), const2),
            pl.BlockSpec((r_step, N_TABS * LANES), const2),
            pl.BlockSpec((RET_HEADS, r_step, r_step), const3),
            pl.BlockSpec((r_step, RET_W), const2),
            pl.BlockSpec((r_step, RET_W), const2),
            pl.BlockSpec((1, RET_W), const2),
            pl.BlockSpec((ns, win, KV_W), lambda si: (si, 0, 0)),
            pl.BlockSpec((ns, win, KV_W), lambda si: (si, 0, 0)),
            pl.BlockSpec((ns, RET_HEADS, RET_DK, RET_DV), lambda si: (si, 0, 0, 0)),
            pl.BlockSpec((d, PROJ_W), const2, pipeline_mode=pl.Buffered(1)),
            pl.BlockSpec((1, RET_W), const2),
            pl.BlockSpec((MIX_W, d), const2, pipeline_mode=pl.Buffered(1)),
            pl.BlockSpec((1, d), const2),
            pl.BlockSpec((1, d), const2),
        ],
        out_specs=[
            pl.BlockSpec((rows, d), const2),
            pl.BlockSpec((r_step, KV_W), lambda si: (si, 0)),
            pl.BlockSpec((r_step, KV_W), lambda si: (si, 0)),
            pl.BlockSpec((ns, RET_HEADS, RET_DK, RET_DV), lambda si: (si, 0, 0, 0)),
            pl.BlockSpec((d, PROJ_W), const2),
            pl.BlockSpec((MIX_W, d), const2),
        ],
        out_shape=[
            jax.ShapeDtypeStruct((rows, d), F32),
            jax.ShapeDtypeStruct((rows, KV_W), F32),
            jax.ShapeDtypeStruct((rows, KV_W), F32),
            jax.ShapeDtypeStruct((nb, RET_HEADS, RET_DK, RET_DV), F32),
            jax.ShapeDtypeStruct((d, PROJ_W), BF16),
            jax.ShapeDtypeStruct((MIX_W, d), BF16),
        ],
        scratch_shapes=[
            pltpu.VMEM((rows, PROJ_W), F32),
            pltpu.VMEM((rows, MIX_W), BF16),
        ],
        compiler_params=pltpu.CompilerParams(
            dimension_semantics=("arbitrary",), vmem_limit_bytes=VMEM_LIMIT),
        name="sample_mixer",
    )(sinks, x.reshape(rows, d), tab, dmask, qdec, kdec, gt, cache_k, cache_v, state, w_in, gnw, w_out, ln1w, ln1b)


def _mlp_kernel(xp_ref, xs_ref, w_up_ref, w_down_ref, lnw_ref, lnb_ref, yp_ref, ys_ref, h_scr, *, prompt_steps):
    i = pl.program_id(0)

    def block(x_ref, y_ref):
        for r0 in range(0, x_ref.shape[0], MLP_UP_ROWS):
            n = min(MLP_UP_ROWS, x_ref.shape[0] - r0)
            h = _dot(x_ref[r0:r0 + n, :].astype(BF16), w_up_ref[...])
            h_scr[0:n, :] = jnp.square(jnp.maximum(h, 0.0)).astype(BF16)
            for r in range(0, n, MLP_LN_ROWS):
                rows = slice(r0 + r, r0 + r + MLP_LN_ROWS)
                z = ALPHA * x_ref[rows, :] + _dot(h_scr[r:r + MLP_LN_ROWS, :], w_down_ref[...])
                y_ref[rows, :] = _layer_norm(z, lnw_ref[...], lnb_ref[...])

    @pl.when(i < prompt_steps)
    def _():
        block(xp_ref, yp_ref)

    @pl.when(i == prompt_steps)
    def _():
        block(xs_ref, ys_ref)


def _mlp(xp, xs, w_up, w_down, lnw, lnb):
    rows, d = xp.shape
    rows_s = xs.shape[0]
    steps = rows // MLP_TILE
    const2 = lambda i: (0, 0)
    prompt_tile = lambda i: (jnp.minimum(i, steps - 1), 0)
    resident = dict(pipeline_mode=pl.Buffered(1))
    return pl.pallas_call(
        functools.partial(_mlp_kernel, prompt_steps=steps),
        grid=(steps + 1,),
        in_specs=[
            pl.BlockSpec((MLP_TILE, d), prompt_tile),
            pl.BlockSpec((rows_s, d), const2, **resident),
            pl.BlockSpec((d, D_FF), const2, **resident),
            pl.BlockSpec((D_FF, d), const2, **resident),
            pl.BlockSpec((1, d), const2),
            pl.BlockSpec((1, d), const2),
        ],
        out_specs=[
            pl.BlockSpec((MLP_TILE, d), prompt_tile),
            pl.BlockSpec((rows_s, d), const2),
        ],
        out_shape=[
            jax.ShapeDtypeStruct((rows, d), F32),
            jax.ShapeDtypeStruct((rows_s, d), F32),
        ],
        scratch_shapes=[pltpu.VMEM((MLP_UP_ROWS, D_FF), BF16)],
        compiler_params=pltpu.CompilerParams(
            dimension_semantics=("arbitrary",), vmem_limit_bytes=VMEM_LIMIT),
        name="mlp",
    )(xp, xs, w_up, w_down, lnw, lnb)


def _rope_tables(pos):
    posf = np.asarray(pos, np.float64)[:, None]
    half_r = RET_DK // 2
    ang = posf * (RET_THETA ** (-np.arange(half_r, dtype=np.float64) / half_r))[None, :]
    c, s = np.cos(ang), np.sin(ang)
    rc = np.concatenate([c, c], axis=-1)
    rs = np.concatenate([-s, s], axis=-1)
    k_scale = RET_DK ** -0.5
    half_s = SWA_ROT_DIM // 2
    ang = posf * (SWA_THETA ** (-np.arange(half_s, dtype=np.float64) / half_s))[None, :]
    c, s = np.cos(ang), np.sin(ang)
    t = posf.shape[0]
    rest = SWA_HD - SWA_ROT_DIM
    z = np.zeros((t, half_s))
    sc = np.tile(np.concatenate([c, c, np.ones((t, rest))], axis=-1), (1, LANES // SWA_HD))
    s1 = np.tile(np.concatenate([z, s, np.zeros((t, rest))], axis=-1), (1, LANES // SWA_HD))
    s2 = np.tile(np.concatenate([-s, z, np.zeros((t, rest))], axis=-1), (1, LANES // SWA_HD))
    q_scale = SWA_HD ** -0.5
    tab = np.concatenate([rc, rs, rc * k_scale, rs * k_scale,
                          sc * q_scale, s1 * q_scale, s2 * q_scale, sc, s1, s2], axis=-1)
    return jnp.asarray(tab.astype(np.float32))


def _decay_tables(t, copies=1):
    lg = np.log1p(-(2.0 ** (-5.0 - np.arange(RET_HEADS, dtype=np.float64))))
    idx = np.arange(t, dtype=np.float64)
    diff = idx[:, None] - idx[None, :]
    dmask = np.where(diff >= 0, np.exp(lg[:, None, None] * np.maximum(diff, 0.0)), 0.0)
    qdec = np.repeat(np.exp(lg[None, :] * (idx[:, None] + 1.0)), RET_DV, axis=1)
    kdec = np.repeat(np.exp(lg[None, :] * (t - 1.0 - idx[:, None])), RET_DV, axis=1)
    gt = np.repeat(np.exp(lg * t)[None, :], RET_DV, axis=1)
    if copies > 1:
        dmask = np.stack([np.kron(np.eye(copies), m) for m in dmask])
        qdec, kdec = np.tile(qdec, (copies, 1)), np.tile(kdec, (copies, 1))
    return tuple(jnp.asarray(a.astype(np.float32)) for a in (dmask, qdec, kdec, gt))


def kernel(x_prompt, x_sample, cache_swa_k, cache_swa_v, state_ret, w_in, ret_gn_w, swa_sinks,
           w_out, ln1_w, ln1_b, w_up, w_down, ln2_w, ln2_b):
    assert w_in.shape[0] == DEPTH == 1
    b, t, d = x_prompt.shape
    nb, n_new, _ = x_sample.shape
    win = cache_swa_k.shape[2]

    sinks = swa_sinks[0]

    tab_p = _rope_tables(np.arange(t))
    tab_s = _rope_tables(np.tile(PAST_LEN + np.arange(n_new), SAMPLE_STREAMS))
    dec_p = _decay_tables(PROMPT_TILE)
    dec_s = _decay_tables(n_new, SAMPLE_STREAMS)

    x1_s, k_s, v_s, r_s, w_in_b, w_out_b = _sample_mixer(
        x_sample, tab_s, *dec_s, cache_swa_k[0].reshape(nb, win, KV_W), cache_swa_v[0].reshape(nb, win, KV_W),
        state_ret[0], sinks, w_in[0], ret_gn_w, w_out[0], ln1_w, ln1_b)
    x1_p, k_p, v_p, r_p, w_up_b, w_down_b = _prompt_mixer(
        x_prompt, tab_p, *dec_p, sinks, w_in_b, ret_gn_w, w_out_b, ln1_w, ln1_b, w_up[0], w_down[0])
    y_p, y_s = _mlp(x1_p.reshape(b * t, d), x1_s, w_up_b, w_down_b, ln2_w, ln2_b)

    return (y_p.reshape(b, t, d), y_s.reshape(nb, n_new, d),
            k_p.reshape(DEPTH, b, SWA_WINDOW, SWA_KV_HEADS, SWA_HD),
            v_p.reshape(DEPTH, b, SWA_WINDOW, SWA_KV_HEADS, SWA_HD),
            r_p.reshape(DEPTH, b, RET_HEADS, RET_DK, RET_DV),
            k_s.reshape(DEPTH, nb, n_new, SWA_KV_HEADS, SWA_HD),
            v_s.reshape(DEPTH, nb, n_new, SWA_KV_HEADS, SWA_HD),
            r_s.reshape(DEPTH, nb, RET_HEADS, RET_DK, RET_DV))
```

```python
import functools

import numpy as np
import jax
import jax.numpy as jnp
from jax import lax
from jax.experimental import pallas as pl
from jax.experimental.pallas import tpu as pltpu

D_MODEL = 1024
DEPTH = 1
PAST_LEN = 2048
CHUNK = 64
RET_HEADS = 4
RET_DK = 128
RET_DV = 128
RET_THETA = 10000.0
RET_W = RET_HEADS * RET_DV
SWA_HEADS = 8
SWA_KV_HEADS = 2
SWA_HD = 64
SWA_WINDOW = 128
SWA_ROT_DIM = SWA_HD // 4
SWA_THETA = 500000.0
SWA_W = SWA_HEADS * SWA_HD
KV_W = SWA_KV_HEADS * SWA_HD
MIX_W = RET_W + SWA_W
D_FF = 4 * D_MODEL
OFF_RQ, OFF_RK, OFF_RV, OFF_RG = 0, RET_W, 2 * RET_W, 3 * RET_W
OFF_SQ = 4 * RET_W
OFF_SK = OFF_SQ + SWA_W
OFF_SV = OFF_SK + KV_W
PROJ_W = OFF_SV + KV_W
SWA_Q_SCALE = SWA_HD ** -0.5
assert SWA_Q_SCALE == 0.125
ALPHA = (2.0 * DEPTH) ** 0.25
LN_EPS = 1e-5
GN_EPS = 1e-5
NEG_INF = -1e30

LANES = 128
MXU_COLS = 256
N_PAIRS = SWA_W // LANES
PAIRS_PER_KV = N_PAIRS // SWA_KV_HEADS
T_RC, T_RS, T_SC, T_S1, T_S2 = range(5)
N_TABS = 5

PROMPT_TILE = 256
TILES_PER_STEP = 4
N_SLABS = PROJ_W // MXU_COLS
LN_BLOCKS = 4
SLABS_AFTER_RET_HEAD = (1, 1, 1, 0)
SLABS_AFTER_SOFTMAX_BLOCK = (1, 1, 1, 1, 1, 1, 0, 0)
SLABS_AFTER_LN_BLOCK = (1, 1, 0, 0)
MLP_CAST_STEPS = 16
SAMPLE_STREAMS = 4
MLP_TILE = 1024
MLP_UP_ROWS = 512
MLP_LN_ROWS = 128
VMEM_LIMIT = 56 * 1024 * 1024

F32 = jnp.float32
BF16 = jnp.bfloat16


def _tab(tab_ref, i):
    return tab_ref[:, i * LANES:(i + 1) * LANES]


def _dot(a, b):
    return jnp.dot(a, b, preferred_element_type=F32)


def _dot_nt(a, b):
    return lax.dot_general(a, b, (((1,), (1,)), ((), ())), preferred_element_type=F32)


def _dot_tn(a, b):
    return lax.dot_general(a, b, (((0,), (0,)), ((), ())), preferred_element_type=F32)


def _layer_norm(z, w, b):
    mu = jnp.mean(z, axis=-1, keepdims=True)
    zc = z - mu
    var = jnp.mean(zc * zc, axis=-1, keepdims=True)
    return zc * lax.rsqrt(var + LN_EPS) * w + b


def _rope_ret(x, c, s):
    return x * c + pltpu.roll(x, RET_DK // 2, axis=1) * s


def _rope_swa(x, c, s1, s2):
    half = SWA_ROT_DIM // 2
    return x * c + pltpu.roll(x, half, axis=1) * s1 + pltpu.roll(x, LANES - half, axis=1) * s2


def _retention_head(q, k, v, g, s_prev, dmask, qdec, kdec, gt, gnw):
    qb = q.astype(BF16)
    kb = k.astype(BF16)
    vb = v.astype(BF16)
    scores = _dot_nt(qb, kb) * dmask
    o = _dot(scores.astype(BF16), vb)
    o = o + _dot(qb, s_prev.astype(BF16)) * qdec
    s_new = gt * s_prev + _dot_tn((k * kdec).astype(BF16), vb)
    mu = jnp.mean(o, axis=-1, keepdims=True)
    oc = o - mu
    var = jnp.mean(oc * oc, axis=-1, keepdims=True)
    on = oc * lax.rsqrt(var + GN_EPS) * gnw
    return jax.nn.silu(g) * on, s_new


def _kv_variants(a):
    lane = lax.broadcasted_iota(jnp.int32, a.shape, 1)
    lo = lane < SWA_HD
    ar = pltpu.roll(a, SWA_HD, axis=1)
    zero = jnp.zeros_like(a)
    return (jnp.where(lo, a, zero).astype(BF16), jnp.where(lo, zero, ar).astype(BF16),
            jnp.where(lo, ar, zero).astype(BF16), jnp.where(lo, zero, a).astype(BF16))


def _pair_rows(a_e, a_o, r0, nk):
    return jnp.concatenate([a_e[r0:r0 + LANES], a_o[r0:r0 + LANES],
                            a_e[r0 + LANES:r0 + nk], a_o[r0 + LANES:r0 + nk]], axis=0)


def _swa_softmax(s, nk, sink_e, sink_o, valid):
    r = nk - LANES
    s_e, s_o, s_x = s[:, :LANES], s[:, LANES:2 * LANES], s[:, 2 * LANES:]
    if valid is not None:
        s_e = jnp.where(valid, s_e, NEG_INF)
        s_o = jnp.where(valid, s_o, NEG_INF)
    x_is_e = lax.broadcasted_iota(jnp.int32, s_x.shape, 1) < r
    rowmax = lambda a: jnp.max(a, axis=-1, keepdims=True)
    x_e = jnp.where(x_is_e, s_x, -jnp.inf)
    x_o = jnp.where(x_is_e, -jnp.inf, s_x)
    if 2 * r == LANES:
        m_e, m_o = rowmax(jnp.maximum(s_e, x_e)), rowmax(jnp.maximum(s_o, x_o))
    else:
        m_e, m_o = jnp.maximum(rowmax(s_e), rowmax(x_e)), jnp.maximum(rowmax(s_o), rowmax(x_o))
    m_e = jnp.maximum(m_e, sink_e)
    m_o = jnp.maximum(m_o, sink_o)
    e_e = jnp.exp(s_e - m_e)
    e_o = jnp.exp(s_o - m_o)
    e_x = jnp.exp(s_x - jnp.where(x_is_e, m_e, m_o))
    lane = lax.broadcasted_iota(jnp.int32, (s.shape[0], LANES), 1)
    e = jnp.concatenate([e_e, e_o, e_x], axis=1).astype(BF16)
    return e, jnp.where(lane < SWA_HD, jnp.exp(sink_e - m_e), jnp.exp(sink_o - m_o))


def _head_indicator(nk):
    row = lax.broadcasted_iota(jnp.int32, (2 * nk, LANES), 0)
    lane = lax.broadcasted_iota(jnp.int32, (2 * nk, LANES), 1)
    even_row = jnp.logical_or(row < LANES, jnp.logical_and(row >= 2 * LANES, row < LANES + nk))
    return jnp.where(even_row == (lane < SWA_HD), 1.0, 0.0).astype(BF16)


def _swa_out(e, vt, ind, sink_term):
    res = _dot(e, jnp.concatenate([vt, ind], axis=1))
    return res[:, :LANES] / (res[:, LANES:] + sink_term)


def _prompt_mixer_kernel(sinks_ref, x_ref, xn_ref, tab_ref, dmask_ref, qdec_ref, kdec_ref, gt_ref, w_in_ref,
                         gnw_ref, w_out_ref, ln1w_ref, ln1b_ref, w_up_f32_ref, w_down_f32_ref,
                         x1_ref, kout_ref, vout_ref, sout_ref, w_up_ref, w_down_ref,
                         pa_scr, pb_scr, xb_scr, s_scr, kext_scr, vext_scr, mix_scr, sw_s_scr, sw_e_scr, sw_d_scr,
                         *, steps_per_seq):
    tt = PROMPT_TILE
    g = pl.program_id(0)
    seq_step = g % steps_per_seq

    @pl.when(g < MLP_CAST_STEPS)
    def _():
        w_up_ref[...] = w_up_f32_ref[...].astype(BF16)
        w_down_ref[...] = w_down_f32_ref[...].astype(BF16)

    @pl.when(seq_step == 0)
    def _():
        s_scr[...] = jnp.zeros_like(s_scr)
        kext_scr[0:SWA_WINDOW, :] = jnp.zeros((SWA_WINDOW, KV_W), F32)
        vext_scr[0:SWA_WINDOW, :] = jnp.zeros((SWA_WINDOW, KV_W), F32)

    @pl.when(g == 0)
    def _():
        pa_scr[...] = _dot(x_ref[0, 0:tt, :].astype(BF16), w_in_ref[...])

    def project_next_slabs(p_nxt, slabs, count):
        for _ in range(count):
            c = next(slabs)
            cols = slice(c * MXU_COLS, (c + 1) * MXU_COLS)
            p_nxt[:, cols] = _dot(xb_scr[...], w_in_ref[:, cols])

    def tile(r0x, p_cur, p_nxt, load_next_x, seq_start):
        rows_x = slice(r0x, r0x + tt)
        tab = lambda i: tab_ref[rows_x, i * LANES:(i + 1) * LANES]
        xb_scr[...] = load_next_x().astype(BF16)
        slabs = iter(range(N_SLABS))

        for h in range(RET_HEADS):
            sl = slice(h * LANES, (h + 1) * LANES)
            q = _rope_ret(p_cur[:, OFF_RQ + h * LANES:OFF_RQ + (h + 1) * LANES], tab(T_RC), tab(T_RS))
            k = _rope_ret(p_cur[:, OFF_RK + h * LANES:OFF_RK + (h + 1) * LANES], tab(T_RC), tab(T_RS))
            v = p_cur[:, OFF_RV + h * LANES:OFF_RV + (h + 1) * LANES]
            gate = p_cur[:, OFF_RG + h * LANES:OFF_RG + (h + 1) * LANES]
            out, s_new = _retention_head(q, k, v, gate, s_scr[h], dmask_ref[h], qdec_ref[:, sl], kdec_ref[:, sl],
                                         gt_ref[:, sl], gnw_ref[:, sl])
            s_scr[h] = s_new
            mix_scr[:, sl] = out.astype(BF16)
            project_next_slabs(p_nxt, slabs, SLABS_AFTER_RET_HEAD[h])

        k_new = _rope_swa(p_cur[:, OFF_SK:OFF_SK + KV_W], tab(T_SC), tab(T_S1), tab(T_S2))
        v_new = p_cur[:, OFF_SV:OFF_SV + KV_W]
        kext_scr[SWA_WINDOW:SWA_WINDOW + tt, :] = k_new
        vext_scr[SWA_WINDOW:SWA_WINDOW + tt, :] = v_new
        k_var = _kv_variants(kext_scr[...])
        v_var = _kv_variants(vext_scr[...])
        nk = SWA_WINDOW + CHUNK
        ind = _head_indicator(nk)
        n_chunks = tt // CHUNK
        qs = [_rope_swa(p_cur[:, OFF_SQ + m * LANES:OFF_SQ + (m + 1) * LANES],
                        tab(T_SC), tab(T_S1), tab(T_S2)).astype(BF16)
              for m in range(N_PAIRS)]
        col = lax.broadcasted_iota(jnp.int32, (PAIRS_PER_KV * CHUNK, LANES), 1)
        row = lax.broadcasted_iota(jnp.int32, (PAIRS_PER_KV * CHUNK, 1), 0)
        blocks = [(grp, i) for grp in range(SWA_KV_HEADS) for i in range(n_chunks)]
        rows_of = lambda n: slice(n * PAIRS_PER_KV * CHUNK, (n + 1) * PAIRS_PER_KV * CHUNK)
        for n, (grp, i) in enumerate(blocks):
            r0 = i * CHUNK
            q_blk = jnp.concatenate([qs[grp * PAIRS_PER_KV + a][r0:r0 + CHUNK] for a in range(PAIRS_PER_KV)], axis=0)
            sw_s_scr[rows_of(n), :] = _dot_nt(q_blk, _pair_rows(k_var[2 * grp], k_var[2 * grp + 1], r0, nk))
        for n, (grp, i) in enumerate(blocks):
            sink_e = sinks_ref[2 * grp * PAIRS_PER_KV]
            sink_o = sinks_ref[2 * grp * PAIRS_PER_KV + 1]
            for a in range(1, PAIRS_PER_KV):
                sink_e = jnp.where(row >= a * CHUNK, sinks_ref[2 * (grp * PAIRS_PER_KV + a)], sink_e)
                sink_o = jnp.where(row >= a * CHUNK, sinks_ref[2 * (grp * PAIRS_PER_KV + a) + 1], sink_o)
            valid = None
            if seq_start is not None and i < SWA_WINDOW // CHUNK:
                valid = jnp.logical_or(col >= SWA_WINDOW - i * CHUNK, jnp.logical_not(seq_start))
            e, sink_term = _swa_softmax(sw_s_scr[rows_of(n), :], nk, sink_e, sink_o, valid)
            sw_e_scr[rows_of(n), :] = e
            sw_d_scr[rows_of(n), :] = sink_term
            project_next_slabs(p_nxt, slabs, SLABS_AFTER_SOFTMAX_BLOCK[n])
        for n, (grp, i) in enumerate(blocks):
            r0 = i * CHUNK
            vt = _pair_rows(v_var[2 * grp], v_var[2 * grp + 1], r0, nk)
            o = _swa_out(sw_e_scr[rows_of(n), :], vt, ind, sw_d_scr[rows_of(n), :]).astype(BF16)
            for a in range(PAIRS_PER_KV):
                m = grp * PAIRS_PER_KV + a
                mix_scr[r0:r0 + CHUNK, RET_W + m * LANES:RET_W + (m + 1) * LANES] = o[a * CHUNK:(a + 1) * CHUNK]

        kext_scr[0:SWA_WINDOW, :] = k_new[tt - SWA_WINDOW:, :]
        vext_scr[0:SWA_WINDOW, :] = v_new[tt - SWA_WINDOW:, :]

        x1_ref[0, rows_x, :] = ALPHA * x_ref[0, rows_x, :] + _dot(mix_scr[...], w_out_ref[...])
        lb = tt // LN_BLOCKS
        for r in range(LN_BLOCKS):
            rows_ln = slice(r0x + r * lb, r0x + (r + 1) * lb)
            x1_ref[0, rows_ln, :] = _layer_norm(x1_ref[0, rows_ln, :], ln1w_ref[...], ln1b_ref[...])
            project_next_slabs(p_nxt, slabs, SLABS_AFTER_LN_BLOCK[r])
        assert next(slabs, None) is None
        return k_new, v_new

    for i in range(TILES_PER_STEP):
        p_cur, p_nxt = (pa_scr, pb_scr) if i % 2 == 0 else (pb_scr, pa_scr)
        if i + 1 < TILES_PER_STEP:
            load_next_x = functools.partial(lambda r0: x_ref[0, r0:r0 + tt, :], (i + 1) * tt)
        else:
            load_next_x = lambda: xn_ref[0]
        k_new, v_new = tile(i * tt, p_cur, p_nxt, load_next_x, seq_step == 0 if i == 0 else None)

    @pl.when(seq_step == steps_per_seq - 1)
    def _():
        kout_ref[0] = k_new[tt - SWA_WINDOW:, :]
        vout_ref[0] = v_new[tt - SWA_WINDOW:, :]
        sout_ref[0] = s_scr[...]


def _prompt_mixer(x, tab, dmask, qdec, kdec, gt, sinks, w_in, gnw, w_out, ln1w, ln1b, w_up, w_down):
    b, t, d = x.shape
    tt = PROMPT_TILE
    rows_step = TILES_PER_STEP * tt
    sps = t // rows_step
    n_steps = b * sps
    up_rows, down_rows = w_up.shape[0] // MLP_CAST_STEPS, w_down.shape[0] // MLP_CAST_STEPS
    row_slice = lambda g: (jnp.minimum(g, MLP_CAST_STEPS - 1), 0)
    const2 = lambda g: (0, 0)
    const3 = lambda g: (0, 0, 0)

    def next_tile(g):
        gn = jnp.minimum(g + 1, n_steps - 1)
        return (gn // sps, (gn % sps) * TILES_PER_STEP, 0)

    resident = dict(pipeline_mode=pl.Buffered(1))
    return pl.pallas_call(
        functools.partial(_prompt_mixer_kernel, steps_per_seq=sps),
        grid=(n_steps,),
        in_specs=[
            pl.BlockSpec(memory_space=pltpu.SMEM),
            pl.BlockSpec((1, rows_step, d), lambda g: (g // sps, g % sps, 0)),
            pl.BlockSpec((1, tt, d), next_tile),
            pl.BlockSpec((rows_step, N_TABS * LANES), lambda g: (g % sps, 0)),
            pl.BlockSpec((RET_HEADS, tt, tt), const3, **resident),
            pl.BlockSpec((tt, RET_W), const2, **resident),
            pl.BlockSpec((tt, RET_W), const2, **resident),
            pl.BlockSpec((1, RET_W), const2),
            pl.BlockSpec((d, PROJ_W), const2, **resident),
            pl.BlockSpec((1, RET_W), const2),
            pl.BlockSpec((MIX_W, d), const2, **resident),
            pl.BlockSpec((1, d), const2),
            pl.BlockSpec((1, d), const2),
            pl.BlockSpec((up_rows, w_up.shape[1]), row_slice),
            pl.BlockSpec((down_rows, w_down.shape[1]), row_slice),
        ],
        out_specs=[
            pl.BlockSpec((1, rows_step, d), lambda g: (g // sps, g % sps, 0)),
            pl.BlockSpec((1, SWA_WINDOW, KV_W), lambda g: (g // sps, 0, 0)),
            pl.BlockSpec((1, SWA_WINDOW, KV_W), lambda g: (g // sps, 0, 0)),
            pl.BlockSpec((1, RET_HEADS, RET_DK, RET_DV), lambda g: (g // sps, 0, 0, 0)),
            pl.BlockSpec((up_rows, w_up.shape[1]), row_slice),
            pl.BlockSpec((down_rows, w_down.shape[1]), row_slice),
        ],
        out_shape=[
            jax.ShapeDtypeStruct((b, t, d), F32),
            jax.ShapeDtypeStruct((b, SWA_WINDOW, KV_W), F32),
            jax.ShapeDtypeStruct((b, SWA_WINDOW, KV_W), F32),
            jax.ShapeDtypeStruct((b, RET_HEADS, RET_DK, RET_DV), F32),
            jax.ShapeDtypeStruct(w_up.shape, BF16),
            jax.ShapeDtypeStruct(w_down.shape, BF16),
        ],
        scratch_shapes=[
            pltpu.VMEM((tt, PROJ_W), F32),
            pltpu.VMEM((tt, PROJ_W), F32),
            pltpu.VMEM((tt, d), BF16),
            pltpu.VMEM((RET_HEADS, RET_DK, RET_DV), F32),
            pltpu.VMEM((SWA_WINDOW + tt, KV_W), F32),
            pltpu.VMEM((SWA_WINDOW + tt, KV_W), F32),
            pltpu.VMEM((tt, MIX_W), BF16),
            pltpu.VMEM((N_PAIRS * tt, 2 * (SWA_WINDOW + CHUNK)), F32),
            pltpu.VMEM((N_PAIRS * tt, 2 * (SWA_WINDOW + CHUNK)), BF16),
            pltpu.VMEM((N_PAIRS * tt, LANES), F32),
        ],
        compiler_params=pltpu.CompilerParams(
            dimension_semantics=("arbitrary",), vmem_limit_bytes=VMEM_LIMIT),
        name="prompt_mixer",
    )(sinks, x, x, tab, dmask, qdec, kdec, gt, w_in, gnw, w_out, ln1w, ln1b, w_up, w_down)


def _sample_mixer_kernel(sinks_ref, x_ref, tab_ref, dmask_ref, qdec_ref, kdec_ref, gt_ref, ck_ref, cv_ref,
                         state_ref, w_in_f32_ref, gnw_ref, w_out_f32_ref, ln1w_ref, ln1b_ref,
                         x1_ref, kout_ref, vout_ref, sout_ref, w_in_ref, w_out_ref,
                         p_scr, mix_scr, *, n_new):
    s_idx = pl.program_id(0)
    last_s = pl.num_programs(0) - 1
    ns = SAMPLE_STREAMS
    t = n_new
    r_step = ns * t

    @pl.when(s_idx == 0)
    def _():
        w_in_ref[...] = w_in_f32_ref[...].astype(BF16)
        w_in_ref[:, OFF_SQ:OFF_SK] = (w_in_f32_ref[:, OFF_SQ:OFF_SK] * SWA_Q_SCALE).astype(BF16)
        w_out_ref[...] = w_out_f32_ref[...].astype(BF16)
        p_scr[...] = _dot(x_ref[...].astype(BF16), w_in_ref[...])

    rows = pl.ds(pl.multiple_of(s_idx * r_step, r_step), r_step)
    of_stream = lambda a, si: a[si * t:(si + 1) * t]
    heads = range(RET_HEADS)
    streams = range(ns)
    groups = range(SWA_KV_HEADS)

    head_cols = lambda off, h: slice(off + h * LANES, off + (h + 1) * LANES)
    lanes_of = lambda h: slice(h * LANES, (h + 1) * LANES)
    q_r = [_rope_ret(p_scr[rows, head_cols(OFF_RQ, h)], _tab(tab_ref, T_RC), _tab(tab_ref, T_RS)) for h in heads]
    k_r = [_rope_ret(p_scr[rows, head_cols(OFF_RK, h)], _tab(tab_ref, T_RC), _tab(tab_ref, T_RS)) for h in heads]
    qb = [q.astype(BF16) for q in q_r]
    kb = [k.astype(BF16) for k in k_r]
    kd = [(k_r[h] * kdec_ref[:, lanes_of(h)]).astype(BF16) for h in heads]
    vb = [p_scr[rows, head_cols(OFF_RV, h)].astype(BF16) for h in heads]
    k_new = _rope_swa(p_scr[rows, OFF_SK:OFF_SK + KV_W], _tab(tab_ref, T_SC), _tab(tab_ref, T_S1), _tab(tab_ref, T_S2))
    v_new = p_scr[rows, OFF_SV:OFF_SV + KV_W]
    kout_ref[...] = k_new
    vout_ref[...] = v_new
    qs = [_rope_swa(p_scr[rows, head_cols(OFF_SQ, m)],
                    _tab(tab_ref, T_SC), _tab(tab_ref, T_S1), _tab(tab_ref, T_S2)).astype(BF16)
          for m in range(N_PAIRS)]
    nk = ck_ref.shape[1] + t
    k_var = [_kv_variants(jnp.concatenate([ck_ref[si], of_stream(k_new, si)], axis=0)) for si in streams]
    v_var = [_kv_variants(jnp.concatenate([cv_ref[si], of_stream(v_new, si)], axis=0)) for si in streams]

    scores = [_dot_nt(qb[h], kb[h]) for h in heads]
    inter = [[_dot(of_stream(qb[h], si), state_ref[si, h].astype(BF16)) for si in streams] for h in heads]
    ktv = [[_dot_tn(of_stream(kd[h], si), of_stream(vb[h], si)) for si in streams] for h in heads]
    sw_s = [[_dot_nt(jnp.concatenate([of_stream(qs[grp * PAIRS_PER_KV + a], si) for a in range(PAIRS_PER_KV)], axis=0),
                     _pair_rows(k_var[si][2 * grp], k_var[si][2 * grp + 1], 0, nk))
             for grp in groups] for si in streams]

    scores = [(scores[h] * dmask_ref[h]).astype(BF16) for h in heads]
    for h in heads:
        for si in streams:
            sout_ref[si, h] = gt_ref[:, lanes_of(h)] * state_ref[si, h] + ktv[h][si]
    row = lax.broadcasted_iota(jnp.int32, (PAIRS_PER_KV * t, 1), 0)
    sw_e = []
    for si in streams:
        per_group = []
        for grp in groups:
            sink_e = sinks_ref[2 * grp * PAIRS_PER_KV]
            sink_o = sinks_ref[2 * grp * PAIRS_PER_KV + 1]
            for a in range(1, PAIRS_PER_KV):
                sink_e = jnp.where(row >= a * t, sinks_ref[2 * (grp * PAIRS_PER_KV + a)], sink_e)
                sink_o = jnp.where(row >= a * t, sinks_ref[2 * (grp * PAIRS_PER_KV + a) + 1], sink_o)
            per_group.append(_swa_softmax(sw_s[si][grp], nk, sink_e, sink_o, None))
        sw_e.append(per_group)

    intra = [_dot(scores[h], vb[h]) for h in heads]
    ind = _head_indicator(nk)
    sw_o = [[_swa_out(sw_e[si][grp][0], _pair_rows(v_var[si][2 * grp], v_var[si][2 * grp + 1], 0, nk), ind,
                      sw_e[si][grp][1]).astype(BF16)
             for grp in groups] for si in streams]

    for h in heads:
        o = intra[h] + jnp.concatenate(inter[h], axis=0) * qdec_ref[:, lanes_of(h)]
        mu = jnp.mean(o, axis=-1, keepdims=True)
        oc = o - mu
        var = jnp.mean(oc * oc, axis=-1, keepdims=True)
        on = oc * lax.rsqrt(var + GN_EPS) * gnw_ref[:, lanes_of(h)]
        mix_scr[rows, lanes_of(h)] = (jax.nn.silu(p_scr[rows, head_cols(OFF_RG, h)]) * on).astype(BF16)
    for m in range(N_PAIRS):
        grp, a = divmod(m, PAIRS_PER_KV)
        o = jnp.concatenate([sw_o[si][grp][a * t:(a + 1) * t] for si in streams], axis=0)
        mix_scr[rows, RET_W + m * LANES:RET_W + (m + 1) * LANES] = o

    @pl.when(s_idx == last_s)
    def _():
        z = ALPHA * x_ref[...] + _dot(mix_scr[...], w_out_ref[...])
        x1_ref[...] = _layer_norm(z, ln1w_ref[...], ln1b_ref[...])


def _sample_mixer(x, tab, dmask, qdec, kdec, gt, cache_k, cache_v, state, sinks, w_in, gnw, w_out, ln1w, ln1b):
    nb, n_new, d = x.shape
    rows = nb * n_new
    win = cache_k.shape[1]
    ns = SAMPLE_STREAMS
    r_step = ns * n_new
    const2 = lambda si: (0, 0)
    const3 = lambda si: (0, 0, 0)
    return pl.pallas_call(
        functools.partial(_sample_mixer_kernel, n_new=n_new),
        grid=(nb // ns,),
        in_specs=[
            pl.BlockSpec(memory_space=pltpu.SMEM),
            pl.BlockSpec((rows, d), const2),
            pl.BlockSpec((r_step, N_TABS * LANES), const2),
            pl.BlockSpec((RET_HEADS, r_step, r_step), const3),
            pl.BlockSpec((r_step, RET_W), const2),
            pl.BlockSpec((r_step, RET_W), const2),
            pl.BlockSpec((1, RET_W), const2),
            pl.BlockSpec((ns, win, KV_W), lambda si: (si, 0, 0)),
            pl.BlockSpec((ns, win, KV_W), lambda si: (si, 0, 0)),
            pl.BlockSpec((ns, RET_HEADS, RET_DK, RET_DV), lambda si: (si, 0, 0, 0)),
            pl.BlockSpec((d, PROJ_W), const2, pipeline_mode=pl.Buffered(1)),
            pl.BlockSpec((1, RET_W), const2),
            pl.BlockSpec((MIX_W, d), const2, pipeline_mode=pl.Buffered(1)),
            pl.BlockSpec((1, d), const2),
            pl.BlockSpec((1, d), const2),
        ],
        out_specs=[
            pl.BlockSpec((rows, d), const2),
            pl.BlockSpec((r_step, KV_W), lambda si: (si, 0)),
            pl.BlockSpec((r_step, KV_W), lambda si: (si, 0)),
            pl.BlockSpec((ns, RET_HEADS, RET_DK, RET_DV), lambda si: (si, 0, 0, 0)),
            pl.BlockSpec((d, PROJ_W), const2),
            pl.BlockSpec((MIX_W, d), const2),
        ],
        out_shape=[
            jax.ShapeDtypeStruct((rows, d), F32),
            jax.ShapeDtypeStruct((rows, KV_W), F32),
            jax.ShapeDtypeStruct((rows, KV_W), F32),
            jax.ShapeDtypeStruct((nb, RET_HEADS, RET_DK, RET_DV), F32),
            jax.ShapeDtypeStruct((d, PROJ_W), BF16),
            jax.ShapeDtypeStruct((MIX_W, d), BF16),
        ],
        scratch_shapes=[
            pltpu.VMEM((rows, PROJ_W), F32),
            pltpu.VMEM((rows, MIX_W), BF16),
        ],
        compiler_params=pltpu.CompilerParams(
            dimension_semantics=("arbitrary",), vmem_limit_bytes=VMEM_LIMIT),
        name="sample_mixer",
    )(sinks, x.reshape(rows, d), tab, dmask, qdec, kdec, gt, cache_k, cache_v, state, w_in, gnw, w_out, ln1w, ln1b)


def _mlp_kernel(xp_ref, xs_ref, w_up_ref, w_down_ref, lnw_ref, lnb_ref, yp_ref, ys_ref, h_scr, *, prompt_steps):
    i = pl.program_id(0)

    def block(x_ref, y_ref):
        for r0 in range(0, x_ref.shape[0], MLP_UP_ROWS):
            n = min(MLP_UP_ROWS, x_ref.shape[0] - r0)
            h = _dot(x_ref[r0:r0 + n, :].astype(BF16), w_up_ref[...])
            h_scr[0:n, :] = jnp.square(jnp.maximum(h, 0.0)).astype(BF16)
            for r in range(0, n, MLP_LN_ROWS):
                rows = slice(r0 + r, r0 + r + MLP_LN_ROWS)
                z = ALPHA * x_ref[rows, :] + _dot(h_scr[r:r + MLP_LN_ROWS, :], w_down_ref[...])
                y_ref[rows, :] = _layer_norm(z, lnw_ref[...], lnb_ref[...])

    @pl.when(i < prompt_steps)
    def _():
        block(xp_ref, yp_ref)

    @pl.when(i == prompt_steps)
    def _():
        block(xs_ref, ys_ref)


def _mlp(xp, xs, w_up, w_down, lnw, lnb):
    rows, d = xp.shape
    rows_s = xs.shape[0]
    steps = rows // MLP_TILE
    const2 = lambda i: (0, 0)
    prompt_tile = lambda i: (jnp.minimum(i, steps - 1), 0)
    resident = dict(pipeline_mode=pl.Buffered(1))
    return pl.pallas_call(
        functools.partial(_mlp_kernel, prompt_steps=steps),
        grid=(steps + 1,),
        in_specs=[
            pl.BlockSpec((MLP_TILE, d), prompt_tile),
            pl.BlockSpec((rows_s, d), const2, **resident),
            pl.BlockSpec((d, D_FF), const2, **resident),
            pl.BlockSpec((D_FF, d), const2, **resident),
            pl.BlockSpec((1, d), const2),
            pl.BlockSpec((1, d), const2),
        ],
        out_specs=[
            pl.BlockSpec((MLP_TILE, d), prompt_tile),
            pl.BlockSpec((rows_s, d), const2),
        ],
        out_shape=[
            jax.ShapeDtypeStruct((rows, d), F32),
            jax.ShapeDtypeStruct((rows_s, d), F32),
        ],
        scratch_shapes=[pltpu.VMEM((MLP_UP_ROWS, D_FF), BF16)],
        compiler_params=pltpu.CompilerParams(
            dimension_semantics=("arbitrary",), vmem_limit_bytes=VMEM_LIMIT),
        name="mlp",
    )(xp, xs, w_up, w_down, lnw, lnb)


def _rope_tables(pos):
    posf = np.asarray(pos, np.float64)[:, None]
    half_r = RET_DK // 2
    ang = posf * (RET_THETA ** (-np.arange(half_r, dtype=np.float64) / half_r))[None, :]
    c, s = np.cos(ang), np.sin(ang)
    rc = np.concatenate([c, c], axis=-1)
    rs = np.concatenate([-s, s], axis=-1)
    half_s = SWA_ROT_DIM // 2
    ang = posf * (SWA_THETA ** (-np.arange(half_s, dtype=np.float64) / half_s))[None, :]
    c, s = np.cos(ang), np.sin(ang)
    t = posf.shape[0]
    rest = SWA_HD - SWA_ROT_DIM
    z = np.zeros((t, half_s))
    sc = np.tile(np.concatenate([c, c, np.ones((t, rest))], axis=-1), (1, LANES // SWA_HD))
    s1 = np.tile(np.concatenate([z, s, np.zeros((t, rest))], axis=-1), (1, LANES // SWA_HD))
    s2 = np.tile(np.concatenate([-s, z, np.zeros((t, rest))], axis=-1), (1, LANES // SWA_HD))
    return jnp.asarray(np.concatenate([rc, rs, sc, s1, s2], axis=-1).astype(np.float32))


def _decay_tables(t, copies=1):
    k_scale = RET_DK ** -0.5
    lg = np.log1p(-(2.0 ** (-5.0 - np.arange(RET_HEADS, dtype=np.float64))))
    idx = np.arange(t, dtype=np.float64)
    diff = idx[:, None] - idx[None, :]
    dmask = k_scale * np.where(diff >= 0, np.exp(lg[:, None, None] * np.maximum(diff, 0.0)), 0.0)
    qdec = np.repeat(np.exp(lg[None, :] * (idx[:, None] + 1.0)), RET_DV, axis=1)
    kdec = k_scale * np.repeat(np.exp(lg[None, :] * (t - 1.0 - idx[:, None])), RET_DV, axis=1)
    gt = np.repeat(np.exp(lg * t)[None, :], RET_DV, axis=1)
    if copies > 1:
        dmask = np.stack([np.kron(np.eye(copies), m) for m in dmask])
        qdec, kdec = np.tile(qdec, (copies, 1)), np.tile(kdec, (copies, 1))
    return tuple(jnp.asarray(a.astype(np.float32)) for a in (dmask, qdec, kdec, gt))


def kernel(x_prompt, x_sample, cache_swa_k, cache_swa_v, state_ret, w_in, ret_gn_w, swa_sinks,
           w_out, ln1_w, ln1_b, w_up, w_down, ln2_w, ln2_b):
    assert w_in.shape[0] == DEPTH == 1
    b, t, d = x_prompt.shape
    nb, n_new, _ = x_sample.shape
    win = cache_swa_k.shape[2]

    sinks = swa_sinks[0]

    tab_p = _rope_tables(np.arange(t))
    tab_s = _rope_tables(np.tile(PAST_LEN + np.arange(n_new), SAMPLE_STREAMS))
    dec_p = _decay_tables(PROMPT_TILE)
    dec_s = _decay_tables(n_new, SAMPLE_STREAMS)

    x1_s, k_s, v_s, r_s, w_in_b, w_out_b = _sample_mixer(
        x_sample, tab_s, *dec_s, cache_swa_k[0].reshape(nb, win, KV_W), cache_swa_v[0].reshape(nb, win, KV_W),
        state_ret[0], sinks, w_in[0], ret_gn_w, w_out[0], ln1_w, ln1_b)
    x1_p, k_p, v_p, r_p, w_up_b, w_down_b = _prompt_mixer(
        x_prompt, tab_p, *dec_p, sinks, w_in_b, ret_gn_w, w_out_b, ln1_w, ln1_b, w_up[0], w_down[0])
    y_p, y_s = _mlp(x1_p.reshape(b * t, d), x1_s, w_up_b, w_down_b, ln2_w, ln2_b)

    return (y_p.reshape(b, t, d), y_s.reshape(nb, n_new, d),
            k_p.reshape(DEPTH, b, SWA_WINDOW, SWA_KV_HEADS, SWA_HD),
            v_p.reshape(DEPTH, b, SWA_WINDOW, SWA_KV_HEADS, SWA_HD),
            r_p.reshape(DEPTH, b, RET_HEADS, RET_DK, RET_DV),
            k_s.reshape(DEPTH, nb, n_new, SWA_KV_HEADS, SWA_HD),
            v_s.reshape(DEPTH, nb, n_new, SWA_KV_HEADS, SWA_HD),
            r_s.reshape(DEPTH, nb, RET_HEADS, RET_DK, RET_DV))
```

```python
import functools

import numpy as np
import jax
import jax.numpy as jnp
from jax import lax
from jax.experimental import pallas as pl
from jax.experimental.pallas import tpu as pltpu

D_MODEL = 1024
DEPTH = 1
PAST_LEN = 2048
CHUNK = 64
RET_HEADS = 4
RET_DK = 128
RET_DV = 128
RET_THETA = 10000.0
RET_W = RET_HEADS * RET_DV
SWA_HEADS = 8
SWA_KV_HEADS = 2
SWA_HD = 64
SWA_WINDOW = 128
SWA_ROT_DIM = SWA_HD // 4
SWA_THETA = 500000.0
SWA_W = SWA_HEADS * SWA_HD
KV_W = SWA_KV_HEADS * SWA_HD
MIX_W = RET_W + SWA_W
D_FF = 4 * D_MODEL
OFF_RQ, OFF_RK, OFF_RV, OFF_RG = 0, RET_W, 2 * RET_W, 3 * RET_W
OFF_SQ = 4 * RET_W
OFF_SK = OFF_SQ + SWA_W
OFF_SV = OFF_SK + KV_W
PROJ_W = OFF_SV + KV_W
SWA_Q_SCALE = SWA_HD ** -0.5
assert SWA_Q_SCALE == 0.125
ALPHA = (2.0 * DEPTH) ** 0.25
LN_EPS = 1e-5
GN_EPS = 1e-5
NEG_INF = -1e30

LANES = 128
MXU_COLS = 256
N_PAIRS = SWA_W // LANES
PAIRS_PER_KV = N_PAIRS // SWA_KV_HEADS
T_RC, T_RS, T_SC, T_S1, T_S2 = range(5)
N_TABS = 5

PROMPT_TILE = 256
TILES_PER_STEP = 2
N_SLABS = PROJ_W // MXU_COLS
LN_BLOCKS = 4
SLABS_AFTER_RET_HEAD = (1, 1, 1, 0)
SLABS_AFTER_SOFTMAX_BLOCK = (1, 1, 1, 1, 1, 1, 0, 0)
SLABS_AFTER_LN_BLOCK = (1, 1, 0, 0)
MLP_CAST_STEPS = 8
SAMPLE_STREAMS = 4
MLP_TILE = 1024
MLP_UP_ROWS = 256
MLP_LN_ROWS = 128
VMEM_LIMIT = 56 * 1024 * 1024

F32 = jnp.float32
BF16 = jnp.bfloat16


def _tab(tab_ref, i):
    return tab_ref[:, i * LANES:(i + 1) * LANES]


def _dot(a, b):
    return jnp.dot(a, b, preferred_element_type=F32)


def _dot_nt(a, b):
    return lax.dot_general(a, b, (((1,), (1,)), ((), ())), preferred_element_type=F32)


def _dot_tn(a, b):
    return lax.dot_general(a, b, (((0,), (0,)), ((), ())), preferred_element_type=F32)


def _layer_norm(z, w, b):
    mu = jnp.mean(z, axis=-1, keepdims=True)
    zc = z - mu
    var = jnp.mean(zc * zc, axis=-1, keepdims=True)
    return zc * lax.rsqrt(var + LN_EPS) * w + b


def _rope_ret(x, c, s):
    return x * c + pltpu.roll(x, RET_DK // 2, axis=1) * s


def _rope_swa(x, c, s1, s2):
    half = SWA_ROT_DIM // 2
    return x * c + pltpu.roll(x, half, axis=1) * s1 + pltpu.roll(x, LANES - half, axis=1) * s2


def _retention_head(q, k, v, g, s_prev, dmask, qdec, kdec, gt, gnw):
    qb = q.astype(BF16)
    kb = k.astype(BF16)
    vb = v.astype(BF16)
    scores = _dot_nt(qb, kb) * dmask
    o = _dot(scores.astype(BF16), vb)
    o = o + _dot(qb, s_prev.astype(BF16)) * qdec
    s_new = gt * s_prev + _dot_tn((k * kdec).astype(BF16), vb)
    mu = jnp.mean(o, axis=-1, keepdims=True)
    oc = o - mu
    var = jnp.mean(oc * oc, axis=-1, keepdims=True)
    on = oc * lax.rsqrt(var + GN_EPS) * gnw
    return jax.nn.silu(g) * on, s_new


def _kv_variants(a):
    lane = lax.broadcasted_iota(jnp.int32, a.shape, 1)
    lo = lane < SWA_HD
    ar = pltpu.roll(a, SWA_HD, axis=1)
    zero = jnp.zeros_like(a)
    return (jnp.where(lo, a, zero).astype(BF16), jnp.where(lo, zero, ar).astype(BF16),
            jnp.where(lo, ar, zero).astype(BF16), jnp.where(lo, zero, a).astype(BF16))


def _pair_rows(a_e, a_o, r0, nk):
    return jnp.concatenate([a_e[r0:r0 + LANES], a_o[r0:r0 + LANES],
                            a_e[r0 + LANES:r0 + nk], a_o[r0 + LANES:r0 + nk]], axis=0)


def _swa_softmax(s, nk, sink_e, sink_o, valid):
    r = nk - LANES
    s_e, s_o, s_x = s[:, :LANES], s[:, LANES:2 * LANES], s[:, 2 * LANES:]
    if valid is not None:
        s_e = jnp.where(valid, s_e, NEG_INF)
        s_o = jnp.where(valid, s_o, NEG_INF)
    x_is_e = lax.broadcasted_iota(jnp.int32, s_x.shape, 1) < r
    rowmax = lambda a: jnp.max(a, axis=-1, keepdims=True)
    x_e = jnp.where(x_is_e, s_x, -jnp.inf)
    x_o = jnp.where(x_is_e, -jnp.inf, s_x)
    if 2 * r == LANES:
        m_e, m_o = rowmax(jnp.maximum(s_e, x_e)), rowmax(jnp.maximum(s_o, x_o))
    else:
        m_e, m_o = jnp.maximum(rowmax(s_e), rowmax(x_e)), jnp.maximum(rowmax(s_o), rowmax(x_o))
    m_e = jnp.maximum(m_e, sink_e)
    m_o = jnp.maximum(m_o, sink_o)
    e_e = jnp.exp(s_e - m_e)
    e_o = jnp.exp(s_o - m_o)
    e_x = jnp.exp(s_x - jnp.where(x_is_e, m_e, m_o))
    lane = lax.broadcasted_iota(jnp.int32, (s.shape[0], LANES), 1)
    e = jnp.concatenate([e_e, e_o, e_x], axis=1).astype(BF16)
    return e, jnp.where(lane < SWA_HD, jnp.exp(sink_e - m_e), jnp.exp(sink_o - m_o))


def _head_indicator(nk):
    row = lax.broadcasted_iota(jnp.int32, (2 * nk, LANES), 0)
    lane = lax.broadcasted_iota(jnp.int32, (2 * nk, LANES), 1)
    even_row = jnp.logical_or(row < LANES, jnp.logical_and(row >= 2 * LANES, row < LANES + nk))
    return jnp.where(even_row == (lane < SWA_HD), 1.0, 0.0).astype(BF16)


def _swa_out(e, vt, ind, sink_term):
    res = _dot(e, jnp.concatenate([vt, ind], axis=1))
    return res[:, :LANES] / (res[:, LANES:] + sink_term)


def _prompt_mixer_kernel(sinks_ref, x_ref, xn_ref, tab_ref, dmask_ref, qdec_ref, kdec_ref, gt_ref, w_in_ref,
                         gnw_ref, w_out_ref, ln1w_ref, ln1b_ref, w_up_f32_ref, w_down_f32_ref,
                         x1_ref, kout_ref, vout_ref, sout_ref, w_up_ref, w_down_ref,
                         pa_scr, pb_scr, xb_scr, s_scr, kext_scr, vext_scr, mix_scr, sw_s_scr, sw_e_scr, sw_d_scr,
                         *, steps_per_seq):
    tt = PROMPT_TILE
    g = pl.program_id(0)
    seq_step = g % steps_per_seq

    @pl.when(g < MLP_CAST_STEPS)
    def _():
        w_up_ref[...] = w_up_f32_ref[...].astype(BF16)
        w_down_ref[...] = w_down_f32_ref[...].astype(BF16)

    @pl.when(seq_step == 0)
    def _():
        s_scr[...] = jnp.zeros_like(s_scr)
        kext_scr[0:SWA_WINDOW, :] = jnp.zeros((SWA_WINDOW, KV_W), F32)
        vext_scr[0:SWA_WINDOW, :] = jnp.zeros((SWA_WINDOW, KV_W), F32)

    @pl.when(g == 0)
    def _():
        pa_scr[...] = _dot(x_ref[0, 0:tt, :].astype(BF16), w_in_ref[...])

    def project_next_slabs(p_nxt, slabs, count):
        for _ in range(count):
            c = next(slabs)
            cols = slice(c * MXU_COLS, (c + 1) * MXU_COLS)
            p_nxt[:, cols] = _dot(xb_scr[...], w_in_ref[:, cols])

    def tile(r0x, p_cur, p_nxt, load_next_x, seq_start):
        rows_x = slice(r0x, r0x + tt)
        tab = lambda i: tab_ref[rows_x, i * LANES:(i + 1) * LANES]
        xb_scr[...] = load_next_x().astype(BF16)
        slabs = iter(range(N_SLABS))

        for h in range(RET_HEADS):
            sl = slice(h * LANES, (h + 1) * LANES)
            q = _rope_ret(p_cur[:, OFF_RQ + h * LANES:OFF_RQ + (h + 1) * LANES], tab(T_RC), tab(T_RS))
            k = _rope_ret(p_cur[:, OFF_RK + h * LANES:OFF_RK + (h + 1) * LANES], tab(T_RC), tab(T_RS))
            v = p_cur[:, OFF_RV + h * LANES:OFF_RV + (h + 1) * LANES]
            gate = p_cur[:, OFF_RG + h * LANES:OFF_RG + (h + 1) * LANES]
            out, s_new = _retention_head(q, k, v, gate, s_scr[h], dmask_ref[h], qdec_ref[:, sl], kdec_ref[:, sl],
                                         gt_ref[:, sl], gnw_ref[:, sl])
            s_scr[h] = s_new
            mix_scr[:, sl] = out.astype(BF16)
            project_next_slabs(p_nxt, slabs, SLABS_AFTER_RET_HEAD[h])

        k_new = _rope_swa(p_cur[:, OFF_SK:OFF_SK + KV_W], tab(T_SC), tab(T_S1), tab(T_S2))
        v_new = p_cur[:, OFF_SV:OFF_SV + KV_W]
        kext_scr[SWA_WINDOW:SWA_WINDOW + tt, :] = k_new
        vext_scr[SWA_WINDOW:SWA_WINDOW + tt, :] = v_new
        k_var = _kv_variants(kext_scr[...])
        v_var = _kv_variants(vext_scr[...])
        nk = SWA_WINDOW + CHUNK
        ind = _head_indicator(nk)
        n_chunks = tt // CHUNK
        qs = [_rope_swa(p_cur[:, OFF_SQ + m * LANES:OFF_SQ + (m + 1) * LANES],
                        tab(T_SC), tab(T_S1), tab(T_S2)).astype(BF16)
              for m in range(N_PAIRS)]
        col = lax.broadcasted_iota(jnp.int32, (PAIRS_PER_KV * CHUNK, LANES), 1)
        row = lax.broadcasted_iota(jnp.int32, (PAIRS_PER_KV * CHUNK, 1), 0)
        blocks = [(grp, i) for grp in range(SWA_KV_HEADS) for i in range(n_chunks)]
        rows_of = lambda n: slice(n * PAIRS_PER_KV * CHUNK, (n + 1) * PAIRS_PER_KV * CHUNK)
        for n, (grp, i) in enumerate(blocks):
            r0 = i * CHUNK
            q_blk = jnp.concatenate([qs[grp * PAIRS_PER_KV + a][r0:r0 + CHUNK] for a in range(PAIRS_PER_KV)], axis=0)
            sw_s_scr[rows_of(n), :] = _dot_nt(q_blk, _pair_rows(k_var[2 * grp], k_var[2 * grp + 1], r0, nk))
        for n, (grp, i) in enumerate(blocks):
            sink_e = sinks_ref[2 * grp * PAIRS_PER_KV]
            sink_o = sinks_ref[2 * grp * PAIRS_PER_KV + 1]
            for a in range(1, PAIRS_PER_KV):
                sink_e = jnp.where(row >= a * CHUNK, sinks_ref[2 * (grp * PAIRS_PER_KV + a)], sink_e)
                sink_o = jnp.where(row >= a * CHUNK, sinks_ref[2 * (grp * PAIRS_PER_KV + a) + 1], sink_o)
            valid = None
            if seq_start is not None and i < SWA_WINDOW // CHUNK:
                valid = jnp.logical_or(col >= SWA_WINDOW - i * CHUNK, jnp.logical_not(seq_start))
            e, sink_term = _swa_softmax(sw_s_scr[rows_of(n), :], nk, sink_e, sink_o, valid)
            sw_e_scr[rows_of(n), :] = e
            sw_d_scr[rows_of(n), :] = sink_term
            project_next_slabs(p_nxt, slabs, SLABS_AFTER_SOFTMAX_BLOCK[n])
        for n, (grp, i) in enumerate(blocks):
            r0 = i * CHUNK
            vt = _pair_rows(v_var[2 * grp], v_var[2 * grp + 1], r0, nk)
            o = _swa_out(sw_e_scr[rows_of(n), :], vt, ind, sw_d_scr[rows_of(n), :]).astype(BF16)
            for a in range(PAIRS_PER_KV):
                m = grp * PAIRS_PER_KV + a
                mix_scr[r0:r0 + CHUNK, RET_W + m * LANES:RET_W + (m + 1) * LANES] = o[a * CHUNK:(a + 1) * CHUNK]

        kext_scr[0:SWA_WINDOW, :] = k_new[tt - SWA_WINDOW:, :]
        vext_scr[0:SWA_WINDOW, :] = v_new[tt - SWA_WINDOW:, :]

        x1_ref[0, rows_x, :] = ALPHA * x_ref[0, rows_x, :] + _dot(mix_scr[...], w_out_ref[...])
        lb = tt // LN_BLOCKS
        for r in range(LN_BLOCKS):
            rows_ln = slice(r0x + r * lb, r0x + (r + 1) * lb)
            x1_ref[0, rows_ln, :] = _layer_norm(x1_ref[0, rows_ln, :], ln1w_ref[...], ln1b_ref[...])
            project_next_slabs(p_nxt, slabs, SLABS_AFTER_LN_BLOCK[r])
        assert next(slabs, None) is None
        return k_new, v_new

    for i in range(TILES_PER_STEP):
        p_cur, p_nxt = (pa_scr, pb_scr) if i % 2 == 0 else (pb_scr, pa_scr)
        if i + 1 < TILES_PER_STEP:
            load_next_x = functools.partial(lambda r0: x_ref[0, r0:r0 + tt, :], (i + 1) * tt)
        else:
            load_next_x = lambda: xn_ref[0]
        k_new, v_new = tile(i * tt, p_cur, p_nxt, load_next_x, seq_step == 0 if i == 0 else None)

    @pl.when(seq_step == steps_per_seq - 1)
    def _():
        kout_ref[0] = k_new[tt - SWA_WINDOW:, :]
        vout_ref[0] = v_new[tt - SWA_WINDOW:, :]
        sout_ref[0] = s_scr[...]


def _prompt_mixer(x, tab, dmask, qdec, kdec, gt, sinks, w_in, gnw, w_out, ln1w, ln1b, w_up, w_down):
    b, t, d = x.shape
    tt = PROMPT_TILE
    rows_step = TILES_PER_STEP * tt
    sps = t // rows_step
    n_steps = b * sps
    up_rows, down_rows = w_up.shape[0] // MLP_CAST_STEPS, w_down.shape[0] // MLP_CAST_STEPS
    row_slice = lambda g: (jnp.minimum(g, MLP_CAST_STEPS - 1), 0)
    const2 = lambda g: (0, 0)
    const3 = lambda g: (0, 0, 0)

    def next_tile(g):
        gn = jnp.minimum(g + 1, n_steps - 1)
        return (gn // sps, (gn % sps) * TILES_PER_STEP, 0)

    resident = dict(pipeline_mode=pl.Buffered(1))
    return pl.pallas_call(
        functools.partial(_prompt_mixer_kernel, steps_per_seq=sps),
        grid=(n_steps,),
        in_specs=[
            pl.BlockSpec(memory_space=pltpu.SMEM),
            pl.BlockSpec((1, rows_step, d), lambda g: (g // sps, g % sps, 0)),
            pl.BlockSpec((1, tt, d), next_tile),
            pl.BlockSpec((rows_step, N_TABS * LANES), lambda g: (g % sps, 0)),
            pl.BlockSpec((RET_HEADS, tt, tt), const3, **resident),
            pl.BlockSpec((tt, RET_W), const2, **resident),
            pl.BlockSpec((tt, RET_W), const2, **resident),
            pl.BlockSpec((1, RET_W), const2),
            pl.BlockSpec((d, PROJ_W), const2, **resident),
            pl.BlockSpec((1, RET_W), const2),
            pl.BlockSpec((MIX_W, d), const2, **resident),
            pl.BlockSpec((1, d), const2),
            pl.BlockSpec((1, d), const2),
            pl.BlockSpec((up_rows, w_up.shape[1]), row_slice),
            pl.BlockSpec((down_rows, w_down.shape[1]), row_slice),
        ],
        out_specs=[
            pl.BlockSpec((1, rows_step, d), lambda g: (g // sps, g % sps, 0)),
            pl.BlockSpec((1, SWA_WINDOW, KV_W), lambda g: (g // sps, 0, 0)),
            pl.BlockSpec((1, SWA_WINDOW, KV_W), lambda g: (g // sps, 0, 0)),
            pl.BlockSpec((1, RET_HEADS, RET_DK, RET_DV), lambda g: (g // sps, 0, 0, 0)),
            pl.BlockSpec((up_rows, w_up.shape[1]), row_slice),
            pl.BlockSpec((down_rows, w_down.shape[1]), row_slice),
        ],
        out_shape=[
            jax.ShapeDtypeStruct((b, t, d), F32),
            jax.ShapeDtypeStruct((b, SWA_WINDOW, KV_W), F32),
            jax.ShapeDtypeStruct((b, SWA_WINDOW, KV_W), F32),
            jax.ShapeDtypeStruct((b, RET_HEADS, RET_DK, RET_DV), F32),
            jax.ShapeDtypeStruct(w_up.shape, BF16),
            jax.ShapeDtypeStruct(w_down.shape, BF16),
        ],
        scratch_shapes=[
            pltpu.VMEM((tt, PROJ_W), F32),
            pltpu.VMEM((tt, PROJ_W), F32),
            pltpu.VMEM((tt, d), BF16),
            pltpu.VMEM((RET_HEADS, RET_DK, RET_DV), F32),
            pltpu.VMEM((SWA_WINDOW + tt, KV_W), F32),
            pltpu.VMEM((SWA_WINDOW + tt, KV_W), F32),
            pltpu.VMEM((tt, MIX_W), BF16),
            pltpu.VMEM((N_PAIRS * tt, 2 * (SWA_WINDOW + CHUNK)), F32),
            pltpu.VMEM((N_PAIRS * tt, 2 * (SWA_WINDOW + CHUNK)), BF16),
            pltpu.VMEM((N_PAIRS * tt, LANES), F32),
        ],
        compiler_params=pltpu.CompilerParams(
            dimension_semantics=("arbitrary",), vmem_limit_bytes=VMEM_LIMIT),
        name="prompt_mixer",
    )(sinks, x, x, tab, dmask, qdec, kdec, gt, w_in, gnw, w_out, ln1w, ln1b, w_up, w_down)


def _sample_mixer_kernel(sinks_ref, x_ref, tab_ref, dmask_ref, qdec_ref, kdec_ref, gt_ref, ck_ref, cv_ref,
                         state_ref, w_in_f32_ref, gnw_ref, w_out_f32_ref, ln1w_ref, ln1b_ref,
                         x1_ref, kout_ref, vout_ref, sout_ref, w_in_ref, w_out_ref,
                         p_scr, mix_scr, *, n_new):
    s_idx = pl.program_id(0)
    last_s = pl.num_programs(0) - 1
    ns = SAMPLE_STREAMS
    t = n_new
    r_step = ns * t

    @pl.when(s_idx == 0)
    def _():
        w_in_ref[...] = w_in_f32_ref[...].astype(BF16)
        w_in_ref[:, OFF_SQ:OFF_SK] = (w_in_f32_ref[:, OFF_SQ:OFF_SK] * SWA_Q_SCALE).astype(BF16)
        w_out_ref[...] = w_out_f32_ref[...].astype(BF16)
        p_scr[...] = _dot(x_ref[...].astype(BF16), w_in_ref[...])

    rows = pl.ds(pl.multiple_of(s_idx * r_step, r_step), r_step)
    of_stream = lambda a, si: a[si * t:(si + 1) * t]
    heads = range(RET_HEADS)
    streams = range(ns)
    groups = range(SWA_KV_HEADS)

    head_cols = lambda off, h: slice(off + h * LANES, off + (h + 1) * LANES)
    lanes_of = lambda h: slice(h * LANES, (h + 1) * LANES)
    q_r = [_rope_ret(p_scr[rows, head_cols(OFF_RQ, h)], _tab(tab_ref, T_RC), _tab(tab_ref, T_RS)) for h in heads]
    k_r = [_rope_ret(p_scr[rows, head_cols(OFF_RK, h)], _tab(tab_ref, T_RC), _tab(tab_ref, T_RS)) for h in heads]
    qb = [q.astype(BF16) for q in q_r]
    kb = [k.astype(BF16) for k in k_r]
    kd = [(k_r[h] * kdec_ref[:, lanes_of(h)]).astype(BF16) for h in heads]
    vb = [p_scr[rows, head_cols(OFF_RV, h)].astype(BF16) for h in heads]
    k_new = _rope_swa(p_scr[rows, OFF_SK:OFF_SK + KV_W], _tab(tab_ref, T_SC), _tab(tab_ref, T_S1), _tab(tab_ref, T_S2))
    v_new = p_scr[rows, OFF_SV:OFF_SV + KV_W]
    kout_ref[...] = k_new
    vout_ref[...] = v_new
    qs = [_rope_swa(p_scr[rows, head_cols(OFF_SQ, m)],
                    _tab(tab_ref, T_SC), _tab(tab_ref, T_S1), _tab(tab_ref, T_S2)).astype(BF16)
          for m in range(N_PAIRS)]
    nk = ck_ref.shape[1] + t
    k_var = [_kv_variants(jnp.concatenate([ck_ref[si], of_stream(k_new, si)], axis=0)) for si in streams]
    v_var = [_kv_variants(jnp.concatenate([cv_ref[si], of_stream(v_new, si)], axis=0)) for si in streams]

    scores = [_dot_nt(qb[h], kb[h]) for h in heads]
    inter = [[_dot(of_stream(qb[h], si), state_ref[si, h].astype(BF16)) for si in streams] for h in heads]
    ktv = [[_dot_tn(of_stream(kd[h], si), of_stream(vb[h], si)) for si in streams] for h in heads]
    sw_s = [[_dot_nt(jnp.concatenate([of_stream(qs[grp * PAIRS_PER_KV + a], si) for a in range(PAIRS_PER_KV)], axis=0),
                     _pair_rows(k_var[si][2 * grp], k_var[si][2 * grp + 1], 0, nk))
             for grp in groups] for si in streams]

    scores = [(scores[h] * dmask_ref[h]).astype(BF16) for h in heads]
    for h in heads:
        for si in streams:
            sout_ref[si, h] = gt_ref[:, lanes_of(h)] * state_ref[si, h] + ktv[h][si]
    row = lax.broadcasted_iota(jnp.int32, (PAIRS_PER_KV * t, 1), 0)
    sw_e = []
    for si in streams:
        per_group = []
        for grp in groups:
            sink_e = sinks_ref[2 * grp * PAIRS_PER_KV]
            sink_o = sinks_ref[2 * grp * PAIRS_PER_KV + 1]
            for a in range(1, PAIRS_PER_KV):
                sink_e = jnp.where(row >= a * t, sinks_ref[2 * (grp * PAIRS_PER_KV + a)], sink_e)
                sink_o = jnp.where(row >= a * t, sinks_ref[2 * (grp * PAIRS_PER_KV + a) + 1], sink_o)
            per_group.append(_swa_softmax(sw_s[si][grp], nk, sink_e, sink_o, None))
        sw_e.append(per_group)

    intra = [_dot(scores[h], vb[h]) for h in heads]
    ind = _head_indicator(nk)
    sw_o = [[_swa_out(sw_e[si][grp][0], _pair_rows(v_var[si][2 * grp], v_var[si][2 * grp + 1], 0, nk), ind,
                      sw_e[si][grp][1]).astype(BF16)
             for grp in groups] for si in streams]

    for h in heads:
        o = intra[h] + jnp.concatenate(inter[h], axis=0) * qdec_ref[:, lanes_of(h)]
        mu = jnp.mean(o, axis=-1, keepdims=True)
        oc = o - mu
        var = jnp.mean(oc * oc, axis=-1, keepdims=True)
        on = oc * lax.rsqrt(var + GN_EPS) * gnw_ref[:, lanes_of(h)]
        mix_scr[rows, lanes_of(h)] = (jax.nn.silu(p_scr[rows, head_cols(OFF_RG, h)]) * on).astype(BF16)
    for m in range(N_PAIRS):
        grp, a = divmod(m, PAIRS_PER_KV)
        o = jnp.concatenate([sw_o[si][grp][a * t:(a + 1) * t] for si in streams], axis=0)
        mix_scr[rows, RET_W + m * LANES:RET_W + (m + 1) * LANES] = o

    @pl.when(s_idx == last_s)
    def _():
        z = ALPHA * x_ref[...] + _dot(mix_scr[...], w_out_ref[...])
        x1_ref[...] = _layer_norm(z, ln1w_ref[...], ln1b_ref[...])


def _sample_mixer(x, tab, dmask, qdec, kdec, gt, cache_k, cache_v, state, sinks, w_in, gnw, w_out, ln1w, ln1b):
    nb, n_new, d = x.shape
    rows = nb * n_new
    win = cache_k.shape[1]
    ns = SAMPLE_STREAMS
    r_step = ns * n_new
    const2 = lambda si: (0, 0)
    const3 = lambda si: (0, 0, 0)
    return pl.pallas_call(
        functools.partial(_sample_mixer_kernel, n_new=n_new),
        grid=(nb // ns,),
        in_specs=[
            pl.BlockSpec(memory_space=pltpu.SMEM),
            pl.BlockSpec((rows, d), const2),
            pl.BlockSpec((r_step, N_TABS * LANES), const2),
            pl.BlockSpec((RET_HEADS, r_step, r_step), const3),
            pl.BlockSpec((r_step, RET_W), const2),
            pl.BlockSpec((r_step, RET_W), const2),
            pl.BlockSpec((1, RET_W), const2),
            pl.BlockSpec((ns, win, KV_W), lambda si: (si, 0, 0)),
            pl.BlockSpec((ns, win, KV_W), lambda si: (si, 0, 0)),
            pl.BlockSpec((ns, RET_HEADS, RET_DK, RET_DV), lambda si: (si, 0, 0, 0)),
            pl.BlockSpec((d, PROJ_W), const2, pipeline_mode=pl.Buffered(1)),
            pl.BlockSpec((1, RET_W), const2),
            pl.BlockSpec((MIX_W, d), const2, pipeline_mode=pl.Buffered(1)),
            pl.BlockSpec((1, d), const2),
            pl.BlockSpec((1, d), const2),
        ],
        out_specs=[
            pl.BlockSpec((rows, d), const2),
            pl.BlockSpec((r_step, KV_W), lambda si: (si, 0)),
            pl.BlockSpec((r_step, KV_W), lambda si: (si, 0)),
            pl.BlockSpec((ns, RET_HEADS, RET_DK, RET_DV), lambda si: (si, 0, 0, 0)),
            pl.BlockSpec((d, PROJ_W), const2),
            pl.BlockSpec((MIX_W, d), const2),
        ],
        out_shape=[
            jax.ShapeDtypeStruct((rows, d), F32),
            jax.ShapeDtypeStruct((rows, KV_W), F32),
            jax.ShapeDtypeStruct((rows, KV_W), F32),
            jax.ShapeDtypeStruct((nb, RET_HEADS, RET_DK, RET_DV), F32),
            jax.ShapeDtypeStruct((d, PROJ_W), BF16),
            jax.ShapeDtypeStruct((MIX_W, d), BF16),
        ],
        scratch_shapes=[
            pltpu.VMEM((rows, PROJ_W), F32),
            pltpu.VMEM((rows, MIX_W), BF16),
        ],
        compiler_params=pltpu.CompilerParams(
            dimension_semantics=("arbitrary",), vmem_limit_bytes=VMEM_LIMIT),
        name="sample_mixer",
    )(sinks, x.reshape(rows, d), tab, dmask, qdec, kdec, gt, cache_k, cache_v, state, w_in, gnw, w_out, ln1w, ln1b)


def _mlp_kernel(xp_ref, xs_ref, w_up_ref, w_down_ref, lnw_ref, lnb_ref, yp_ref, ys_ref, h_scr, *, prompt_steps):
    i = pl.program_id(0)

    def block(x_ref, y_ref):
        def rows_block(u, carry):
            r0 = pl.multiple_of(u * MLP_UP_ROWS, MLP_UP_ROWS)
            h = _dot(x_ref[pl.ds(r0, MLP_UP_ROWS), :].astype(BF16), w_up_ref[...])
            h_scr[...] = jnp.square(jnp.maximum(h, 0.0)).astype(BF16)
            for r in range(0, MLP_UP_ROWS, MLP_LN_ROWS):
                rows = pl.ds(r0 + r, MLP_LN_ROWS)
                z = ALPHA * x_ref[rows, :] + _dot(h_scr[r:r + MLP_LN_ROWS, :], w_down_ref[...])
                y_ref[rows, :] = _layer_norm(z, lnw_ref[...], lnb_ref[...])
            return carry

        lax.fori_loop(0, x_ref.shape[0] // MLP_UP_ROWS, rows_block, 0)

    @pl.when(i < prompt_steps)
    def _():
        block(xp_ref, yp_ref)

    @pl.when(i == prompt_steps)
    def _():
        block(xs_ref, ys_ref)


def _mlp(xp, xs, w_up, w_down, lnw, lnb):
    rows, d = xp.shape
    rows_s = xs.shape[0]
    steps = rows // MLP_TILE
    const2 = lambda i: (0, 0)
    prompt_tile = lambda i: (jnp.minimum(i, steps - 1), 0)
    resident = dict(pipeline_mode=pl.Buffered(1))
    return pl.pallas_call(
        functools.partial(_mlp_kernel, prompt_steps=steps),
        grid=(steps + 1,),
        in_specs=[
            pl.BlockSpec((MLP_TILE, d), prompt_tile),
            pl.BlockSpec((rows_s, d), const2, **resident),
            pl.BlockSpec((d, D_FF), const2, **resident),
            pl.BlockSpec((D_FF, d), const2, **resident),
            pl.BlockSpec((1, d), const2),
            pl.BlockSpec((1, d), const2),
        ],
        out_specs=[
            pl.BlockSpec((MLP_TILE, d), prompt_tile),
            pl.BlockSpec((rows_s, d), const2),
        ],
        out_shape=[
            jax.ShapeDtypeStruct((rows, d), F32),
            jax.ShapeDtypeStruct((rows_s, d), F32),
        ],
        scratch_shapes=[pltpu.VMEM((MLP_UP_ROWS, D_FF), BF16)],
        compiler_params=pltpu.CompilerParams(
            dimension_semantics=("arbitrary",), vmem_limit_bytes=VMEM_LIMIT),
        name="mlp",
    )(xp, xs, w_up, w_down, lnw, lnb)


def _rope_tables(pos):
    posf = np.asarray(pos, np.float64)[:, None]
    half_r = RET_DK // 2
    ang = posf * (RET_THETA ** (-np.arange(half_r, dtype=np.float64) / half_r))[None, :]
    c, s = np.cos(ang), np.sin(ang)
    rc = np.concatenate([c, c], axis=-1)
    rs = np.concatenate([-s, s], axis=-1)
    half_s = SWA_ROT_DIM // 2
    ang = posf * (SWA_THETA ** (-np.arange(half_s, dtype=np.float64) / half_s))[None, :]
    c, s = np.cos(ang), np.sin(ang)
    t = posf.shape[0]
    rest = SWA_HD - SWA_ROT_DIM
    z = np.zeros((t, half_s))
    sc = np.tile(np.concatenate([c, c, np.ones((t, rest))], axis=-1), (1, LANES // SWA_HD))
    s1 = np.tile(np.concatenate([z, s, np.zeros((t, rest))], axis=-1), (1, LANES // SWA_HD))
    s2 = np.tile(np.concatenate([-s, z, np.zeros((t, rest))], axis=-1), (1, LANES // SWA_HD))
    return jnp.asarray(np.concatenate([rc, rs, sc, s1, s2], axis=-1).astype(np.float32))


def _decay_tables(t, copies=1):
    k_scale = RET_DK ** -0.5
    lg = np.log1p(-(2.0 ** (-5.0 - np.arange(RET_HEADS, dtype=np.float64))))
    idx = np.arange(t, dtype=np.float64)
    diff = idx[:, None] - idx[None, :]
    dmask = k_scale * np.where(diff >= 0, np.exp(lg[:, None, None] * np.maximum(diff, 0.0)), 0.0)
    qdec = np.repeat(np.exp(lg[None, :] * (idx[:, None] + 1.0)), RET_DV, axis=1)
    kdec = k_scale * np.repeat(np.exp(lg[None, :] * (t - 1.0 - idx[:, None])), RET_DV, axis=1)
    gt = np.repeat(np.exp(lg * t)[None, :], RET_DV, axis=1)
    if copies > 1:
        dmask = np.stack([np.kron(np.eye(copies), m) for m in dmask])
        qdec, kdec = np.tile(qdec, (copies, 1)), np.tile(kdec, (copies, 1))
    return tuple(jnp.asarray(a.astype(np.float32)) for a in (dmask, qdec, kdec, gt))


def kernel(x_prompt, x_sample, cache_swa_k, cache_swa_v, state_ret, w_in, ret_gn_w, swa_sinks,
           w_out, ln1_w, ln1_b, w_up, w_down, ln2_w, ln2_b):
    assert w_in.shape[0] == DEPTH == 1
    b, t, d = x_prompt.shape
    nb, n_new, _ = x_sample.shape
    win = cache_swa_k.shape[2]

    sinks = swa_sinks[0]

    tab_p = _rope_tables(np.arange(t))
    tab_s = _rope_tables(np.tile(PAST_LEN + np.arange(n_new), SAMPLE_STREAMS))
    dec_p = _decay_tables(PROMPT_TILE)
    dec_s = _decay_tables(n_new, SAMPLE_STREAMS)

    x1_s, k_s, v_s, r_s, w_in_b, w_out_b = _sample_mixer(
        x_sample, tab_s, *dec_s, cache_swa_k[0].reshape(nb, win, KV_W), cache_swa_v[0].reshape(nb, win, KV_W),
        state_ret[0], sinks, w_in[0], ret_gn_w, w_out[0], ln1_w, ln1_b)
    x1_p, k_p, v_p, r_p, w_up_b, w_down_b = _prompt_mixer(
        x_prompt, tab_p, *dec_p, sinks, w_in_b, ret_gn_w, w_out_b, ln1_w, ln1_b, w_up[0], w_down[0])
    y_p, y_s = _mlp(x1_p.reshape(b * t, d), x1_s, w_up_b, w_down_b, ln2_w, ln2_b)

    return (y_p.reshape(b, t, d), y_s.reshape(nb, n_new, d),
            k_p.reshape(DEPTH, b, SWA_WINDOW, SWA_KV_HEADS, SWA_HD),
            v_p.reshape(DEPTH, b, SWA_WINDOW, SWA_KV_HEADS, SWA_HD),
            r_p.reshape(DEPTH, b, RET_HEADS, RET_DK, RET_DV),
            k_s.reshape(DEPTH, nb, n_new, SWA_KV_HEADS, SWA_HD),
            v_s.reshape(DEPTH, nb, n_new, SWA_KV_HEADS, SWA_HD),
            r_s.reshape(DEPTH, nb, RET_HEADS, RET_DK, RET_DV))
```

```python
import functools

import numpy as np
import jax
import jax.numpy as jnp
from jax import lax
from jax.experimental import pallas as pl
from jax.experimental.pallas import tpu as pltpu

D_MODEL = 1024
DEPTH = 1
PAST_LEN = 2048
CHUNK = 64
RET_HEADS = 4
RET_DK = 128
RET_DV = 128
RET_THETA = 10000.0
RET_W = RET_HEADS * RET_DV
SWA_HEADS = 8
SWA_KV_HEADS = 2
SWA_HD = 64
SWA_WINDOW = 128
SWA_ROT_DIM = SWA_HD // 4
SWA_THETA = 500000.0
SWA_W = SWA_HEADS * SWA_HD
KV_W = SWA_KV_HEADS * SWA_HD
MIX_W = RET_W + SWA_W
D_FF = 4 * D_MODEL
OFF_RQ, OFF_RK, OFF_RV, OFF_RG = 0, RET_W, 2 * RET_W, 3 * RET_W
OFF_SQ = 4 * RET_W
OFF_SK = OFF_SQ + SWA_W
OFF_SV = OFF_SK + KV_W
PROJ_W = OFF_SV + KV_W
SWA_Q_SCALE = SWA_HD ** -0.5
assert SWA_Q_SCALE == 0.125
ALPHA = (2.0 * DEPTH) ** 0.25
LN_EPS = 1e-5
GN_EPS = 1e-5
NEG_INF = -1e30

LANES = 128
MXU_COLS = 256
N_PAIRS = SWA_W // LANES
PAIRS_PER_KV = N_PAIRS // SWA_KV_HEADS
T_RC, T_RS, T_SC, T_S1, T_S2 = range(5)
N_TABS = 5

PROMPT_TILE = 256
TILES_PER_STEP = 2
N_SLABS = PROJ_W // MXU_COLS
LN_BLOCKS = 4
SLABS_AFTER_RET_HEAD = (1, 1, 1, 0)
SLABS_AFTER_SOFTMAX_BLOCK = (1, 1, 1, 1, 1, 1, 0, 0)
SLABS_AFTER_LN_BLOCK = (1, 1, 0, 0)
MLP_CAST_STEPS = 8
SAMPLE_STREAMS = 4
MLP_TILE = 1024
MLP_UP_ROWS = 512
MLP_LN_ROWS = 128
VMEM_LIMIT = 56 * 1024 * 1024

F32 = jnp.float32
BF16 = jnp.bfloat16


def _tab(tab_ref, i):
    return tab_ref[:, i * LANES:(i + 1) * LANES]


def _dot(a, b):
    return jnp.dot(a, b, preferred_element_type=F32)


def _dot_nt(a, b):
    return lax.dot_general(a, b, (((1,), (1,)), ((), ())), preferred_element_type=F32)


def _dot_tn(a, b):
    return lax.dot_general(a, b, (((0,), (0,)), ((), ())), preferred_element_type=F32)


def _layer_norm(z, w, b):
    mu = jnp.mean(z, axis=-1, keepdims=True)
    zc = z - mu
    var = jnp.mean(zc * zc, axis=-1, keepdims=True)
    return zc * lax.rsqrt(var + LN_EPS) * w + b


def _rope_ret(x, c, s):
    return x * c + pltpu.roll(x, RET_DK // 2, axis=1) * s


def _rope_swa(x, c, s1, s2):
    half = SWA_ROT_DIM // 2
    return x * c + pltpu.roll(x, half, axis=1) * s1 + pltpu.roll(x, LANES - half, axis=1) * s2


def _retention_head(q, k, v, g, s_prev, dmask, qdec, kdec, gt, gnw):
    qb = q.astype(BF16)
    kb = k.astype(BF16)
    vb = v.astype(BF16)
    scores = _dot_nt(qb, kb) * dmask
    o = _dot(scores.astype(BF16), vb)
    o = o + _dot(qb, s_prev.astype(BF16)) * qdec
    s_new = gt * s_prev + _dot_tn((k * kdec).astype(BF16), vb)
    mu = jnp.mean(o, axis=-1, keepdims=True)
    oc = o - mu
    var = jnp.mean(oc * oc, axis=-1, keepdims=True)
    on = oc * lax.rsqrt(var + GN_EPS) * gnw
    return jax.nn.silu(g) * on, s_new


def _kv_variants(a):
    lane = lax.broadcasted_iota(jnp.int32, a.shape, 1)
    lo = lane < SWA_HD
    ar = pltpu.roll(a, SWA_HD, axis=1)
    zero = jnp.zeros_like(a)
    return (jnp.where(lo, a, zero).astype(BF16), jnp.where(lo, zero, ar).astype(BF16),
            jnp.where(lo, ar, zero).astype(BF16), jnp.where(lo, zero, a).astype(BF16))


def _pair_rows(a_e, a_o, r0, nk):
    return jnp.concatenate([a_e[r0:r0 + LANES], a_o[r0:r0 + LANES],
                            a_e[r0 + LANES:r0 + nk], a_o[r0 + LANES:r0 + nk]], axis=0)


def _swa_softmax(s, nk, sink_e, sink_o, valid):
    r = nk - LANES
    s_e, s_o, s_x = s[:, :LANES], s[:, LANES:2 * LANES], s[:, 2 * LANES:]
    if valid is not None:
        s_e = jnp.where(valid, s_e, NEG_INF)
        s_o = jnp.where(valid, s_o, NEG_INF)
    x_is_e = lax.broadcasted_iota(jnp.int32, s_x.shape, 1) < r
    rowmax = lambda a: jnp.max(a, axis=-1, keepdims=True)
    x_e = jnp.where(x_is_e, s_x, -jnp.inf)
    x_o = jnp.where(x_is_e, -jnp.inf, s_x)
    if 2 * r == LANES:
        m_e, m_o = rowmax(jnp.maximum(s_e, x_e)), rowmax(jnp.maximum(s_o, x_o))
    else:
        m_e, m_o = jnp.maximum(rowmax(s_e), rowmax(x_e)), jnp.maximum(rowmax(s_o), rowmax(x_o))
    m_e = jnp.maximum(m_e, sink_e)
    m_o = jnp.maximum(m_o, sink_o)
    e_e = jnp.exp(s_e - m_e)
    e_o = jnp.exp(s_o - m_o)
    e_x = jnp.exp(s_x - jnp.where(x_is_e, m_e, m_o))
    lane = lax.broadcasted_iota(jnp.int32, (s.shape[0], LANES), 1)
    e = jnp.concatenate([e_e, e_o, e_x], axis=1).astype(BF16)
    return e, jnp.where(lane < SWA_HD, jnp.exp(sink_e - m_e), jnp.exp(sink_o - m_o))


def _head_indicator(nk):
    row = lax.broadcasted_iota(jnp.int32, (2 * nk, LANES), 0)
    lane = lax.broadcasted_iota(jnp.int32, (2 * nk, LANES), 1)
    even_row = jnp.logical_or(row < LANES, jnp.logical_and(row >= 2 * LANES, row < LANES + nk))
    return jnp.where(even_row == (lane < SWA_HD), 1.0, 0.0).astype(BF16)


def _swa_out(e, vt, ind, sink_term):
    res = _dot(e, jnp.concatenate([vt, ind], axis=1))
    return res[:, :LANES] / (res[:, LANES:] + sink_term)


def _prompt_mixer_kernel(sinks_ref, x_ref, xn_ref, tab_ref, dmask_ref, qdec_ref, kdec_ref, gt_ref, w_in_ref,
                         gnw_ref, w_out_ref, ln1w_ref, ln1b_ref, w_up_f32_ref, w_down_f32_ref,
                         x1_ref, kout_ref, vout_ref, sout_ref, w_up_ref, w_down_ref,
                         pa_scr, pb_scr, xb_scr, s_scr, kext_scr, vext_scr, mix_scr, sw_s_scr, sw_e_scr, sw_d_scr,
                         *, steps_per_seq):
    tt = PROMPT_TILE
    g = pl.program_id(0)
    seq_step = g % steps_per_seq

    @pl.when(g < MLP_CAST_STEPS)
    def _():
        w_up_ref[...] = w_up_f32_ref[...].astype(BF16)
        w_down_ref[...] = w_down_f32_ref[...].astype(BF16)

    @pl.when(seq_step == 0)
    def _():
        s_scr[...] = jnp.zeros_like(s_scr)
        kext_scr[0:SWA_WINDOW, :] = jnp.zeros((SWA_WINDOW, KV_W), F32)
        vext_scr[0:SWA_WINDOW, :] = jnp.zeros((SWA_WINDOW, KV_W), F32)

    @pl.when(g == 0)
    def _():
        pa_scr[...] = _dot(x_ref[0, 0:tt, :].astype(BF16), w_in_ref[...])

    def project_next_slabs(p_nxt, slabs, count):
        for _ in range(count):
            c = next(slabs)
            cols = slice(c * MXU_COLS, (c + 1) * MXU_COLS)
            p_nxt[:, cols] = _dot(xb_scr[...], w_in_ref[:, cols])

    def tile(r0x, p_cur, p_nxt, load_next_x, seq_start):
        rows_x = slice(r0x, r0x + tt)
        tab = lambda i: tab_ref[rows_x, i * LANES:(i + 1) * LANES]
        xb_scr[...] = load_next_x().astype(BF16)
        slabs = iter(range(N_SLABS))

        for h in range(RET_HEADS):
            sl = slice(h * LANES, (h + 1) * LANES)
            q = _rope_ret(p_cur[:, OFF_RQ + h * LANES:OFF_RQ + (h + 1) * LANES], tab(T_RC), tab(T_RS))
            k = _rope_ret(p_cur[:, OFF_RK + h * LANES:OFF_RK + (h + 1) * LANES], tab(T_RC), tab(T_RS))
            v = p_cur[:, OFF_RV + h * LANES:OFF_RV + (h + 1) * LANES]
            gate = p_cur[:, OFF_RG + h * LANES:OFF_RG + (h + 1) * LANES]
            out, s_new = _retention_head(q, k, v, gate, s_scr[h], dmask_ref[h], qdec_ref[:, sl], kdec_ref[:, sl],
                                         gt_ref[:, sl], gnw_ref[:, sl])
            s_scr[h] = s_new
            mix_scr[:, sl] = out.astype(BF16)
            project_next_slabs(p_nxt, slabs, SLABS_AFTER_RET_HEAD[h])

        k_new = _rope_swa(p_cur[:, OFF_SK:OFF_SK + KV_W], tab(T_SC), tab(T_S1), tab(T_S2))
        v_new = p_cur[:, OFF_SV:OFF_SV + KV_W]
        kext_scr[SWA_WINDOW:SWA_WINDOW + tt, :] = k_new
        vext_scr[SWA_WINDOW:SWA_WINDOW + tt, :] = v_new
        k_var = _kv_variants(kext_scr[...])
        v_var = _kv_variants(vext_scr[...])
        nk = SWA_WINDOW + CHUNK
        ind = _head_indicator(nk)
        n_chunks = tt // CHUNK
        qs = [_rope_swa(p_cur[:, OFF_SQ + m * LANES:OFF_SQ + (m + 1) * LANES],
                        tab(T_SC), tab(T_S1), tab(T_S2)).astype(BF16)
              for m in range(N_PAIRS)]
        col = lax.broadcasted_iota(jnp.int32, (PAIRS_PER_KV * CHUNK, LANES), 1)
        row = lax.broadcasted_iota(jnp.int32, (PAIRS_PER_KV * CHUNK, 1), 0)
        blocks = [(grp, i) for grp in range(SWA_KV_HEADS) for i in range(n_chunks)]
        rows_of = lambda n: slice(n * PAIRS_PER_KV * CHUNK, (n + 1) * PAIRS_PER_KV * CHUNK)
        for n, (grp, i) in enumerate(blocks):
            r0 = i * CHUNK
            q_blk = jnp.concatenate([qs[grp * PAIRS_PER_KV + a][r0:r0 + CHUNK] for a in range(PAIRS_PER_KV)], axis=0)
            sw_s_scr[rows_of(n), :] = _dot_nt(q_blk, _pair_rows(k_var[2 * grp], k_var[2 * grp + 1], r0, nk))
        for n, (grp, i) in enumerate(blocks):
            sink_e = sinks_ref[2 * grp * PAIRS_PER_KV]
            sink_o = sinks_ref[2 * grp * PAIRS_PER_KV + 1]
            for a in range(1, PAIRS_PER_KV):
                sink_e = jnp.where(row >= a * CHUNK, sinks_ref[2 * (grp * PAIRS_PER_KV + a)], sink_e)
                sink_o = jnp.where(row >= a * CHUNK, sinks_ref[2 * (grp * PAIRS_PER_KV + a) + 1], sink_o)
            valid = None
            if seq_start is not None and i < SWA_WINDOW // CHUNK:
                valid = jnp.logical_or(col >= SWA_WINDOW - i * CHUNK, jnp.logical_not(seq_start))
            e, sink_term = _swa_softmax(sw_s_scr[rows_of(n), :], nk, sink_e, sink_o, valid)
            sw_e_scr[rows_of(n), :] = e
            sw_d_scr[rows_of(n), :] = sink_term
            project_next_slabs(p_nxt, slabs, SLABS_AFTER_SOFTMAX_BLOCK[n])
        for n, (grp, i) in enumerate(blocks):
            r0 = i * CHUNK
            vt = _pair_rows(v_var[2 * grp], v_var[2 * grp + 1], r0, nk)
            o = _swa_out(sw_e_scr[rows_of(n), :], vt, ind, sw_d_scr[rows_of(n), :]).astype(BF16)
            for a in range(PAIRS_PER_KV):
                m = grp * PAIRS_PER_KV + a
                mix_scr[r0:r0 + CHUNK, RET_W + m * LANES:RET_W + (m + 1) * LANES] = o[a * CHUNK:(a + 1) * CHUNK]

        kext_scr[0:SWA_WINDOW, :] = k_new[tt - SWA_WINDOW:, :]
        vext_scr[0:SWA_WINDOW, :] = v_new[tt - SWA_WINDOW:, :]

        x1_ref[0, rows_x, :] = ALPHA * x_ref[0, rows_x, :] + _dot(mix_scr[...], w_out_ref[...])
        lb = tt // LN_BLOCKS
        for r in range(LN_BLOCKS):
            rows_ln = slice(r0x + r * lb, r0x + (r + 1) * lb)
            x1_ref[0, rows_ln, :] = _layer_norm(x1_ref[0, rows_ln, :], ln1w_ref[...], ln1b_ref[...])
            project_next_slabs(p_nxt, slabs, SLABS_AFTER_LN_BLOCK[r])
        assert next(slabs, None) is None
        return k_new, v_new

    for i in range(TILES_PER_STEP):
        p_cur, p_nxt = (pa_scr, pb_scr) if i % 2 == 0 else (pb_scr, pa_scr)
        if i + 1 < TILES_PER_STEP:
            load_next_x = functools.partial(lambda r0: x_ref[0, r0:r0 + tt, :], (i + 1) * tt)
        else:
            load_next_x = lambda: xn_ref[0]
        k_new, v_new = tile(i * tt, p_cur, p_nxt, load_next_x, seq_step == 0 if i == 0 else None)

    @pl.when(seq_step == steps_per_seq - 1)
    def _():
        kout_ref[0] = k_new[tt - SWA_WINDOW:, :]
        vout_ref[0] = v_new[tt - SWA_WINDOW:, :]
        sout_ref[0] = s_scr[...]


def _prompt_mixer(x, tab, dmask, qdec, kdec, gt, sinks, w_in, gnw, w_out, ln1w, ln1b, w_up, w_down):
    b, t, d = x.shape
    tt = PROMPT_TILE
    rows_step = TILES_PER_STEP * tt
    sps = t // rows_step
    n_steps = b * sps
    up_rows, down_rows = w_up.shape[0] // MLP_CAST_STEPS, w_down.shape[0] // MLP_CAST_STEPS
    row_slice = lambda g: (jnp.minimum(g, MLP_CAST_STEPS - 1), 0)
    const2 = lambda g: (0, 0)
    const3 = lambda g: (0, 0, 0)

    def next_tile(g):
        gn = jnp.minimum(g + 1, n_steps - 1)
        return (gn // sps, (gn % sps) * TILES_PER_STEP, 0)

    resident = dict(pipeline_mode=pl.Buffered(1))
    return pl.pallas_call(
        functools.partial(_prompt_mixer_kernel, steps_per_seq=sps),
        grid=(n_steps,),
        in_specs=[
            pl.BlockSpec(memory_space=pltpu.SMEM),
            pl.BlockSpec((1, rows_step, d), lambda g: (g // sps, g % sps, 0)),
            pl.BlockSpec((1, tt, d), next_tile),
            pl.BlockSpec((rows_step, N_TABS * LANES), lambda g: (g % sps, 0)),
            pl.BlockSpec((RET_HEADS, tt, tt), const3, **resident),
            pl.BlockSpec((tt, RET_W), const2, **resident),
            pl.BlockSpec((tt, RET_W), const2, **resident),
            pl.BlockSpec((1, RET_W), const2),
            pl.BlockSpec((d, PROJ_W), const2, **resident),
            pl.BlockSpec((1, RET_W), const2),
            pl.BlockSpec((MIX_W, d), const2, **resident),
            pl.BlockSpec((1, d), const2),
            pl.BlockSpec((1, d), const2),
            pl.BlockSpec((up_rows, w_up.shape[1]), row_slice),
            pl.BlockSpec((down_rows, w_down.shape[1]), row_slice),
        ],
        out_specs=[
            pl.BlockSpec((1, rows_step, d), lambda g: (g // sps, g % sps, 0)),
            pl.BlockSpec((1, SWA_WINDOW, KV_W), lambda g: (g // sps, 0, 0)),
            pl.BlockSpec((1, SWA_WINDOW, KV_W), lambda g: (g // sps, 0, 0)),
            pl.BlockSpec((1, RET_HEADS, RET_DK, RET_DV), lambda g: (g // sps, 0, 0, 0)),
            pl.BlockSpec((up_rows, w_up.shape[1]), row_slice),
            pl.BlockSpec((down_rows, w_down.shape[1]), row_slice),
        ],
        out_shape=[
            jax.ShapeDtypeStruct((b, t, d), F32),
            jax.ShapeDtypeStruct((b, SWA_WINDOW, KV_W), F32),
            jax.ShapeDtypeStruct((b, SWA_WINDOW, KV_W), F32),
            jax.ShapeDtypeStruct((b, RET_HEADS, RET_DK, RET_DV), F32),
            jax.ShapeDtypeStruct(w_up.shape, BF16),
            jax.ShapeDtypeStruct(w_down.shape, BF16),
        ],
        scratch_shapes=[
            pltpu.VMEM((tt, PROJ_W), F32),
            pltpu.VMEM((tt, PROJ_W), F32),
            pltpu.VMEM((tt, d), BF16),
            pltpu.VMEM((RET_HEADS, RET_DK, RET_DV), F32),
            pltpu.VMEM((SWA_WINDOW + tt, KV_W), F32),
            pltpu.VMEM((SWA_WINDOW + tt, KV_W), F32),
            pltpu.VMEM((tt, MIX_W), BF16),
            pltpu.VMEM((N_PAIRS * tt, 2 * (SWA_WINDOW + CHUNK)), F32),
            pltpu.VMEM((N_PAIRS * tt, 2 * (SWA_WINDOW + CHUNK)), BF16),
            pltpu.VMEM((N_PAIRS * tt, LANES), F32),
        ],
        compiler_params=pltpu.CompilerParams(
            dimension_semantics=("arbitrary",), vmem_limit_bytes=VMEM_LIMIT),
        name="prompt_mixer",
    )(sinks, x, x, tab, dmask, qdec, kdec, gt, w_in, gnw, w_out, ln1w, ln1b, w_up, w_down)


def _sample_mixer_kernel(sinks_ref, x_ref, tab_ref, dmask_ref, qdec_ref, kdec_ref, gt_ref, ck_ref, cv_ref,
                         state_ref, w_in_f32_ref, gnw_ref, w_out_f32_ref, ln1w_ref, ln1b_ref,
                         x1_ref, kout_ref, vout_ref, sout_ref, w_in_ref, w_out_ref,
                         p_scr, mix_scr, *, n_new):
    s_idx = pl.program_id(0)
    last_s = pl.num_programs(0) - 1
    ns = SAMPLE_STREAMS
    t = n_new
    r_step = ns * t

    @pl.when(s_idx == 0)
    def _():
        w_in_ref[...] = w_in_f32_ref[...].astype(BF16)
        w_in_ref[:, OFF_SQ:OFF_SK] = (w_in_f32_ref[:, OFF_SQ:OFF_SK] * SWA_Q_SCALE).astype(BF16)
        w_out_ref[...] = w_out_f32_ref[...].astype(BF16)
        p_scr[...] = _dot(x_ref[...].astype(BF16), w_in_ref[...])

    rows = pl.ds(pl.multiple_of(s_idx * r_step, r_step), r_step)
    of_stream = lambda a, si: a[si * t:(si + 1) * t]
    heads = range(RET_HEADS)
    streams = range(ns)
    groups = range(SWA_KV_HEADS)

    head_cols = lambda off, h: slice(off + h * LANES, off + (h + 1) * LANES)
    lanes_of = lambda h: slice(h * LANES, (h + 1) * LANES)
    q_r = [_rope_ret(p_scr[rows, head_cols(OFF_RQ, h)], _tab(tab_ref, T_RC), _tab(tab_ref, T_RS)) for h in heads]
    k_r = [_rope_ret(p_scr[rows, head_cols(OFF_RK, h)], _tab(tab_ref, T_RC), _tab(tab_ref, T_RS)) for h in heads]
    qb = [q.astype(BF16) for q in q_r]
    kb = [k.astype(BF16) for k in k_r]
    kd = [(k_r[h] * kdec_ref[:, lanes_of(h)]).astype(BF16) for h in heads]
    vb = [p_scr[rows, head_cols(OFF_RV, h)].astype(BF16) for h in heads]
    k_new = _rope_swa(p_scr[rows, OFF_SK:OFF_SK + KV_W], _tab(tab_ref, T_SC), _tab(tab_ref, T_S1), _tab(tab_ref, T_S2))
    v_new = p_scr[rows, OFF_SV:OFF_SV + KV_W]
    kout_ref[...] = k_new
    vout_ref[...] = v_new
    qs = [_rope_swa(p_scr[rows, head_cols(OFF_SQ, m)],
                    _tab(tab_ref, T_SC), _tab(tab_ref, T_S1), _tab(tab_ref, T_S2)).astype(BF16)
          for m in range(N_PAIRS)]
    nk = ck_ref.shape[1] + t
    k_var = [_kv_variants(jnp.concatenate([ck_ref[si], of_stream(k_new, si)], axis=0)) for si in streams]
    v_var = [_kv_variants(jnp.concatenate([cv_ref[si], of_stream(v_new, si)], axis=0)) for si in streams]

    scores = [_dot_nt(qb[h], kb[h]) for h in heads]
    inter = [[_dot(of_stream(qb[h], si), state_ref[si, h].astype(BF16)) for si in streams] for h in heads]
    ktv = [[_dot_tn(of_stream(kd[h], si), of_stream(vb[h], si)) for si in streams] for h in heads]
    sw_s = [[_dot_nt(jnp.concatenate([of_stream(qs[grp * PAIRS_PER_KV + a], si) for a in range(PAIRS_PER_KV)], axis=0),
                     _pair_rows(k_var[si][2 * grp], k_var[si][2 * grp + 1], 0, nk))
             for grp in groups] for si in streams]

    scores = [(scores[h] * dmask_ref[h]).astype(BF16) for h in heads]
    for h in heads:
        for si in streams:
            sout_ref[si, h] = gt_ref[:, lanes_of(h)] * state_ref[si, h] + ktv[h][si]
    row = lax.broadcasted_iota(jnp.int32, (PAIRS_PER_KV * t, 1), 0)
    sw_e = []
    for si in streams:
        per_group = []
        for grp in groups:
            sink_e = sinks_ref[2 * grp * PAIRS_PER_KV]
            sink_o = sinks_ref[2 * grp * PAIRS_PER_KV + 1]
            for a in range(1, PAIRS_PER_KV):
                sink_e = jnp.where(row >= a * t, sinks_ref[2 * (grp * PAIRS_PER_KV + a)], sink_e)
                sink_o = jnp.where(row >= a * t, sinks_ref[2 * (grp * PAIRS_PER_KV + a) + 1], sink_o)
            per_group.append(_swa_softmax(sw_s[si][grp], nk, sink_e, sink_o, None))
        sw_e.append(per_group)

    intra = [_dot(scores[h], vb[h]) for h in heads]
    ind = _head_indicator(nk)
    sw_o = [[_swa_out(sw_e[si][grp][0], _pair_rows(v_var[si][2 * grp], v_var[si][2 * grp + 1], 0, nk), ind,
                      sw_e[si][grp][1]).astype(BF16)
             for grp in groups] for si in streams]

    for h in heads:
        o = intra[h] + jnp.concatenate(inter[h], axis=0) * qdec_ref[:, lanes_of(h)]
        mu = jnp.mean(o, axis=-1, keepdims=True)
        oc = o - mu
        var = jnp.mean(oc * oc, axis=-1, keepdims=True)
        on = oc * lax.rsqrt(var + GN_EPS) * gnw_ref[:, lanes_of(h)]
        mix_scr[rows, lanes_of(h)] = (jax.nn.silu(p_scr[rows, head_cols(OFF_RG, h)]) * on).astype(BF16)
    for m in range(N_PAIRS):
        grp, a = divmod(m, PAIRS_PER_KV)
        o = jnp.concatenate([sw_o[si][grp][a * t:(a + 1) * t] for si in streams], axis=0)
        mix_scr[rows, RET_W + m * LANES:RET_W + (m + 1) * LANES] = o

    @pl.when(s_idx == last_s)
    def _():
        z = ALPHA * x_ref[...] + _dot(mix_scr[...], w_out_ref[...])
        x1_ref[...] = _layer_norm(z, ln1w_ref[...], ln1b_ref[...])


def _sample_mixer(x, tab, dmask, qdec, kdec, gt, cache_k, cache_v, state, sinks, w_in, gnw, w_out, ln1w, ln1b):
    nb, n_new, d = x.shape
    rows = nb * n_new
    win = cache_k.shape[1]
    ns = SAMPLE_STREAMS
    r_step = ns * n_new
    const2 = lambda si: (0, 0)
    const3 = lambda si: (0, 0, 0)
    return pl.pallas_call(
        functools.partial(_sample_mixer_kernel, n_new=n_new),
        grid=(nb // ns,),
        in_specs=[
            pl.BlockSpec(memory_space=pltpu.SMEM),
            pl.BlockSpec((rows, d), const2),
            pl.BlockSpec((r_step, N_TABS * LANES), const2),
            pl.BlockSpec((RET_HEADS, r_step, r_step), const3),
            pl.BlockSpec((r_step, RET_W), const2),
            pl.BlockSpec((r_step, RET_W), const2),
            pl.BlockSpec((1, RET_W), const2),
            pl.BlockSpec((ns, win, KV_W), lambda si: (si, 0, 0)),
            pl.BlockSpec((ns, win, KV_W), lambda si: (si, 0, 0)),
            pl.BlockSpec((ns, RET_HEADS, RET_DK, RET_DV), lambda si: (si, 0, 0, 0)),
            pl.BlockSpec((d, PROJ_W), const2, pipeline_mode=pl.Buffered(1)),
            pl.BlockSpec((1, RET_W), const2),
            pl.BlockSpec((MIX_W, d), const2, pipeline_mode=pl.Buffered(1)),
            pl.BlockSpec((1, d), const2),
            pl.BlockSpec((1, d), const2),
        ],
        out_specs=[
            pl.BlockSpec((rows, d), const2),
            pl.BlockSpec((r_step, KV_W), lambda si: (si, 0)),
            pl.BlockSpec((r_step, KV_W), lambda si: (si, 0)),
            pl.BlockSpec((ns, RET_HEADS, RET_DK, RET_DV), lambda si: (si, 0, 0, 0)),
            pl.BlockSpec((d, PROJ_W), const2),
            pl.BlockSpec((MIX_W, d), const2),
        ],
        out_shape=[
            jax.ShapeDtypeStruct((rows, d), F32),
            jax.ShapeDtypeStruct((rows, KV_W), F32),
            jax.ShapeDtypeStruct((rows, KV_W), F32),
            jax.ShapeDtypeStruct((nb, RET_HEADS, RET_DK, RET_DV), F32),
            jax.ShapeDtypeStruct((d, PROJ_W), BF16),
            jax.ShapeDtypeStruct((MIX_W, d), BF16),
        ],
        scratch_shapes=[
            pltpu.VMEM((rows, PROJ_W), F32),
            pltpu.VMEM((rows, MIX_W), BF16),
        ],
        compiler_params=pltpu.CompilerParams(
            dimension_semantics=("arbitrary",), vmem_limit_bytes=VMEM_LIMIT),
        name="sample_mixer",
    )(sinks, x.reshape(rows, d), tab, dmask, qdec, kdec, gt, cache_k, cache_v, state, w_in, gnw, w_out, ln1w, ln1b)


def _mlp_kernel(xp_ref, xs_ref, w_up_ref, w_down_ref, lnw_ref, lnb_ref, yp_ref, ys_ref, h_scr, *, prompt_steps):
    i = pl.program_id(0)

    def block(x_ref, y_ref):
        for r0 in range(0, x_ref.shape[0], MLP_UP_ROWS):
            n = min(MLP_UP_ROWS, x_ref.shape[0] - r0)
            h = _dot(x_ref[r0:r0 + n, :].astype(BF16), w_up_ref[...])
            h_scr[0:n, :] = jnp.square(jnp.maximum(h, 0.0)).astype(BF16)
            for r in range(0, n, MLP_LN_ROWS):
                rows = slice(r0 + r, r0 + r + MLP_LN_ROWS)
                z = ALPHA * x_ref[rows, :] + _dot(h_scr[r:r + MLP_LN_ROWS, :], w_down_ref[...])
                y_ref[rows, :] = _layer_norm(z, lnw_ref[...], lnb_ref[...])

    @pl.when(i < prompt_steps)
    def _():
        block(xp_ref, yp_ref)

    @pl.when(i == prompt_steps)
    def _():
        block(xs_ref, ys_ref)


def _mlp(xp, xs, w_up, w_down, lnw, lnb):
    rows, d = xp.shape
    rows_s = xs.shape[0]
    steps = rows // MLP_TILE
    const2 = lambda i: (0, 0)
    prompt_tile = lambda i: (jnp.minimum(i, steps - 1), 0)
    resident = dict(pipeline_mode=pl.Buffered(1))
    return pl.pallas_call(
        functools.partial(_mlp_kernel, prompt_steps=steps),
        grid=(steps + 1,),
        in_specs=[
            pl.BlockSpec((MLP_TILE, d), prompt_tile),
            pl.BlockSpec((rows_s, d), const2, **resident),
            pl.BlockSpec((d, D_FF), const2, **resident),
            pl.BlockSpec((D_FF, d), const2, **resident),
            pl.BlockSpec((1, d), const2),
            pl.BlockSpec((1, d), const2),
        ],
        out_specs=[
            pl.BlockSpec((MLP_TILE, d), prompt_tile),
            pl.BlockSpec((rows_s, d), const2),
        ],
        out_shape=[
            jax.ShapeDtypeStruct((rows, d), F32),
            jax.ShapeDtypeStruct((rows_s, d), F32),
        ],
        scratch_shapes=[pltpu.VMEM((MLP_UP_ROWS, D_FF), BF16)],
        compiler_params=pltpu.CompilerParams(
            dimension_semantics=("arbitrary",), vmem_limit_bytes=VMEM_LIMIT),
        name="mlp",
    )(xp, xs, w_up, w_down, lnw, lnb)


def _rope_tables(pos):
    posf = np.asarray(pos, np.float64)[:, None]
    half_r = RET_DK // 2
    ang = posf * (RET_THETA ** (-np.arange(half_r, dtype=np.float64) / half_r))[None, :]
    c, s = np.cos(ang), np.sin(ang)
    rc = np.concatenate([c, c], axis=-1)
    rs = np.concatenate([-s, s], axis=-1)
    half_s = SWA_ROT_DIM // 2
    ang = posf * (SWA_THETA ** (-np.arange(half_s, dtype=np.float64) / half_s))[None, :]
    c, s = np.cos(ang), np.sin(ang)
    t = posf.shape[0]
    rest = SWA_HD - SWA_ROT_DIM
    z = np.zeros((t, half_s))
    sc = np.tile(np.concatenate([c, c, np.ones((t, rest))], axis=-1), (1, LANES // SWA_HD))
    s1 = np.tile(np.concatenate([z, s, np.zeros((t, rest))], axis=-1), (1, LANES // SWA_HD))
    s2 = np.tile(np.concatenate([-s, z, np.zeros((t, rest))], axis=-1), (1, LANES // SWA_HD))
    return jnp.asarray(np.concatenate([rc, rs, sc, s1, s2], axis=-1).astype(np.float32))


def _decay_tables(t, copies=1):
    k_scale = RET_DK ** -0.5
    lg = np.log1p(-(2.0 ** (-5.0 - np.arange(RET_HEADS, dtype=np.float64))))
    idx = np.arange(t, dtype=np.float64)
    diff = idx[:, None] - idx[None, :]
    dmask = k_scale * np.where(diff >= 0, np.exp(lg[:, None, None] * np.maximum(diff, 0.0)), 0.0)
    qdec = np.repeat(np.exp(lg[None, :] * (idx[:, None] + 1.0)), RET_DV, axis=1)
    kdec = k_scale * np.repeat(np.exp(lg[None, :] * (t - 1.0 - idx[:, None])), RET_DV, axis=1)
    gt = np.repeat(np.exp(lg * t)[None, :], RET_DV, axis=1)
    if copies > 1:
        dmask = np.stack([np.kron(np.eye(copies), m) for m in dmask])
        qdec, kdec = np.tile(qdec, (copies, 1)), np.tile(kdec, (copies, 1))
    return tuple(jnp.asarray(a.astype(np.float32)) for a in (dmask, qdec, kdec, gt))


def kernel(x_prompt, x_sample, cache_swa_k, cache_swa_v, state_ret, w_in, ret_gn_w, swa_sinks,
           w_out, ln1_w, ln1_b, w_up, w_down, ln2_w, ln2_b):
    assert w_in.shape[0] == DEPTH == 1
    b, t, d = x_prompt.shape
    nb, n_new, _ = x_sample.shape
    win = cache_swa_k.shape[2]

    sinks = swa_sinks[0]

    tab_p = _rope_tables(np.arange(t))
    tab_s = _rope_tables(np.tile(PAST_LEN + np.arange(n_new), SAMPLE_STREAMS))
    dec_p = _decay_tables(PROMPT_TILE)
    dec_s = _decay_tables(n_new, SAMPLE_STREAMS)

    x1_s, k_s, v_s, r_s, w_in_b, w_out_b = _sample_mixer(
        x_sample, tab_s, *dec_s, cache_swa_k[0].reshape(nb, win, KV_W), cache_swa_v[0].reshape(nb, win, KV_W),
        state_ret[0], sinks, w_in[0], ret_gn_w, w_out[0], ln1_w, ln1_b)
    x1_p, k_p, v_p, r_p, w_up_b, w_down_b = _prompt_mixer(
        x_prompt, tab_p, *dec_p, sinks, w_in_b, ret_gn_w, w_out_b, ln1_w, ln1_b, w_up[0], w_down[0])
    y_p, y_s = _mlp(x1_p.reshape(b * t, d), x1_s, w_up_b, w_down_b, ln2_w, ln2_b)

    return (y_p.reshape(b, t, d), y_s.reshape(nb, n_new, d),
            k_p.reshape(DEPTH, b, SWA_WINDOW, SWA_KV_HEADS, SWA_HD),
            v_p.reshape(DEPTH, b, SWA_WINDOW, SWA_KV_HEADS, SWA_HD),
            r_p.reshape(DEPTH, b, RET_HEADS, RET_DK, RET_DV),
            k_s.reshape(DEPTH, nb, n_new, SWA_KV_HEADS, SWA_HD),
            v_s.reshape(DEPTH, nb, n_new, SWA_KV_HEADS, SWA_HD),
            r_s.reshape(DEPTH, nb, RET_HEADS, RET_DK, RET_DV))
```

```python
import functools

import numpy as np
import jax
import jax.numpy as jnp
from jax import lax
from jax.experimental import pallas as pl
from jax.experimental.pallas import tpu as pltpu

D_MODEL = 1024
DEPTH = 1
PAST_LEN = 2048
CHUNK = 64
RET_HEADS = 4
RET_DK = 128
RET_DV = 128
RET_THETA = 10000.0
RET_W = RET_HEADS * RET_DV
SWA_HEADS = 8
SWA_KV_HEADS = 2
SWA_HD = 64
SWA_WINDOW = 128
SWA_ROT_DIM = SWA_HD // 4
SWA_THETA = 500000.0
SWA_W = SWA_HEADS * SWA_HD
KV_W = SWA_KV_HEADS * SWA_HD
MIX_W = RET_W + SWA_W
D_FF = 4 * D_MODEL
OFF_RQ, OFF_RK, OFF_RV, OFF_RG = 0, RET_W, 2 * RET_W, 3 * RET_W
OFF_SQ = 4 * RET_W
OFF_SK = OFF_SQ + SWA_W
OFF_SV = OFF_SK + KV_W
PROJ_W = OFF_SV + KV_W
SWA_Q_SCALE = SWA_HD ** -0.5
assert SWA_Q_SCALE == 0.125
ALPHA = (2.0 * DEPTH) ** 0.25
LN_EPS = 1e-5
GN_EPS = 1e-5
NEG_INF = -1e30

LANES = 128
MXU_COLS = 256
N_PAIRS = SWA_W // LANES
PAIRS_PER_KV = N_PAIRS // SWA_KV_HEADS
T_RC, T_RS, T_SC, T_S1, T_S2 = range(5)
N_TABS = 5

PROMPT_TILE = 256
TILES_PER_STEP = 2
N_SLABS = PROJ_W // MXU_COLS
LN_BLOCKS = 4
SLABS_AFTER_RET_HEAD = (1, 1, 1, 0)
SLABS_AFTER_SOFTMAX_BLOCK = (1, 1, 1, 1, 1, 1, 0, 0)
SLABS_AFTER_LN_BLOCK = (1, 1, 0, 0)
MLP_CAST_STEPS = 8
SAMPLE_STREAMS = 4
MLP_TILE = 1024
MLP_UP_ROWS = 512
MLP_LN_ROWS = 128
VMEM_LIMIT = 56 * 1024 * 1024

F32 = jnp.float32
BF16 = jnp.bfloat16


def _tab(tab_ref, i):
    return tab_ref[:, i * LANES:(i + 1) * LANES]


def _dot(a, b):
    return jnp.dot(a, b, preferred_element_type=F32)


def _dot_nt(a, b):
    return lax.dot_general(a, b, (((1,), (1,)), ((), ())), preferred_element_type=F32)


def _dot_tn(a, b):
    return lax.dot_general(a, b, (((0,), (0,)), ((), ())), preferred_element_type=F32)


def _layer_norm(z, w, b):
    mu = jnp.mean(z, axis=-1, keepdims=True)
    zc = z - mu
    var = jnp.mean(zc * zc, axis=-1, keepdims=True)
    return zc * lax.rsqrt(var + LN_EPS) * w + b


def _rope_ret(x, c, s):
    return x * c + pltpu.roll(x, RET_DK // 2, axis=1) * s


def _rope_swa(x, c, s1, s2):
    half = SWA_ROT_DIM // 2
    return x * c + pltpu.roll(x, half, axis=1) * s1 + pltpu.roll(x, LANES - half, axis=1) * s2


def _retention_head(q, k, v, g, s_prev, dmask, qdec, kdec, gt, gnw):
    qb = q.astype(BF16)
    kb = k.astype(BF16)
    vb = v.astype(BF16)
    scores = _dot_nt(qb, kb) * dmask
    o = _dot(scores.astype(BF16), vb)
    o = o + _dot(qb, s_prev.astype(BF16)) * qdec
    s_new = gt * s_prev + _dot_tn((k * kdec).astype(BF16), vb)
    mu = jnp.mean(o, axis=-1, keepdims=True)
    oc = o - mu
    var = jnp.mean(oc * oc, axis=-1, keepdims=True)
    on = oc * lax.rsqrt(var + GN_EPS) * gnw
    return jax.nn.silu(g) * on, s_new


def _kv_variants(a):
    lane = lax.broadcasted_iota(jnp.int32, a.shape, 1)
    lo = lane < SWA_HD
    ar = pltpu.roll(a, SWA_HD, axis=1)
    zero = jnp.zeros_like(a)
    return (jnp.where(lo, a, zero).astype(BF16), jnp.where(lo, zero, ar).astype(BF16),
            jnp.where(lo, ar, zero).astype(BF16), jnp.where(lo, zero, a).astype(BF16))


def _pair_rows(a_e, a_o, r0, nk):
    return jnp.concatenate([a_e[r0:r0 + LANES], a_o[r0:r0 + LANES],
                            a_e[r0 + LANES:r0 + nk], a_o[r0 + LANES:r0 + nk]], axis=0)


def _swa_softmax(s, nk, sink_e, sink_o, valid):
    r = nk - LANES
    s_e, s_o, s_x = s[:, :LANES], s[:, LANES:2 * LANES], s[:, 2 * LANES:]
    if valid is not None:
        s_e = jnp.where(valid, s_e, NEG_INF)
        s_o = jnp.where(valid, s_o, NEG_INF)
    x_is_e = lax.broadcasted_iota(jnp.int32, s_x.shape, 1) < r
    rowmax = lambda a: jnp.max(a, axis=-1, keepdims=True)
    x_e = jnp.where(x_is_e, s_x, -jnp.inf)
    x_o = jnp.where(x_is_e, -jnp.inf, s_x)
    if 2 * r == LANES:
        m_e, m_o = rowmax(jnp.maximum(s_e, x_e)), rowmax(jnp.maximum(s_o, x_o))
    else:
        m_e, m_o = jnp.maximum(rowmax(s_e), rowmax(x_e)), jnp.maximum(rowmax(s_o), rowmax(x_o))
    m_e = jnp.maximum(m_e, sink_e)
    m_o = jnp.maximum(m_o, sink_o)
    e_e = jnp.exp(s_e - m_e)
    e_o = jnp.exp(s_o - m_o)
    e_x = jnp.exp(s_x - jnp.where(x_is_e, m_e, m_o))
    lane = lax.broadcasted_iota(jnp.int32, (s.shape[0], LANES), 1)
    e = jnp.concatenate([e_e, e_o, e_x], axis=1).astype(BF16)
    return e, jnp.where(lane < SWA_HD, jnp.exp(sink_e - m_e), jnp.exp(sink_o - m_o))


def _head_indicator(nk):
    row = lax.broadcasted_iota(jnp.int32, (2 * nk, LANES), 0)
    lane = lax.broadcasted_iota(jnp.int32, (2 * nk, LANES), 1)
    even_row = jnp.logical_or(row < LANES, jnp.logical_and(row >= 2 * LANES, row < LANES + nk))
    return jnp.where(even_row == (lane < SWA_HD), 1.0, 0.0).astype(BF16)


def _swa_out(e, vt, ind, sink_term):
    res = _dot(e, jnp.concatenate([vt, ind], axis=1))
    return res[:, :LANES] / (res[:, LANES:] + sink_term)


def _prompt_mixer_kernel(sinks_ref, x_ref, xn_ref, tab_ref, dmask_ref, qdec_ref, kdec_ref, gt_ref, w_in_ref,
                         gnw_ref, w_out_ref, ln1w_ref, ln1b_ref, w_up_f32_ref, w_down_f32_ref,
                         x1_ref, kout_ref, vout_ref, sout_ref, w_up_ref, w_down_ref,
                         pa_scr, pb_scr, xb_scr, s_scr, kext_scr, vext_scr, mix_scr, sw_s_scr, sw_e_scr, sw_d_scr,
                         *, steps_per_seq):
    tt = PROMPT_TILE
    g = pl.program_id(0)
    seq_step = g % steps_per_seq

    @pl.when(g < MLP_CAST_STEPS)
    def _():
        w_up_ref[...] = w_up_f32_ref[...].astype(BF16)
        w_down_ref[...] = w_down_f32_ref[...].astype(BF16)

    @pl.when(seq_step == 0)
    def _():
        s_scr[...] = jnp.zeros_like(s_scr)
        kext_scr[0:SWA_WINDOW, :] = jnp.zeros((SWA_WINDOW, KV_W), F32)
        vext_scr[0:SWA_WINDOW, :] = jnp.zeros((SWA_WINDOW, KV_W), F32)

    @pl.when(g == 0)
    def _():
        pa_scr[...] = _dot(x_ref[0, 0:tt, :].astype(BF16), w_in_ref[...])

    def project_next_slabs(p_nxt, slabs, count):
        for _ in range(count):
            c = next(slabs)
            cols = slice(c * MXU_COLS, (c + 1) * MXU_COLS)
            p_nxt[:, cols] = _dot(xb_scr[...], w_in_ref[:, cols])

    def tile(r0x, p_cur, p_nxt, load_next_x, seq_start):
        rows_x = slice(r0x, r0x + tt)
        tab = lambda i: tab_ref[rows_x, i * LANES:(i + 1) * LANES]
        xb_scr[...] = load_next_x().astype(BF16)
        slabs = iter(range(N_SLABS))

        for h in range(RET_HEADS):
            sl = slice(h * LANES, (h + 1) * LANES)
            q = _rope_ret(p_cur[:, OFF_RQ + h * LANES:OFF_RQ + (h + 1) * LANES], tab(T_RC), tab(T_RS))
            k = _rope_ret(p_cur[:, OFF_RK + h * LANES:OFF_RK + (h + 1) * LANES], tab(T_RC), tab(T_RS))
            v = p_cur[:, OFF_RV + h * LANES:OFF_RV + (h + 1) * LANES]
            gate = p_cur[:, OFF_RG + h * LANES:OFF_RG + (h + 1) * LANES]
            out, s_new = _retention_head(q, k, v, gate, s_scr[h], dmask_ref[h], qdec_ref[:, sl], kdec_ref[:, sl],
                                         gt_ref[:, sl], gnw_ref[:, sl])
            s_scr[h] = s_new
            mix_scr[:, sl] = out.astype(BF16)
            project_next_slabs(p_nxt, slabs, SLABS_AFTER_RET_HEAD[h])

        k_new = _rope_swa(p_cur[:, OFF_SK:OFF_SK + KV_W], tab(T_SC), tab(T_S1), tab(T_S2))
        v_new = p_cur[:, OFF_SV:OFF_SV + KV_W]
        kext_scr[SWA_WINDOW:SWA_WINDOW + tt, :] = k_new
        vext_scr[SWA_WINDOW:SWA_WINDOW + tt, :] = v_new
        k_var = _kv_variants(kext_scr[...])
        v_var = _kv_variants(vext_scr[...])
        nk = SWA_WINDOW + CHUNK
        ind = _head_indicator(nk)
        n_chunks = tt // CHUNK
        qs = [_rope_swa(p_cur[:, OFF_SQ + m * LANES:OFF_SQ + (m + 1) * LANES],
                        tab(T_SC), tab(T_S1), tab(T_S2)).astype(BF16)
              for m in range(N_PAIRS)]
        col = lax.broadcasted_iota(jnp.int32, (PAIRS_PER_KV * CHUNK, LANES), 1)
        row = lax.broadcasted_iota(jnp.int32, (PAIRS_PER_KV * CHUNK, 1), 0)
        blocks = [(grp, i) for grp in range(SWA_KV_HEADS) for i in range(n_chunks)]
        rows_of = lambda n: slice(n * PAIRS_PER_KV * CHUNK, (n + 1) * PAIRS_PER_KV * CHUNK)
        for n, (grp, i) in enumerate(blocks):
            r0 = i * CHUNK
            q_blk = jnp.concatenate([qs[grp * PAIRS_PER_KV + a][r0:r0 + CHUNK] for a in range(PAIRS_PER_KV)], axis=0)
            sw_s_scr[rows_of(n), :] = _dot_nt(q_blk, _pair_rows(k_var[2 * grp], k_var[2 * grp + 1], r0, nk))
        for n, (grp, i) in enumerate(blocks):
            sink_e = sinks_ref[2 * grp * PAIRS_PER_KV]
            sink_o = sinks_ref[2 * grp * PAIRS_PER_KV + 1]
            for a in range(1, PAIRS_PER_KV):
                sink_e = jnp.where(row >= a * CHUNK, sinks_ref[2 * (grp * PAIRS_PER_KV + a)], sink_e)
                sink_o = jnp.where(row >= a * CHUNK, sinks_ref[2 * (grp * PAIRS_PER_KV + a) + 1], sink_o)
            valid = None
            if seq_start is not None and i < SWA_WINDOW // CHUNK:
                valid = jnp.logical_or(col >= SWA_WINDOW - i * CHUNK, jnp.logical_not(seq_start))
            e, sink_term = _swa_softmax(sw_s_scr[rows_of(n), :], nk, sink_e, sink_o, valid)
            sw_e_scr[rows_of(n), :] = e
            sw_d_scr[rows_of(n), :] = sink_term
            project_next_slabs(p_nxt, slabs, SLABS_AFTER_SOFTMAX_BLOCK[n])
        for n, (grp, i) in enumerate(blocks):
            r0 = i * CHUNK
            vt = _pair_rows(v_var[2 * grp], v_var[2 * grp + 1], r0, nk)
            o = _swa_out(sw_e_scr[rows_of(n), :], vt, ind, sw_d_scr[rows_of(n), :]).astype(BF16)
            for a in range(PAIRS_PER_KV):
                m = grp * PAIRS_PER_KV + a
                mix_scr[r0:r0 + CHUNK, RET_W + m * LANES:RET_W + (m + 1) * LANES] = o[a * CHUNK:(a + 1) * CHUNK]

        kext_scr[0:SWA_WINDOW, :] = k_new[tt - SWA_WINDOW:, :]
        vext_scr[0:SWA_WINDOW, :] = v_new[tt - SWA_WINDOW:, :]

        x1_ref[0, rows_x, :] = ALPHA * x_ref[0, rows_x, :] + _dot(mix_scr[...], w_out_ref[...])
        lb = tt // LN_BLOCKS
        for r in range(LN_BLOCKS):
            rows_ln = slice(r0x + r * lb, r0x + (r + 1) * lb)
            x1_ref[0, rows_ln, :] = _layer_norm(x1_ref[0, rows_ln, :], ln1w_ref[...], ln1b_ref[...])
            project_next_slabs(p_nxt, slabs, SLABS_AFTER_LN_BLOCK[r])
        assert next(slabs, None) is None
        return k_new, v_new

    for i in range(TILES_PER_STEP):
        p_cur, p_nxt = (pa_scr, pb_scr) if i % 2 == 0 else (pb_scr, pa_scr)
        if i + 1 < TILES_PER_STEP:
            load_next_x = functools.partial(lambda r0: x_ref[0, r0:r0 + tt, :], (i + 1) * tt)
        else:
            load_next_x = lambda: xn_ref[0]
        k_new, v_new = tile(i * tt, p_cur, p_nxt, load_next_x, seq_step == 0 if i == 0 else None)

    @pl.when(seq_step == steps_per_seq - 1)
    def _():
        for grp in range(SWA_KV_HEADS):
            kout_ref[0, grp] = k_new[tt - SWA_WINDOW:, grp * SWA_HD:(grp + 1) * SWA_HD].T
            vout_ref[0, grp] = v_new[tt - SWA_WINDOW:, grp * SWA_HD:(grp + 1) * SWA_HD].T
        sout_ref[0] = s_scr[...]


def _prompt_mixer(x, tab, dmask, qdec, kdec, gt, sinks, w_in, gnw, w_out, ln1w, ln1b, w_up, w_down):
    b, t, d = x.shape
    tt = PROMPT_TILE
    rows_step = TILES_PER_STEP * tt
    sps = t // rows_step
    n_steps = b * sps
    up_rows, down_rows = w_up.shape[0] // MLP_CAST_STEPS, w_down.shape[0] // MLP_CAST_STEPS
    row_slice = lambda g: (jnp.minimum(g, MLP_CAST_STEPS - 1), 0)
    const2 = lambda g: (0, 0)
    const3 = lambda g: (0, 0, 0)

    def next_tile(g):
        gn = jnp.minimum(g + 1, n_steps - 1)
        return (gn // sps, (gn % sps) * TILES_PER_STEP, 0)

    resident = dict(pipeline_mode=pl.Buffered(1))
    return pl.pallas_call(
        functools.partial(_prompt_mixer_kernel, steps_per_seq=sps),
        grid=(n_steps,),
        in_specs=[
            pl.BlockSpec(memory_space=pltpu.SMEM),
            pl.BlockSpec((1, rows_step, d), lambda g: (g // sps, g % sps, 0)),
            pl.BlockSpec((1, tt, d), next_tile),
            pl.BlockSpec((rows_step, N_TABS * LANES), lambda g: (g % sps, 0)),
            pl.BlockSpec((RET_HEADS, tt, tt), const3, **resident),
            pl.BlockSpec((tt, RET_W), const2, **resident),
            pl.BlockSpec((tt, RET_W), const2, **resident),
            pl.BlockSpec((1, RET_W), const2),
            pl.BlockSpec((d, PROJ_W), const2, **resident),
            pl.BlockSpec((1, RET_W), const2),
            pl.BlockSpec((MIX_W, d), const2, **resident),
            pl.BlockSpec((1, d), const2),
            pl.BlockSpec((1, d), const2),
            pl.BlockSpec((up_rows, w_up.shape[1]), row_slice),
            pl.BlockSpec((down_rows, w_down.shape[1]), row_slice),
        ],
        out_specs=[
            pl.BlockSpec((1, rows_step, d), lambda g: (g // sps, g % sps, 0)),
            pl.BlockSpec((1, SWA_KV_HEADS, SWA_HD, SWA_WINDOW), lambda g: (g // sps, 0, 0, 0)),
            pl.BlockSpec((1, SWA_KV_HEADS, SWA_HD, SWA_WINDOW), lambda g: (g // sps, 0, 0, 0)),
            pl.BlockSpec((1, RET_HEADS, RET_DK, RET_DV), lambda g: (g // sps, 0, 0, 0)),
            pl.BlockSpec((up_rows, w_up.shape[1]), row_slice),
            pl.BlockSpec((down_rows, w_down.shape[1]), row_slice),
        ],
        out_shape=[
            jax.ShapeDtypeStruct((b, t, d), F32),
            jax.ShapeDtypeStruct((b, SWA_KV_HEADS, SWA_HD, SWA_WINDOW), F32),
            jax.ShapeDtypeStruct((b, SWA_KV_HEADS, SWA_HD, SWA_WINDOW), F32),
            jax.ShapeDtypeStruct((b, RET_HEADS, RET_DK, RET_DV), F32),
            jax.ShapeDtypeStruct(w_up.shape, BF16),
            jax.ShapeDtypeStruct(w_down.shape, BF16),
        ],
        scratch_shapes=[
            pltpu.VMEM((tt, PROJ_W), F32),
            pltpu.VMEM((tt, PROJ_W), F32),
            pltpu.VMEM((tt, d), BF16),
            pltpu.VMEM((RET_HEADS, RET_DK, RET_DV), F32),
            pltpu.VMEM((SWA_WINDOW + tt, KV_W), F32),
            pltpu.VMEM((SWA_WINDOW + tt, KV_W), F32),
            pltpu.VMEM((tt, MIX_W), BF16),
            pltpu.VMEM((N_PAIRS * tt, 2 * (SWA_WINDOW + CHUNK)), F32),
            pltpu.VMEM((N_PAIRS * tt, 2 * (SWA_WINDOW + CHUNK)), BF16),
            pltpu.VMEM((N_PAIRS * tt, LANES), F32),
        ],
        compiler_params=pltpu.CompilerParams(
            dimension_semantics=("arbitrary",), vmem_limit_bytes=VMEM_LIMIT),
        name="prompt_mixer",
    )(sinks, x, x, tab, dmask, qdec, kdec, gt, w_in, gnw, w_out, ln1w, ln1b, w_up, w_down)


def _sample_mixer_kernel(sinks_ref, x_ref, tab_ref, dmask_ref, qdec_ref, kdec_ref, gt_ref, ck_ref, cv_ref,
                         state_ref, w_in_f32_ref, gnw_ref, w_out_f32_ref, ln1w_ref, ln1b_ref,
                         x1_ref, kout_ref, vout_ref, sout_ref, w_in_ref, w_out_ref,
                         p_scr, mix_scr, *, n_new):
    s_idx = pl.program_id(0)
    last_s = pl.num_programs(0) - 1
    ns = SAMPLE_STREAMS
    t = n_new
    r_step = ns * t

    @pl.when(s_idx == 0)
    def _():
        w_in_ref[...] = w_in_f32_ref[...].astype(BF16)
        w_in_ref[:, OFF_SQ:OFF_SK] = (w_in_f32_ref[:, OFF_SQ:OFF_SK] * SWA_Q_SCALE).astype(BF16)
        w_out_ref[...] = w_out_f32_ref[...].astype(BF16)
        p_scr[...] = _dot(x_ref[...].astype(BF16), w_in_ref[...])

    rows = pl.ds(pl.multiple_of(s_idx * r_step, r_step), r_step)
    of_stream = lambda a, si: a[si * t:(si + 1) * t]
    heads = range(RET_HEADS)
    streams = range(ns)
    groups = range(SWA_KV_HEADS)

    head_cols = lambda off, h: slice(off + h * LANES, off + (h + 1) * LANES)
    lanes_of = lambda h: slice(h * LANES, (h + 1) * LANES)
    q_r = [_rope_ret(p_scr[rows, head_cols(OFF_RQ, h)], _tab(tab_ref, T_RC), _tab(tab_ref, T_RS)) for h in heads]
    k_r = [_rope_ret(p_scr[rows, head_cols(OFF_RK, h)], _tab(tab_ref, T_RC), _tab(tab_ref, T_RS)) for h in heads]
    qb = [q.astype(BF16) for q in q_r]
    kb = [k.astype(BF16) for k in k_r]
    kd = [(k_r[h] * kdec_ref[:, lanes_of(h)]).astype(BF16) for h in heads]
    vb = [p_scr[rows, head_cols(OFF_RV, h)].astype(BF16) for h in heads]
    k_new = _rope_swa(p_scr[rows, OFF_SK:OFF_SK + KV_W], _tab(tab_ref, T_SC), _tab(tab_ref, T_S1), _tab(tab_ref, T_S2))
    v_new = p_scr[rows, OFF_SV:OFF_SV + KV_W]
    kout_ref[...] = k_new
    vout_ref[...] = v_new
    qs = [_rope_swa(p_scr[rows, head_cols(OFF_SQ, m)],
                    _tab(tab_ref, T_SC), _tab(tab_ref, T_S1), _tab(tab_ref, T_S2)).astype(BF16)
          for m in range(N_PAIRS)]
    nk = ck_ref.shape[-1] + t
    cached = lambda ref, si: jnp.concatenate([ref[si, g].T for g in groups], axis=1)
    k_var = [_kv_variants(jnp.concatenate([cached(ck_ref, si), of_stream(k_new, si)], axis=0)) for si in streams]
    v_var = [_kv_variants(jnp.concatenate([cached(cv_ref, si), of_stream(v_new, si)], axis=0)) for si in streams]

    scores = [_dot_nt(qb[h], kb[h]) for h in heads]
    inter = [[_dot(of_stream(qb[h], si), state_ref[si, h].astype(BF16)) for si in streams] for h in heads]
    ktv = [[_dot_tn(of_stream(kd[h], si), of_stream(vb[h], si)) for si in streams] for h in heads]
    sw_s = [[_dot_nt(jnp.concatenate([of_stream(qs[grp * PAIRS_PER_KV + a], si) for a in range(PAIRS_PER_KV)], axis=0),
                     _pair_rows(k_var[si][2 * grp], k_var[si][2 * grp + 1], 0, nk))
             for grp in groups] for si in streams]

    scores = [(scores[h] * dmask_ref[h]).astype(BF16) for h in heads]
    for h in heads:
        for si in streams:
            sout_ref[si, h] = gt_ref[:, lanes_of(h)] * state_ref[si, h] + ktv[h][si]
    row = lax.broadcasted_iota(jnp.int32, (PAIRS_PER_KV * t, 1), 0)
    sw_e = []
    for si in streams:
        per_group = []
        for grp in groups:
            sink_e = sinks_ref[2 * grp * PAIRS_PER_KV]
            sink_o = sinks_ref[2 * grp * PAIRS_PER_KV + 1]
            for a in range(1, PAIRS_PER_KV):
                sink_e = jnp.where(row >= a * t, sinks_ref[2 * (grp * PAIRS_PER_KV + a)], sink_e)
                sink_o = jnp.where(row >= a * t, sinks_ref[2 * (grp * PAIRS_PER_KV + a) + 1], sink_o)
            per_group.append(_swa_softmax(sw_s[si][grp], nk, sink_e, sink_o, None))
        sw_e.append(per_group)

    intra = [_dot(scores[h], vb[h]) for h in heads]
    ind = _head_indicator(nk)
    sw_o = [[_swa_out(sw_e[si][grp][0], _pair_rows(v_var[si][2 * grp], v_var[si][2 * grp + 1], 0, nk), ind,
                      sw_e[si][grp][1]).astype(BF16)
             for grp in groups] for si in streams]

    for h in heads:
        o = intra[h] + jnp.concatenate(inter[h], axis=0) * qdec_ref[:, lanes_of(h)]
        mu = jnp.mean(o, axis=-1, keepdims=True)
        oc = o - mu
        var = jnp.mean(oc * oc, axis=-1, keepdims=True)
        on = oc * lax.rsqrt(var + GN_EPS) * gnw_ref[:, lanes_of(h)]
        mix_scr[rows, lanes_of(h)] = (jax.nn.silu(p_scr[rows, head_cols(OFF_RG, h)]) * on).astype(BF16)
    for m in range(N_PAIRS):
        grp, a = divmod(m, PAIRS_PER_KV)
        o = jnp.concatenate([sw_o[si][grp][a * t:(a + 1) * t] for si in streams], axis=0)
        mix_scr[rows, RET_W + m * LANES:RET_W + (m + 1) * LANES] = o

    @pl.when(s_idx == last_s)
    def _():
        z = ALPHA * x_ref[...] + _dot(mix_scr[...], w_out_ref[...])
        x1_ref[...] = _layer_norm(z, ln1w_ref[...], ln1b_ref[...])


def _sample_mixer(x, tab, dmask, qdec, kdec, gt, cache_k, cache_v, state, sinks, w_in, gnw, w_out, ln1w, ln1b):
    nb, n_new, d = x.shape
    rows = nb * n_new
    win = cache_k.shape[-1]
    ns = SAMPLE_STREAMS
    r_step = ns * n_new
    const2 = lambda si: (0, 0)
    const3 = lambda si: (0, 0, 0)
    return pl.pallas_call(
        functools.partial(_sample_mixer_kernel, n_new=n_new),
        grid=(nb // ns,),
        in_specs=[
            pl.BlockSpec(memory_space=pltpu.SMEM),
            pl.BlockSpec((rows, d), const2),
            pl.BlockSpec((r_step, N_TABS * LANES), const2),
            pl.BlockSpec((RET_HEADS, r_step, r_step), const3),
            pl.BlockSpec((r_step, RET_W), const2),
            pl.BlockSpec((r_step, RET_W), const2),
            pl.BlockSpec((1, RET_W), const2),
            pl.BlockSpec((ns, SWA_KV_HEADS, SWA_HD, win), lambda si: (si, 0, 0, 0)),
            pl.BlockSpec((ns, SWA_KV_HEADS, SWA_HD, win), lambda si: (si, 0, 0, 0)),
            pl.BlockSpec((ns, RET_HEADS, RET_DK, RET_DV), lambda si: (si, 0, 0, 0)),
            pl.BlockSpec((d, PROJ_W), const2, pipeline_mode=pl.Buffered(1)),
            pl.BlockSpec((1, RET_W), const2),
            pl.BlockSpec((MIX_W, d), const2, pipeline_mode=pl.Buffered(1)),
            pl.BlockSpec((1, d), const2),
            pl.BlockSpec((1, d), const2),
        ],
        out_specs=[
            pl.BlockSpec((rows, d), const2),
            pl.BlockSpec((r_step, KV_W), lambda si: (si, 0)),
            pl.BlockSpec((r_step, KV_W), lambda si: (si, 0)),
            pl.BlockSpec((ns, RET_HEADS, RET_DK, RET_DV), lambda si: (si, 0, 0, 0)),
            pl.BlockSpec((d, PROJ_W), const2),
            pl.BlockSpec((MIX_W, d), const2),
        ],
        out_shape=[
            jax.ShapeDtypeStruct((rows, d), F32),
            jax.ShapeDtypeStruct((rows, KV_W), F32),
            jax.ShapeDtypeStruct((rows, KV_W), F32),
            jax.ShapeDtypeStruct((nb, RET_HEADS, RET_DK, RET_DV), F32),
            jax.ShapeDtypeStruct((d, PROJ_W), BF16),
            jax.ShapeDtypeStruct((MIX_W, d), BF16),
        ],
        scratch_shapes=[
            pltpu.VMEM((rows, PROJ_W), F32),
            pltpu.VMEM((rows, MIX_W), BF16),
        ],
        compiler_params=pltpu.CompilerParams(
            dimension_semantics=("arbitrary",), vmem_limit_bytes=VMEM_LIMIT),
        name="sample_mixer",
    )(sinks, x.reshape(rows, d), tab, dmask, qdec, kdec, gt, cache_k, cache_v, state, w_in, gnw, w_out, ln1w, ln1b)


def _mlp_kernel(xp_ref, xs_ref, w_up_ref, w_down_ref, lnw_ref, lnb_ref, yp_ref, ys_ref, h_scr, *, prompt_steps):
    i = pl.program_id(0)

    def block(x_ref, y_ref):
        for r0 in range(0, x_ref.shape[0], MLP_UP_ROWS):
            n = min(MLP_UP_ROWS, x_ref.shape[0] - r0)
            h = _dot(x_ref[r0:r0 + n, :].astype(BF16), w_up_ref[...])
            h_scr[0:n, :] = jnp.square(jnp.maximum(h, 0.0)).astype(BF16)
            for r in range(0, n, MLP_LN_ROWS):
                rows = slice(r0 + r, r0 + r + MLP_LN_ROWS)
                z = ALPHA * x_ref[rows, :] + _dot(h_scr[r:r + MLP_LN_ROWS, :], w_down_ref[...])
                y_ref[rows, :] = _layer_norm(z, lnw_ref[...], lnb_ref[...])

    @pl.when(i < prompt_steps)
    def _():
        block(xp_ref, yp_ref)

    @pl.when(i == prompt_steps)
    def _():
        block(xs_ref, ys_ref)


def _mlp(xp, xs, w_up, w_down, lnw, lnb):
    rows, d = xp.shape
    rows_s = xs.shape[0]
    steps = rows // MLP_TILE
    const2 = lambda i: (0, 0)
    prompt_tile = lambda i: (jnp.minimum(i, steps - 1), 0)
    resident = dict(pipeline_mode=pl.Buffered(1))
    return pl.pallas_call(
        functools.partial(_mlp_kernel, prompt_steps=steps),
        grid=(steps + 1,),
        in_specs=[
            pl.BlockSpec((MLP_TILE, d), prompt_tile),
            pl.BlockSpec((rows_s, d), const2, **resident),
            pl.BlockSpec((d, D_FF), const2, **resident),
            pl.BlockSpec((D_FF, d), const2, **resident),
            pl.BlockSpec((1, d), const2),
            pl.BlockSpec((1, d), const2),
        ],
        out_specs=[
            pl.BlockSpec((MLP_TILE, d), prompt_tile),
            pl.BlockSpec((rows_s, d), const2),
        ],
        out_shape=[
            jax.ShapeDtypeStruct((rows, d), F32),
            jax.ShapeDtypeStruct((rows_s, d), F32),
        ],
        scratch_shapes=[pltpu.VMEM((MLP_UP_ROWS, D_FF), BF16)],
        compiler_params=pltpu.CompilerParams(
            dimension_semantics=("arbitrary",), vmem_limit_bytes=VMEM_LIMIT),
        name="mlp",
    )(xp, xs, w_up, w_down, lnw, lnb)


def _rope_tables(pos):
    posf = np.asarray(pos, np.float64)[:, None]
    half_r = RET_DK // 2
    ang = posf * (RET_THETA ** (-np.arange(half_r, dtype=np.float64) / half_r))[None, :]
    c, s = np.cos(ang), np.sin(ang)
    rc = np.concatenate([c, c], axis=-1)
    rs = np.concatenate([-s, s], axis=-1)
    half_s = SWA_ROT_DIM // 2
    ang = posf * (SWA_THETA ** (-np.arange(half_s, dtype=np.float64) / half_s))[None, :]
    c, s = np.cos(ang), np.sin(ang)
    t = posf.shape[0]
    rest = SWA_HD - SWA_ROT_DIM
    z = np.zeros((t, half_s))
    sc = np.tile(np.concatenate([c, c, np.ones((t, rest))], axis=-1), (1, LANES // SWA_HD))
    s1 = np.tile(np.concatenate([z, s, np.zeros((t, rest))], axis=-1), (1, LANES // SWA_HD))
    s2 = np.tile(np.concatenate([-s, z, np.zeros((t, rest))], axis=-1), (1, LANES // SWA_HD))
    return jnp.asarray(np.concatenate([rc, rs, sc, s1, s2], axis=-1).astype(np.float32))


def _decay_tables(t, copies=1):
    k_scale = RET_DK ** -0.5
    lg = np.log1p(-(2.0 ** (-5.0 - np.arange(RET_HEADS, dtype=np.float64))))
    idx = np.arange(t, dtype=np.float64)
    diff = idx[:, None] - idx[None, :]
    dmask = k_scale * np.where(diff >= 0, np.exp(lg[:, None, None] * np.maximum(diff, 0.0)), 0.0)
    qdec = np.repeat(np.exp(lg[None, :] * (idx[:, None] + 1.0)), RET_DV, axis=1)
    kdec = k_scale * np.repeat(np.exp(lg[None, :] * (t - 1.0 - idx[:, None])), RET_DV, axis=1)
    gt = np.repeat(np.exp(lg * t)[None, :], RET_DV, axis=1)
    if copies > 1:
        dmask = np.stack([np.kron(np.eye(copies), m) for m in dmask])
        qdec, kdec = np.tile(qdec, (copies, 1)), np.tile(kdec, (copies, 1))
    return tuple(jnp.asarray(a.astype(np.float32)) for a in (dmask, qdec, kdec, gt))


def kernel(x_prompt, x_sample, cache_swa_k, cache_swa_v, state_ret, w_in, ret_gn_w, swa_sinks,
           w_out, ln1_w, ln1_b, w_up, w_down, ln2_w, ln2_b):
    assert w_in.shape[0] == DEPTH == 1
    b, t, d = x_prompt.shape
    nb, n_new, _ = x_sample.shape
    win = cache_swa_k.shape[2]

    sinks = swa_sinks[0]

    tab_p = _rope_tables(np.arange(t))
    tab_s = _rope_tables(np.tile(PAST_LEN + np.arange(n_new), SAMPLE_STREAMS))
    dec_p = _decay_tables(PROMPT_TILE)
    dec_s = _decay_tables(n_new, SAMPLE_STREAMS)

    x1_s, k_s, v_s, r_s, w_in_b, w_out_b = _sample_mixer(
        x_sample, tab_s, *dec_s, jnp.transpose(cache_swa_k[0], (0, 2, 3, 1)), jnp.transpose(cache_swa_v[0], (0, 2, 3, 1)),
        state_ret[0], sinks, w_in[0], ret_gn_w, w_out[0], ln1_w, ln1_b)
    x1_p, k_p, v_p, r_p, w_up_b, w_down_b = _prompt_mixer(
        x_prompt, tab_p, *dec_p, sinks, w_in_b, ret_gn_w, w_out_b, ln1_w, ln1_b, w_up[0], w_down[0])
    y_p, y_s = _mlp(x1_p.reshape(b * t, d), x1_s, w_up_b, w_down_b, ln2_w, ln2_b)

    return (y_p.reshape(b, t, d), y_s.reshape(nb, n_new, d),
            jnp.transpose(k_p, (0, 3, 1, 2))[None],
            jnp.transpose(v_p, (0, 3, 1, 2))[None],
            r_p.reshape(DEPTH, b, RET_HEADS, RET_DK, RET_DV),
            k_s.reshape(DEPTH, nb, n_new, SWA_KV_HEADS, SWA_HD),
            v_s.reshape(DEPTH, nb, n_new, SWA_KV_HEADS, SWA_HD),
            r_s.reshape(DEPTH, nb, RET_HEADS, RET_DK, RET_DV))
```

```python
import functools

import numpy as np
import jax
import jax.numpy as jnp
from jax import lax
from jax.experimental import pallas as pl
from jax.experimental.pallas import tpu as pltpu

D_MODEL = 1024
DEPTH = 1
PAST_LEN = 2048
CHUNK = 64
RET_HEADS = 4
RET_DK = 128
RET_DV = 128
RET_THETA = 10000.0
RET_W = RET_HEADS * RET_DV
SWA_HEADS = 8
SWA_KV_HEADS = 2
SWA_HD = 64
SWA_WINDOW = 128
SWA_ROT_DIM = SWA_HD // 4
SWA_THETA = 500000.0
SWA_W = SWA_HEADS * SWA_HD
KV_W = SWA_KV_HEADS * SWA_HD
MIX_W = RET_W + SWA_W
D_FF = 4 * D_MODEL
OFF_RQ, OFF_RK, OFF_RV, OFF_RG = 0, RET_W, 2 * RET_W, 3 * RET_W
OFF_SQ = 4 * RET_W
OFF_SK = OFF_SQ + SWA_W
OFF_SV = OFF_SK + KV_W
PROJ_W = OFF_SV + KV_W
SWA_Q_SCALE = SWA_HD ** -0.5
assert SWA_Q_SCALE == 0.125
ALPHA = (2.0 * DEPTH) ** 0.25
LN_EPS = 1e-5
GN_EPS = 1e-5
NEG_INF = -1e30

LANES = 128
MXU_COLS = 256
N_PAIRS = SWA_W // LANES
PAIRS_PER_KV = N_PAIRS // SWA_KV_HEADS
T_RC, T_RS, T_SC, T_S1, T_S2 = range(5)
N_TABS = 5

PROMPT_TILE = 256
TILES_PER_STEP = 2
N_SLABS = PROJ_W // MXU_COLS
LN_BLOCKS = 4
SLABS_AFTER_RET_PHASE = (1, 1, 1)
SLABS_AFTER_SOFTMAX_BLOCK = (1, 1, 1, 1, 1, 1, 0, 0)
SLABS_AFTER_LN_BLOCK = (1, 1, 0, 0)
MLP_CAST_STEPS = 8
SAMPLE_STREAMS = 8
MLP_TILE = 1024
MLP_UP_ROWS = 512
MLP_LN_ROWS = 128
VMEM_LIMIT = 56 * 1024 * 1024

F32 = jnp.float32
BF16 = jnp.bfloat16


def _tab(tab_ref, i):
    return tab_ref[:, i * LANES:(i + 1) * LANES]


def _dot(a, b):
    return jnp.dot(a, b, preferred_element_type=F32)


def _dot_nt(a, b):
    return lax.dot_general(a, b, (((1,), (1,)), ((), ())), preferred_element_type=F32)


def _dot_tn(a, b):
    return lax.dot_general(a, b, (((0,), (0,)), ((), ())), preferred_element_type=F32)


def _layer_norm(z, w, b):
    mu = jnp.mean(z, axis=-1, keepdims=True)
    zc = z - mu
    var = jnp.mean(zc * zc, axis=-1, keepdims=True)
    return zc * lax.rsqrt(var + LN_EPS) * w + b


def _rope_ret(x, c, s):
    return x * c + pltpu.roll(x, RET_DK // 2, axis=1) * s


def _rope_swa(x, c, s1, s2):
    half = SWA_ROT_DIM // 2
    return x * c + pltpu.roll(x, half, axis=1) * s1 + pltpu.roll(x, LANES - half, axis=1) * s2


def _kv_variants(a):
    lane = lax.broadcasted_iota(jnp.int32, a.shape, 1)
    lo = lane < SWA_HD
    ar = pltpu.roll(a, SWA_HD, axis=1)
    zero = jnp.zeros_like(a)
    return (jnp.where(lo, a, zero).astype(BF16), jnp.where(lo, zero, ar).astype(BF16),
            jnp.where(lo, ar, zero).astype(BF16), jnp.where(lo, zero, a).astype(BF16))


def _pair_rows(a_e, a_o, r0, nk):
    return jnp.concatenate([a_e[r0:r0 + LANES], a_o[r0:r0 + LANES],
                            a_e[r0 + LANES:r0 + nk], a_o[r0 + LANES:r0 + nk]], axis=0)


def _swa_softmax(s, nk, sink_e, sink_o, valid):
    r = nk - LANES
    s_e, s_o, s_x = s[:, :LANES], s[:, LANES:2 * LANES], s[:, 2 * LANES:]
    if valid is not None:
        s_e = jnp.where(valid, s_e, NEG_INF)
        s_o = jnp.where(valid, s_o, NEG_INF)
    x_is_e = lax.broadcasted_iota(jnp.int32, s_x.shape, 1) < r
    rowmax = lambda a: jnp.max(a, axis=-1, keepdims=True)
    x_e = jnp.where(x_is_e, s_x, -jnp.inf)
    x_o = jnp.where(x_is_e, -jnp.inf, s_x)
    if 2 * r == LANES:
        m_e, m_o = rowmax(jnp.maximum(s_e, x_e)), rowmax(jnp.maximum(s_o, x_o))
    else:
        m_e, m_o = jnp.maximum(rowmax(s_e), rowmax(x_e)), jnp.maximum(rowmax(s_o), rowmax(x_o))
    m_e = jnp.maximum(m_e, sink_e)
    m_o = jnp.maximum(m_o, sink_o)
    e_e = jnp.exp(s_e - m_e)
    e_o = jnp.exp(s_o - m_o)
    e_x = jnp.exp(s_x - jnp.where(x_is_e, m_e, m_o))
    lane = lax.broadcasted_iota(jnp.int32, (s.shape[0], LANES), 1)
    e = jnp.concatenate([e_e, e_o, e_x], axis=1).astype(BF16)
    return e, jnp.where(lane < SWA_HD, jnp.exp(sink_e - m_e), jnp.exp(sink_o - m_o))


def _head_indicator(nk):
    row = lax.broadcasted_iota(jnp.int32, (2 * nk, LANES), 0)
    lane = lax.broadcasted_iota(jnp.int32, (2 * nk, LANES), 1)
    even_row = jnp.logical_or(row < LANES, jnp.logical_and(row >= 2 * LANES, row < LANES + nk))
    return jnp.where(even_row == (lane < SWA_HD), 1.0, 0.0).astype(BF16)


def _swa_out(e, vt, ind, sink_term):
    res = _dot(e, jnp.concatenate([vt, ind], axis=1))
    return res[:, :LANES] / (res[:, LANES:] + sink_term)


def _prompt_mixer_kernel(sinks_ref, x_ref, xn_ref, tab_ref, dmask_ref, qdec_ref, kdec_ref, gt_ref, w_in_ref,
                         gnw_ref, w_out_ref, ln1w_ref, ln1b_ref, w_up_f32_ref, w_down_f32_ref,
                         x1_ref, kout_ref, vout_ref, sout_ref, w_up_ref, w_down_ref,
                         pa_scr, pb_scr, xb_scr, s_scr, kext_scr, vext_scr, mix_scr, sw_s_scr, sw_e_scr, sw_d_scr,
                         *, steps_per_seq):
    tt = PROMPT_TILE
    g = pl.program_id(0)
    seq_step = g % steps_per_seq

    @pl.when(g < MLP_CAST_STEPS)
    def _():
        w_up_ref[...] = w_up_f32_ref[...].astype(BF16)
        w_down_ref[...] = w_down_f32_ref[...].astype(BF16)

    @pl.when(seq_step == 0)
    def _():
        s_scr[...] = jnp.zeros_like(s_scr)
        kext_scr[0:SWA_WINDOW, :] = jnp.zeros((SWA_WINDOW, KV_W), F32)
        vext_scr[0:SWA_WINDOW, :] = jnp.zeros((SWA_WINDOW, KV_W), F32)

    @pl.when(g == 0)
    def _():
        pa_scr[...] = _dot(x_ref[0, 0:tt, :].astype(BF16), w_in_ref[...])

    def project_next_slabs(p_nxt, slabs, count):
        for _ in range(count):
            c = next(slabs)
            cols = slice(c * MXU_COLS, (c + 1) * MXU_COLS)
            p_nxt[:, cols] = _dot(xb_scr[...], w_in_ref[:, cols])

    def tile(r0x, p_cur, p_nxt, load_next_x, seq_start):
        rows_x = slice(r0x, r0x + tt)
        tab = lambda i: tab_ref[rows_x, i * LANES:(i + 1) * LANES]
        xb_scr[...] = load_next_x().astype(BF16)
        slabs = iter(range(N_SLABS))

        heads = range(RET_HEADS)
        lanes_of = lambda h: slice(h * LANES, (h + 1) * LANES)
        head_cols = lambda off, h: slice(off + h * LANES, off + (h + 1) * LANES)
        q_r = [_rope_ret(p_cur[:, head_cols(OFF_RQ, h)], tab(T_RC), tab(T_RS)) for h in heads]
        k_r = [_rope_ret(p_cur[:, head_cols(OFF_RK, h)], tab(T_RC), tab(T_RS)) for h in heads]
        qb = [q.astype(BF16) for q in q_r]
        kb = [k.astype(BF16) for k in k_r]
        vb = [p_cur[:, head_cols(OFF_RV, h)].astype(BF16) for h in heads]
        kd = [(k_r[h] * kdec_ref[:, lanes_of(h)]).astype(BF16) for h in heads]
        scores = [_dot_nt(qb[h], kb[h]) for h in heads]
        inter = [_dot(qb[h], s_scr[h].astype(BF16)) for h in heads]
        ktv = [_dot_tn(kd[h], vb[h]) for h in heads]
        project_next_slabs(p_nxt, slabs, SLABS_AFTER_RET_PHASE[0])
        scores = [(scores[h] * dmask_ref[h]).astype(BF16) for h in heads]
        for h in heads:
            s_scr[h] = gt_ref[:, lanes_of(h)] * s_scr[h] + ktv[h]
        project_next_slabs(p_nxt, slabs, SLABS_AFTER_RET_PHASE[1])
        intra = [_dot(scores[h], vb[h]) for h in heads]
        project_next_slabs(p_nxt, slabs, SLABS_AFTER_RET_PHASE[2])
        for h in heads:
            o = intra[h] + inter[h] * qdec_ref[:, lanes_of(h)]
            mu = jnp.mean(o, axis=-1, keepdims=True)
            oc = o - mu
            var = jnp.mean(oc * oc, axis=-1, keepdims=True)
            on = oc * lax.rsqrt(var + GN_EPS) * gnw_ref[:, lanes_of(h)]
            mix_scr[:, lanes_of(h)] = (jax.nn.silu(p_cur[:, head_cols(OFF_RG, h)]) * on).astype(BF16)

        k_new = _rope_swa(p_cur[:, OFF_SK:OFF_SK + KV_W], tab(T_SC), tab(T_S1), tab(T_S2))
        v_new = p_cur[:, OFF_SV:OFF_SV + KV_W]
        kext_scr[SWA_WINDOW:SWA_WINDOW + tt, :] = k_new
        vext_scr[SWA_WINDOW:SWA_WINDOW + tt, :] = v_new
        k_var = _kv_variants(kext_scr[...])
        v_var = _kv_variants(vext_scr[...])
        nk = SWA_WINDOW + CHUNK
        ind = _head_indicator(nk)
        n_chunks = tt // CHUNK
        qs = [_rope_swa(p_cur[:, OFF_SQ + m * LANES:OFF_SQ + (m + 1) * LANES],
                        tab(T_SC), tab(T_S1), tab(T_S2)).astype(BF16)
              for m in range(N_PAIRS)]
        col = lax.broadcasted_iota(jnp.int32, (PAIRS_PER_KV * CHUNK, LANES), 1)
        row = lax.broadcasted_iota(jnp.int32, (PAIRS_PER_KV * CHUNK, 1), 0)
        blocks = [(grp, i) for grp in range(SWA_KV_HEADS) for i in range(n_chunks)]
        rows_of = lambda n: slice(n * PAIRS_PER_KV * CHUNK, (n + 1) * PAIRS_PER_KV * CHUNK)
        for n, (grp, i) in enumerate(blocks):
            r0 = i * CHUNK
            q_blk = jnp.concatenate([qs[grp * PAIRS_PER_KV + a][r0:r0 + CHUNK] for a in range(PAIRS_PER_KV)], axis=0)
            sw_s_scr[rows_of(n), :] = _dot_nt(q_blk, _pair_rows(k_var[2 * grp], k_var[2 * grp + 1], r0, nk))
        for n, (grp, i) in enumerate(blocks):
            sink_e = sinks_ref[2 * grp * PAIRS_PER_KV]
            sink_o = sinks_ref[2 * grp * PAIRS_PER_KV + 1]
            for a in range(1, PAIRS_PER_KV):
                sink_e = jnp.where(row >= a * CHUNK, sinks_ref[2 * (grp * PAIRS_PER_KV + a)], sink_e)
                sink_o = jnp.where(row >= a * CHUNK, sinks_ref[2 * (grp * PAIRS_PER_KV + a) + 1], sink_o)
            valid = None
            if seq_start is not None and i < SWA_WINDOW // CHUNK:
                valid = jnp.logical_or(col >= SWA_WINDOW - i * CHUNK, jnp.logical_not(seq_start))
            e, sink_term = _swa_softmax(sw_s_scr[rows_of(n), :], nk, sink_e, sink_o, valid)
            sw_e_scr[rows_of(n), :] = e
            sw_d_scr[rows_of(n), :] = sink_term
            project_next_slabs(p_nxt, slabs, SLABS_AFTER_SOFTMAX_BLOCK[n])
        for n, (grp, i) in enumerate(blocks):
            r0 = i * CHUNK
            vt = _pair_rows(v_var[2 * grp], v_var[2 * grp + 1], r0, nk)
            o = _swa_out(sw_e_scr[rows_of(n), :], vt, ind, sw_d_scr[rows_of(n), :]).astype(BF16)
            for a in range(PAIRS_PER_KV):
                m = grp * PAIRS_PER_KV + a
                mix_scr[r0:r0 + CHUNK, RET_W + m * LANES:RET_W + (m + 1) * LANES] = o[a * CHUNK:(a + 1) * CHUNK]

        kext_scr[0:SWA_WINDOW, :] = k_new[tt - SWA_WINDOW:, :]
        vext_scr[0:SWA_WINDOW, :] = v_new[tt - SWA_WINDOW:, :]

        x1_ref[0, rows_x, :] = ALPHA * x_ref[0, rows_x, :] + _dot(mix_scr[...], w_out_ref[...])
        lb = tt // LN_BLOCKS
        for r in range(LN_BLOCKS):
            rows_ln = slice(r0x + r * lb, r0x + (r + 1) * lb)
            x1_ref[0, rows_ln, :] = _layer_norm(x1_ref[0, rows_ln, :], ln1w_ref[...], ln1b_ref[...])
            project_next_slabs(p_nxt, slabs, SLABS_AFTER_LN_BLOCK[r])
        assert next(slabs, None) is None
        return k_new, v_new

    for i in range(TILES_PER_STEP):
        p_cur, p_nxt = (pa_scr, pb_scr) if i % 2 == 0 else (pb_scr, pa_scr)
        if i + 1 < TILES_PER_STEP:
            load_next_x = functools.partial(lambda r0: x_ref[0, r0:r0 + tt, :], (i + 1) * tt)
        else:
            load_next_x = lambda: xn_ref[0]
        k_new, v_new = tile(i * tt, p_cur, p_nxt, load_next_x, seq_step == 0 if i == 0 else None)

    @pl.when(seq_step == steps_per_seq - 1)
    def _():
        for grp in range(SWA_KV_HEADS):
            kout_ref[0, grp] = k_new[tt - SWA_WINDOW:, grp * SWA_HD:(grp + 1) * SWA_HD].T
            vout_ref[0, grp] = v_new[tt - SWA_WINDOW:, grp * SWA_HD:(grp + 1) * SWA_HD].T
        sout_ref[0] = s_scr[...]


def _prompt_mixer(x, tab, dmask, qdec, kdec, gt, sinks, w_in, gnw, w_out, ln1w, ln1b, w_up, w_down):
    b, t, d = x.shape
    tt = PROMPT_TILE
    rows_step = TILES_PER_STEP * tt
    sps = t // rows_step
    n_steps = b * sps
    up_rows, down_rows = w_up.shape[0] // MLP_CAST_STEPS, w_down.shape[0] // MLP_CAST_STEPS
    row_slice = lambda g: (jnp.minimum(g, MLP_CAST_STEPS - 1), 0)
    const2 = lambda g: (0, 0)
    const3 = lambda g: (0, 0, 0)

    def next_tile(g):
        gn = jnp.minimum(g + 1, n_steps - 1)
        return (gn // sps, (gn % sps) * TILES_PER_STEP, 0)

    resident = dict(pipeline_mode=pl.Buffered(1))
    return pl.pallas_call(
        functools.partial(_prompt_mixer_kernel, steps_per_seq=sps),
        grid=(n_steps,),
        in_specs=[
            pl.BlockSpec(memory_space=pltpu.SMEM),
            pl.BlockSpec((1, rows_step, d), lambda g: (g // sps, g % sps, 0)),
            pl.BlockSpec((1, tt, d), next_tile),
            pl.BlockSpec((rows_step, N_TABS * LANES), lambda g: (g % sps, 0)),
            pl.BlockSpec((RET_HEADS, tt, tt), const3, **resident),
            pl.BlockSpec((tt, RET_W), const2, **resident),
            pl.BlockSpec((tt, RET_W), const2, **resident),
            pl.BlockSpec((1, RET_W), const2),
            pl.BlockSpec((d, PROJ_W), const2, **resident),
            pl.BlockSpec((1, RET_W), const2),
            pl.BlockSpec((MIX_W, d), const2, **resident),
            pl.BlockSpec((1, d), const2),
            pl.BlockSpec((1, d), const2),
            pl.BlockSpec((up_rows, w_up.shape[1]), row_slice),
            pl.BlockSpec((down_rows, w_down.shape[1]), row_slice),
        ],
        out_specs=[
            pl.BlockSpec((1, rows_step, d), lambda g: (g // sps, g % sps, 0)),
            pl.BlockSpec((1, SWA_KV_HEADS, SWA_HD, SWA_WINDOW), lambda g: (g // sps, 0, 0, 0)),
            pl.BlockSpec((1, SWA_KV_HEADS, SWA_HD, SWA_WINDOW), lambda g: (g // sps, 0, 0, 0)),
            pl.BlockSpec((1, RET_HEADS, RET_DK, RET_DV), lambda g: (g // sps, 0, 0, 0)),
            pl.BlockSpec((up_rows, w_up.shape[1]), row_slice),
            pl.BlockSpec((down_rows, w_down.shape[1]), row_slice),
        ],
        out_shape=[
            jax.ShapeDtypeStruct((b, t, d), F32),
            jax.ShapeDtypeStruct((b, SWA_KV_HEADS, SWA_HD, SWA_WINDOW), F32),
            jax.ShapeDtypeStruct((b, SWA_KV_HEADS, SWA_HD, SWA_WINDOW), F32),
            jax.ShapeDtypeStruct((b, RET_HEADS, RET_DK, RET_DV), F32),
            jax.ShapeDtypeStruct(w_up.shape, BF16),
            jax.ShapeDtypeStruct(w_down.shape, BF16),
        ],
        scratch_shapes=[
            pltpu.VMEM((tt, PROJ_W), F32),
            pltpu.VMEM((tt, PROJ_W), F32),
            pltpu.VMEM((tt, d), BF16),
            pltpu.VMEM((RET_HEADS, RET_DK, RET_DV), F32),
            pltpu.VMEM((SWA_WINDOW + tt, KV_W), F32),
            pltpu.VMEM((SWA_WINDOW + tt, KV_W), F32),
            pltpu.VMEM((tt, MIX_W), BF16),
            pltpu.VMEM((N_PAIRS * tt, 2 * (SWA_WINDOW + CHUNK)), F32),
            pltpu.VMEM((N_PAIRS * tt, 2 * (SWA_WINDOW + CHUNK)), BF16),
            pltpu.VMEM((N_PAIRS * tt, LANES), F32),
        ],
        compiler_params=pltpu.CompilerParams(
            dimension_semantics=("arbitrary",), vmem_limit_bytes=VMEM_LIMIT),
        name="prompt_mixer",
    )(sinks, x, x, tab, dmask, qdec, kdec, gt, w_in, gnw, w_out, ln1w, ln1b, w_up, w_down)


def _sample_mixer_kernel(sinks_ref, x_ref, tab_ref, dmask_ref, qdec_ref, kdec_ref, gt_ref, ck_ref, cv_ref,
                         state_ref, w_in_f32_ref, gnw_ref, w_out_f32_ref, ln1w_ref, ln1b_ref,
                         x1_ref, kout_ref, vout_ref, sout_ref, w_in_ref, w_out_ref,
                         p_scr, mix_scr, *, n_new):
    s_idx = pl.program_id(0)
    last_s = pl.num_programs(0) - 1
    ns = SAMPLE_STREAMS
    t = n_new
    r_step = ns * t

    @pl.when(s_idx == 0)
    def _():
        w_in_ref[...] = w_in_f32_ref[...].astype(BF16)
        w_in_ref[:, OFF_SQ:OFF_SK] = (w_in_f32_ref[:, OFF_SQ:OFF_SK] * SWA_Q_SCALE).astype(BF16)
        w_out_ref[...] = w_out_f32_ref[...].astype(BF16)
        p_scr[...] = _dot(x_ref[...].astype(BF16), w_in_ref[...])

    rows = pl.ds(pl.multiple_of(s_idx * r_step, r_step), r_step)
    of_stream = lambda a, si: a[si * t:(si + 1) * t]
    heads = range(RET_HEADS)
    streams = range(ns)
    groups = range(SWA_KV_HEADS)

    head_cols = lambda off, h: slice(off + h * LANES, off + (h + 1) * LANES)
    lanes_of = lambda h: slice(h * LANES, (h + 1) * LANES)
    q_r = [_rope_ret(p_scr[rows, head_cols(OFF_RQ, h)], _tab(tab_ref, T_RC), _tab(tab_ref, T_RS)) for h in heads]
    k_r = [_rope_ret(p_scr[rows, head_cols(OFF_RK, h)], _tab(tab_ref, T_RC), _tab(tab_ref, T_RS)) for h in heads]
    qb = [q.astype(BF16) for q in q_r]
    kb = [k.astype(BF16) for k in k_r]
    kd = [(k_r[h] * kdec_ref[:, lanes_of(h)]).astype(BF16) for h in heads]
    vb = [p_scr[rows, head_cols(OFF_RV, h)].astype(BF16) for h in heads]
    k_new = _rope_swa(p_scr[rows, OFF_SK:OFF_SK + KV_W], _tab(tab_ref, T_SC), _tab(tab_ref, T_S1), _tab(tab_ref, T_S2))
    v_new = p_scr[rows, OFF_SV:OFF_SV + KV_W]
    for si in streams:
        for grp in groups:
            kout_ref[si, :, grp, :] = of_stream(k_new, si)[:, grp * SWA_HD:(grp + 1) * SWA_HD]
            vout_ref[si, :, grp, :] = of_stream(v_new, si)[:, grp * SWA_HD:(grp + 1) * SWA_HD]
    qs = [_rope_swa(p_scr[rows, head_cols(OFF_SQ, m)],
                    _tab(tab_ref, T_SC), _tab(tab_ref, T_S1), _tab(tab_ref, T_S2)).astype(BF16)
          for m in range(N_PAIRS)]
    nk = ck_ref.shape[-1] + t
    cached = lambda ref, si: jnp.concatenate([ref[si, g].T for g in groups], axis=1)
    k_var = [_kv_variants(jnp.concatenate([cached(ck_ref, si), of_stream(k_new, si)], axis=0)) for si in streams]
    v_var = [_kv_variants(jnp.concatenate([cached(cv_ref, si), of_stream(v_new, si)], axis=0)) for si in streams]

    scores = [_dot_nt(qb[h], kb[h]) for h in heads]
    inter = [[_dot(of_stream(qb[h], si), state_ref[si, h].astype(BF16)) for si in streams] for h in heads]
    ktv = [[_dot_tn(of_stream(kd[h], si), of_stream(vb[h], si)) for si in streams] for h in heads]
    sw_s = [[_dot_nt(jnp.concatenate([of_stream(qs[grp * PAIRS_PER_KV + a], si) for a in range(PAIRS_PER_KV)], axis=0),
                     _pair_rows(k_var[si][2 * grp], k_var[si][2 * grp + 1], 0, nk))
             for grp in groups] for si in streams]

    scores = [(scores[h] * dmask_ref[h]).astype(BF16) for h in heads]
    for h in heads:
        for si in streams:
            sout_ref[si, h] = gt_ref[:, lanes_of(h)] * state_ref[si, h] + ktv[h][si]
    row = lax.broadcasted_iota(jnp.int32, (PAIRS_PER_KV * t, 1), 0)
    sw_e = []
    for si in streams:
        per_group = []
        for grp in groups:
            sink_e = sinks_ref[2 * grp * PAIRS_PER_KV]
            sink_o = sinks_ref[2 * grp * PAIRS_PER_KV + 1]
            for a in range(1, PAIRS_PER_KV):
                sink_e = jnp.where(row >= a * t, sinks_ref[2 * (grp * PAIRS_PER_KV + a)], sink_e)
                sink_o = jnp.where(row >= a * t, sinks_ref[2 * (grp * PAIRS_PER_KV + a) + 1], sink_o)
            per_group.append(_swa_softmax(sw_s[si][grp], nk, sink_e, sink_o, None))
        sw_e.append(per_group)

    intra = [_dot(scores[h], vb[h]) for h in heads]
    ind = _head_indicator(nk)
    sw_o = [[_swa_out(sw_e[si][grp][0], _pair_rows(v_var[si][2 * grp], v_var[si][2 * grp + 1], 0, nk), ind,
                      sw_e[si][grp][1]).astype(BF16)
             for grp in groups] for si in streams]

    for h in heads:
        o = intra[h] + jnp.concatenate(inter[h], axis=0) * qdec_ref[:, lanes_of(h)]
        mu = jnp.mean(o, axis=-1, keepdims=True)
        oc = o - mu
        var = jnp.mean(oc * oc, axis=-1, keepdims=True)
        on = oc * lax.rsqrt(var + GN_EPS) * gnw_ref[:, lanes_of(h)]
        mix_scr[rows, lanes_of(h)] = (jax.nn.silu(p_scr[rows, head_cols(OFF_RG, h)]) * on).astype(BF16)
    for m in range(N_PAIRS):
        grp, a = divmod(m, PAIRS_PER_KV)
        o = jnp.concatenate([sw_o[si][grp][a * t:(a + 1) * t] for si in streams], axis=0)
        mix_scr[rows, RET_W + m * LANES:RET_W + (m + 1) * LANES] = o

    @pl.when(s_idx == last_s)
    def _():
        z = ALPHA * x_ref[...] + _dot(mix_scr[...], w_out_ref[...])
        x1_ref[...] = _layer_norm(z, ln1w_ref[...], ln1b_ref[...])


def _sample_mixer(x, tab, dmask, qdec, kdec, gt, cache_k, cache_v, state, sinks, w_in, gnw, w_out, ln1w, ln1b):
    nb, n_new, d = x.shape
    rows = nb * n_new
    win = cache_k.shape[-1]
    ns = SAMPLE_STREAMS
    r_step = ns * n_new
    const2 = lambda si: (0, 0)
    const3 = lambda si: (0, 0, 0)
    return pl.pallas_call(
        functools.partial(_sample_mixer_kernel, n_new=n_new),
        grid=(nb // ns,),
        in_specs=[
            pl.BlockSpec(memory_space=pltpu.SMEM),
            pl.BlockSpec((rows, d), const2),
            pl.BlockSpec((r_step, N_TABS * LANES), const2),
            pl.BlockSpec((RET_HEADS, r_step, r_step), const3),
            pl.BlockSpec((r_step, RET_W), const2),
            pl.BlockSpec((r_step, RET_W), const2),
            pl.BlockSpec((1, RET_W), const2),
            pl.BlockSpec((ns, SWA_KV_HEADS, SWA_HD, win), lambda si: (si, 0, 0, 0)),
            pl.BlockSpec((ns, SWA_KV_HEADS, SWA_HD, win), lambda si: (si, 0, 0, 0)),
            pl.BlockSpec((ns, RET_HEADS, RET_DK, RET_DV), lambda si: (si, 0, 0, 0)),
            pl.BlockSpec((d, PROJ_W), const2, pipeline_mode=pl.Buffered(1)),
            pl.BlockSpec((1, RET_W), const2),
            pl.BlockSpec((MIX_W, d), const2, pipeline_mode=pl.Buffered(1)),
            pl.BlockSpec((1, d), const2),
            pl.BlockSpec((1, d), const2),
        ],
        out_specs=[
            pl.BlockSpec((rows, d), const2),
            pl.BlockSpec((ns, n_new, SWA_KV_HEADS, SWA_HD), lambda si: (si, 0, 0, 0)),
            pl.BlockSpec((ns, n_new, SWA_KV_HEADS, SWA_HD), lambda si: (si, 0, 0, 0)),
            pl.BlockSpec((ns, RET_HEADS, RET_DK, RET_DV), lambda si: (si, 0, 0, 0)),
            pl.BlockSpec((d, PROJ_W), const2),
            pl.BlockSpec((MIX_W, d), const2),
        ],
        out_shape=[
            jax.ShapeDtypeStruct((rows, d), F32),
            jax.ShapeDtypeStruct((nb, n_new, SWA_KV_HEADS, SWA_HD), F32),
            jax.ShapeDtypeStruct((nb, n_new, SWA_KV_HEADS, SWA_HD), F32),
            jax.ShapeDtypeStruct((nb, RET_HEADS, RET_DK, RET_DV), F32),
            jax.ShapeDtypeStruct((d, PROJ_W), BF16),
            jax.ShapeDtypeStruct((MIX_W, d), BF16),
        ],
        scratch_shapes=[
            pltpu.VMEM((rows, PROJ_W), F32),
            pltpu.VMEM((rows, MIX_W), BF16),
        ],
        compiler_params=pltpu.CompilerParams(
            dimension_semantics=("arbitrary",), vmem_limit_bytes=VMEM_LIMIT),
        name="sample_mixer",
    )(sinks, x.reshape(rows, d), tab, dmask, qdec, kdec, gt, cache_k, cache_v, state, w_in, gnw, w_out, ln1w, ln1b)


def _mlp_kernel(xp_ref, xs_ref, w_up_ref, w_down_ref, lnw_ref, lnb_ref, yp_ref, ys_ref, h_scr, *, prompt_steps):
    i = pl.program_id(0)

    def block(x_ref, y_ref):
        for r0 in range(0, x_ref.shape[0], MLP_UP_ROWS):
            n = min(MLP_UP_ROWS, x_ref.shape[0] - r0)
            h = _dot(x_ref[r0:r0 + n, :].astype(BF16), w_up_ref[...])
            h_scr[0:n, :] = jnp.square(jnp.maximum(h, 0.0)).astype(BF16)
            for r in range(0, n, MLP_LN_ROWS):
                rows = slice(r0 + r, r0 + r + MLP_LN_ROWS)
                z = ALPHA * x_ref[rows, :] + _dot(h_scr[r:r + MLP_LN_ROWS, :], w_down_ref[...])
                y_ref[rows, :] = _layer_norm(z, lnw_ref[...], lnb_ref[...])

    @pl.when(i < prompt_steps)
    def _():
        block(xp_ref, yp_ref)

    @pl.when(i == prompt_steps)
    def _():
        block(xs_ref, ys_ref)


def _mlp(xp, xs, w_up, w_down, lnw, lnb):
    rows, d = xp.shape
    rows_s = xs.shape[0]
    steps = rows // MLP_TILE
    const2 = lambda i: (0, 0)
    prompt_tile = lambda i: (jnp.minimum(i, steps - 1), 0)
    resident = dict(pipeline_mode=pl.Buffered(1))
    return pl.pallas_call(
        functools.partial(_mlp_kernel, prompt_steps=steps),
        grid=(steps + 1,),
        in_specs=[
            pl.BlockSpec((MLP_TILE, d), prompt_tile),
            pl.BlockSpec((rows_s, d), const2, **resident),
            pl.BlockSpec((d, D_FF), const2, **resident),
            pl.BlockSpec((D_FF, d), const2, **resident),
            pl.BlockSpec((1, d), const2),
            pl.BlockSpec((1, d), const2),
        ],
        out_specs=[
            pl.BlockSpec((MLP_TILE, d), prompt_tile),
            pl.BlockSpec((rows_s, d), const2),
        ],
        out_shape=[
            jax.ShapeDtypeStruct((rows, d), F32),
            jax.ShapeDtypeStruct((rows_s, d), F32),
        ],
        scratch_shapes=[pltpu.VMEM((MLP_UP_ROWS, D_FF), BF16)],
        compiler_params=pltpu.CompilerParams(
            dimension_semantics=("arbitrary",), vmem_limit_bytes=VMEM_LIMIT),
        name="mlp",
    )(xp, xs, w_up, w_down, lnw, lnb)


def _rope_tables(pos):
    posf = np.asarray(pos, np.float64)[:, None]
    half_r = RET_DK // 2
    ang = posf * (RET_THETA ** (-np.arange(half_r, dtype=np.float64) / half_r))[None, :]
    c, s = np.cos(ang), np.sin(ang)
    rc = np.concatenate([c, c], axis=-1)
    rs = np.concatenate([-s, s], axis=-1)
    half_s = SWA_ROT_DIM // 2
    ang = posf * (SWA_THETA ** (-np.arange(half_s, dtype=np.float64) / half_s))[None, :]
    c, s = np.cos(ang), np.sin(ang)
    t = posf.shape[0]
    rest = SWA_HD - SWA_ROT_DIM
    z = np.zeros((t, half_s))
    sc = np.tile(np.concatenate([c, c, np.ones((t, rest))], axis=-1), (1, LANES // SWA_HD))
    s1 = np.tile(np.concatenate([z, s, np.zeros((t, rest))], axis=-1), (1, LANES // SWA_HD))
    s2 = np.tile(np.concatenate([-s, z, np.zeros((t, rest))], axis=-1), (1, LANES // SWA_HD))
    return jnp.asarray(np.concatenate([rc, rs, sc, s1, s2], axis=-1).astype(np.float32))


def _decay_tables(t, copies=1):
    k_scale = RET_DK ** -0.5
    lg = np.log1p(-(2.0 ** (-5.0 - np.arange(RET_HEADS, dtype=np.float64))))
    idx = np.arange(t, dtype=np.float64)
    diff = idx[:, None] - idx[None, :]
    dmask = k_scale * np.where(diff >= 0, np.exp(lg[:, None, None] * np.maximum(diff, 0.0)), 0.0)
    qdec = np.repeat(np.exp(lg[None, :] * (idx[:, None] + 1.0)), RET_DV, axis=1)
    kdec = k_scale * np.repeat(np.exp(lg[None, :] * (t - 1.0 - idx[:, None])), RET_DV, axis=1)
    gt = np.repeat(np.exp(lg * t)[None, :], RET_DV, axis=1)
    if copies > 1:
        dmask = np.stack([np.kron(np.eye(copies), m) for m in dmask])
        qdec, kdec = np.tile(qdec, (copies, 1)), np.tile(kdec, (copies, 1))
    return tuple(jnp.asarray(a.astype(np.float32)) for a in (dmask, qdec, kdec, gt))


def kernel(x_prompt, x_sample, cache_swa_k, cache_swa_v, state_ret, w_in, ret_gn_w, swa_sinks,
           w_out, ln1_w, ln1_b, w_up, w_down, ln2_w, ln2_b):
    assert w_in.shape[0] == DEPTH == 1
    b, t, d = x_prompt.shape
    nb, n_new, _ = x_sample.shape
    win = cache_swa_k.shape[2]

    sinks = swa_sinks[0]

    tab_p = _rope_tables(np.arange(t))
    tab_s = _rope_tables(np.tile(PAST_LEN + np.arange(n_new), SAMPLE_STREAMS))
    dec_p = _decay_tables(PROMPT_TILE)
    dec_s = _decay_tables(n_new, SAMPLE_STREAMS)

    x1_s, k_s, v_s, r_s, w_in_b, w_out_b = _sample_mixer(
        x_sample, tab_s, *dec_s, jnp.transpose(cache_swa_k[0], (0, 2, 3, 1)), jnp.transpose(cache_swa_v[0], (0, 2, 3, 1)),
        state_ret[0], sinks, w_in[0], ret_gn_w, w_out[0], ln1_w, ln1_b)
    x1_p, k_p, v_p, r_p, w_up_b, w_down_b = _prompt_mixer(
        x_prompt, tab_p, *dec_p, sinks, w_in_b, ret_gn_w, w_out_b, ln1_w, ln1_b, w_up[0], w_down[0])
    y_p, y_s = _mlp(x1_p.reshape(b * t, d), x1_s, w_up_b, w_down_b, ln2_w, ln2_b)

    return (y_p.reshape(b, t, d), y_s.reshape(nb, n_new, d),
            jnp.transpose(k_p, (0, 3, 1, 2))[None],
            jnp.transpose(v_p, (0, 3, 1, 2))[None],
            r_p.reshape(DEPTH, b, RET_HEADS, RET_DK, RET_DV),
            k_s[None], v_s[None],
            r_s.reshape(DEPTH, nb, RET_HEADS, RET_DK, RET_DV))
```

```python
import functools

import numpy as np
import jax
import jax.numpy as jnp
from jax import lax
from jax.experimental import pallas as pl
from jax.experimental.pallas import tpu as pltpu

D_MODEL = 1024
DEPTH = 1
PAST_LEN = 2048
CHUNK = 64
RET_HEADS = 4
RET_DK = 128
RET_DV = 128
RET_THETA = 10000.0
RET_W = RET_HEADS * RET_DV
SWA_HEADS = 8
SWA_KV_HEADS = 2
SWA_HD = 64
SWA_WINDOW = 128
SWA_ROT_DIM = SWA_HD // 4
SWA_THETA = 500000.0
SWA_W = SWA_HEADS * SWA_HD
KV_W = SWA_KV_HEADS * SWA_HD
MIX_W = RET_W + SWA_W
D_FF = 4 * D_MODEL
OFF_RQ, OFF_RK, OFF_RV, OFF_RG = 0, RET_W, 2 * RET_W, 3 * RET_W
OFF_SQ = 4 * RET_W
OFF_SK = OFF_SQ + SWA_W
OFF_SV = OFF_SK + KV_W
PROJ_W = OFF_SV + KV_W
SWA_Q_SCALE = SWA_HD ** -0.5
assert SWA_Q_SCALE == 0.125
ALPHA = (2.0 * DEPTH) ** 0.25
LN_EPS = 1e-5
GN_EPS = 1e-5
NEG_INF = -1e30

LANES = 128
MXU_COLS = 256
N_PAIRS = SWA_W // LANES
PAIRS_PER_KV = N_PAIRS // SWA_KV_HEADS
T_RC, T_RS, T_SC, T_S1, T_S2 = range(5)
N_TABS = 5

PROMPT_TILE = 256
TILES_PER_STEP = 2
N_SLABS = PROJ_W // MXU_COLS
LN_BLOCKS = 4
SLABS_AFTER_RET_PHASE = (1, 1, 1)
SLABS_AFTER_SOFTMAX_BLOCK = (1, 1, 1, 1, 1, 1, 0, 0)
SLABS_AFTER_LN_BLOCK = (1, 1, 0, 0)
MLP_CAST_STEPS = 8
SAMPLE_STREAMS = 8
MLP_TILE = 1024
MLP_UP_ROWS = 512
MLP_LN_ROWS = 128
VMEM_LIMIT = 56 * 1024 * 1024

F32 = jnp.float32
BF16 = jnp.bfloat16


def _tab(tab_ref, i):
    return tab_ref[:, i * LANES:(i + 1) * LANES]


def _dot(a, b):
    return jnp.dot(a, b, preferred_element_type=F32)


def _dot_nt(a, b):
    return lax.dot_general(a, b, (((1,), (1,)), ((), ())), preferred_element_type=F32)


def _dot_tn(a, b):
    return lax.dot_general(a, b, (((0,), (0,)), ((), ())), preferred_element_type=F32)


def _layer_norm(z, w, b):
    mu = jnp.mean(z, axis=-1, keepdims=True)
    zc = z - mu
    var = jnp.mean(zc * zc, axis=-1, keepdims=True)
    return zc * lax.rsqrt(var + LN_EPS) * w + b


def _rope_ret(x, c, s):
    return x * c + pltpu.roll(x, RET_DK // 2, axis=1) * s


def _rope_swa(x, c, s1, s2):
    half = SWA_ROT_DIM // 2
    return x * c + pltpu.roll(x, half, axis=1) * s1 + pltpu.roll(x, LANES - half, axis=1) * s2


def _kv_variants(a):
    lane = lax.broadcasted_iota(jnp.int32, a.shape, 1)
    lo = lane < SWA_HD
    ar = pltpu.roll(a, SWA_HD, axis=1)
    zero = jnp.zeros_like(a)
    return (jnp.where(lo, a, zero).astype(BF16), jnp.where(lo, zero, ar).astype(BF16),
            jnp.where(lo, ar, zero).astype(BF16), jnp.where(lo, zero, a).astype(BF16))


def _pair_rows(a_e, a_o, r0, nk):
    return jnp.concatenate([a_e[r0:r0 + LANES], a_o[r0:r0 + LANES],
                            a_e[r0 + LANES:r0 + nk], a_o[r0 + LANES:r0 + nk]], axis=0)


def _swa_softmax(s, nk, sink_e, sink_o, valid):
    r = nk - LANES
    s_e, s_o, s_x = s[:, :LANES], s[:, LANES:2 * LANES], s[:, 2 * LANES:]
    if valid is not None:
        s_e = jnp.where(valid, s_e, NEG_INF)
        s_o = jnp.where(valid, s_o, NEG_INF)
    x_is_e = lax.broadcasted_iota(jnp.int32, s_x.shape, 1) < r
    rowmax = lambda a: jnp.max(a, axis=-1, keepdims=True)
    x_e = jnp.where(x_is_e, s_x, -jnp.inf)
    x_o = jnp.where(x_is_e, -jnp.inf, s_x)
    if 2 * r == LANES:
        m_e, m_o = rowmax(jnp.maximum(s_e, x_e)), rowmax(jnp.maximum(s_o, x_o))
    else:
        m_e, m_o = jnp.maximum(rowmax(s_e), rowmax(x_e)), jnp.maximum(rowmax(s_o), rowmax(x_o))
    m_e = jnp.maximum(m_e, sink_e)
    m_o = jnp.maximum(m_o, sink_o)
    e_e = jnp.exp(s_e - m_e)
    e_o = jnp.exp(s_o - m_o)
    e_x = jnp.exp(s_x - jnp.where(x_is_e, m_e, m_o))
    lane = lax.broadcasted_iota(jnp.int32, (s.shape[0], LANES), 1)
    e = jnp.concatenate([e_e, e_o, e_x], axis=1).astype(BF16)
    return e, jnp.where(lane < SWA_HD, jnp.exp(sink_e - m_e), jnp.exp(sink_o - m_o))


def _head_indicator(nk):
    row = lax.broadcasted_iota(jnp.int32, (2 * nk, LANES), 0)
    lane = lax.broadcasted_iota(jnp.int32, (2 * nk, LANES), 1)
    even_row = jnp.logical_or(row < LANES, jnp.logical_and(row >= 2 * LANES, row < LANES + nk))
    return jnp.where(even_row == (lane < SWA_HD), 1.0, 0.0).astype(BF16)


def _swa_out(e, vt, ind, sink_term):
    res = _dot(e, jnp.concatenate([vt, ind], axis=1))
    return res[:, :LANES] / (res[:, LANES:] + sink_term)


def _prompt_mixer_kernel(sinks_ref, x_ref, xn_ref, tab_ref, dmask_ref, qdec_ref, kdec_ref, gt_ref, w_in_ref,
                         gnw_ref, w_out_ref, ln1w_ref, ln1b_ref, w_up_f32_ref, w_down_f32_ref,
                         x1_ref, kout_ref, vout_ref, sout_ref, w_up_ref, w_down_ref,
                         pa_scr, pb_scr, xb_scr, s_scr, kext_scr, vext_scr, mix_scr, sw_s_scr, sw_e_scr, sw_d_scr,
                         *, steps_per_seq):
    tt = PROMPT_TILE
    g = pl.program_id(0)
    seq_step = g % steps_per_seq

    @pl.when(g < MLP_CAST_STEPS)
    def _():
        w_up_ref[...] = w_up_f32_ref[...].astype(BF16)
        w_down_ref[...] = w_down_f32_ref[...].astype(BF16)

    @pl.when(seq_step == 0)
    def _():
        s_scr[...] = jnp.zeros_like(s_scr)
        kext_scr[0:SWA_WINDOW, :] = jnp.zeros((SWA_WINDOW, KV_W), F32)
        vext_scr[0:SWA_WINDOW, :] = jnp.zeros((SWA_WINDOW, KV_W), F32)

    @pl.when(g == 0)
    def _():
        pa_scr[...] = _dot(x_ref[0, 0:tt, :].astype(BF16), w_in_ref[...])

    def project_next_slabs(p_nxt, slabs, count):
        for _ in range(count):
            c = next(slabs)
            cols = slice(c * MXU_COLS, (c + 1) * MXU_COLS)
            p_nxt[:, cols] = _dot(xb_scr[...], w_in_ref[:, cols])

    def tile(r0x, p_cur, p_nxt, load_next_x, seq_start):
        rows_x = slice(r0x, r0x + tt)
        tab = lambda i: tab_ref[rows_x, i * LANES:(i + 1) * LANES]
        xb_scr[...] = load_next_x().astype(BF16)
        slabs = iter(range(N_SLABS))

        heads = range(RET_HEADS)
        lanes_of = lambda h: slice(h * LANES, (h + 1) * LANES)
        head_cols = lambda off, h: slice(off + h * LANES, off + (h + 1) * LANES)
        q_r = [_rope_ret(p_cur[:, head_cols(OFF_RQ, h)], tab(T_RC), tab(T_RS)) for h in heads]
        k_r = [_rope_ret(p_cur[:, head_cols(OFF_RK, h)], tab(T_RC), tab(T_RS)) for h in heads]
        qb = [q.astype(BF16) for q in q_r]
        kb = [k.astype(BF16) for k in k_r]
        vb = [p_cur[:, head_cols(OFF_RV, h)].astype(BF16) for h in heads]
        kd = [(k_r[h] * kdec_ref[:, lanes_of(h)]).astype(BF16) for h in heads]
        scores = [_dot_nt(qb[h], kb[h]) for h in heads]
        inter = [_dot(qb[h], s_scr[h].astype(BF16)) for h in heads]
        ktv = [_dot_tn(kd[h], vb[h]) for h in heads]
        project_next_slabs(p_nxt, slabs, SLABS_AFTER_RET_PHASE[0])
        scores = [(scores[h] * dmask_ref[h]).astype(BF16) for h in heads]
        for h in heads:
            s_scr[h] = gt_ref[:, lanes_of(h)] * s_scr[h] + ktv[h]
        project_next_slabs(p_nxt, slabs, SLABS_AFTER_RET_PHASE[1])
        intra = [_dot(scores[h], vb[h]) for h in heads]
        project_next_slabs(p_nxt, slabs, SLABS_AFTER_RET_PHASE[2])
        for h in heads:
            o = intra[h] + inter[h] * qdec_ref[:, lanes_of(h)]
            mu = jnp.mean(o, axis=-1, keepdims=True)
            oc = o - mu
            var = jnp.mean(oc * oc, axis=-1, keepdims=True)
            on = oc * lax.rsqrt(var + GN_EPS) * gnw_ref[:, lanes_of(h)]
            mix_scr[:, lanes_of(h)] = (jax.nn.silu(p_cur[:, head_cols(OFF_RG, h)]) * on).astype(BF16)

        k_new = _rope_swa(p_cur[:, OFF_SK:OFF_SK + KV_W], tab(T_SC), tab(T_S1), tab(T_S2))
        v_new = p_cur[:, OFF_SV:OFF_SV + KV_W]
        kext_scr[SWA_WINDOW:SWA_WINDOW + tt, :] = k_new
        vext_scr[SWA_WINDOW:SWA_WINDOW + tt, :] = v_new
        k_var = _kv_variants(kext_scr[...])
        nk = SWA_WINDOW + CHUNK
        ind = _head_indicator(nk)
        n_chunks = tt // CHUNK
        rope_q = lambda m, r0: _rope_swa(
            p_cur[r0:r0 + CHUNK, OFF_SQ + m * LANES:OFF_SQ + (m + 1) * LANES],
            tab_ref[r0x + r0:r0x + r0 + CHUNK, T_SC * LANES:(T_SC + 1) * LANES],
            tab_ref[r0x + r0:r0x + r0 + CHUNK, T_S1 * LANES:(T_S1 + 1) * LANES],
            tab_ref[r0x + r0:r0x + r0 + CHUNK, T_S2 * LANES:(T_S2 + 1) * LANES]).astype(BF16)
        col = lax.broadcasted_iota(jnp.int32, (PAIRS_PER_KV * CHUNK, LANES), 1)
        row = lax.broadcasted_iota(jnp.int32, (PAIRS_PER_KV * CHUNK, 1), 0)
        blocks = [(grp, i) for grp in range(SWA_KV_HEADS) for i in range(n_chunks)]
        rows_of = lambda n: slice(n * PAIRS_PER_KV * CHUNK, (n + 1) * PAIRS_PER_KV * CHUNK)
        for n, (grp, i) in enumerate(blocks):
            r0 = i * CHUNK
            q_blk = jnp.concatenate([rope_q(grp * PAIRS_PER_KV + a, r0) for a in range(PAIRS_PER_KV)], axis=0)
            sw_s_scr[rows_of(n), :] = _dot_nt(q_blk, _pair_rows(k_var[2 * grp], k_var[2 * grp + 1], r0, nk))
        for n, (grp, i) in enumerate(blocks):
            sink_e = sinks_ref[2 * grp * PAIRS_PER_KV]
            sink_o = sinks_ref[2 * grp * PAIRS_PER_KV + 1]
            for a in range(1, PAIRS_PER_KV):
                sink_e = jnp.where(row >= a * CHUNK, sinks_ref[2 * (grp * PAIRS_PER_KV + a)], sink_e)
                sink_o = jnp.where(row >= a * CHUNK, sinks_ref[2 * (grp * PAIRS_PER_KV + a) + 1], sink_o)
            valid = None
            if seq_start is not None and i < SWA_WINDOW // CHUNK:
                valid = jnp.logical_or(col >= SWA_WINDOW - i * CHUNK, jnp.logical_not(seq_start))
            e, sink_term = _swa_softmax(sw_s_scr[rows_of(n), :], nk, sink_e, sink_o, valid)
            sw_e_scr[rows_of(n), :] = e
            sw_d_scr[rows_of(n), :] = sink_term
            project_next_slabs(p_nxt, slabs, SLABS_AFTER_SOFTMAX_BLOCK[n])
        v_var = _kv_variants(vext_scr[...])
        for n, (grp, i) in enumerate(blocks):
            r0 = i * CHUNK
            vt = _pair_rows(v_var[2 * grp], v_var[2 * grp + 1], r0, nk)
            o = _swa_out(sw_e_scr[rows_of(n), :], vt, ind, sw_d_scr[rows_of(n), :]).astype(BF16)
            for a in range(PAIRS_PER_KV):
                m = grp * PAIRS_PER_KV + a
                mix_scr[r0:r0 + CHUNK, RET_W + m * LANES:RET_W + (m + 1) * LANES] = o[a * CHUNK:(a + 1) * CHUNK]

        kext_scr[0:SWA_WINDOW, :] = k_new[tt - SWA_WINDOW:, :]
        vext_scr[0:SWA_WINDOW, :] = v_new[tt - SWA_WINDOW:, :]

        x1_ref[0, rows_x, :] = ALPHA * x_ref[0, rows_x, :] + _dot(mix_scr[...], w_out_ref[...])
        lb = tt // LN_BLOCKS
        for r in range(LN_BLOCKS):
            rows_ln = slice(r0x + r * lb, r0x + (r + 1) * lb)
            x1_ref[0, rows_ln, :] = _layer_norm(x1_ref[0, rows_ln, :], ln1w_ref[...], ln1b_ref[...])
            project_next_slabs(p_nxt, slabs, SLABS_AFTER_LN_BLOCK[r])
        assert next(slabs, None) is None
        return k_new, v_new

    for i in range(TILES_PER_STEP):
        p_cur, p_nxt = (pa_scr, pb_scr) if i % 2 == 0 else (pb_scr, pa_scr)
        if i + 1 < TILES_PER_STEP:
            load_next_x = functools.partial(lambda r0: x_ref[0, r0:r0 + tt, :], (i + 1) * tt)
        else:
            load_next_x = lambda: xn_ref[0]
        k_new, v_new = tile(i * tt, p_cur, p_nxt, load_next_x, seq_step == 0 if i == 0 else None)

    @pl.when(seq_step == steps_per_seq - 1)
    def _():
        for grp in range(SWA_KV_HEADS):
            kout_ref[0, grp] = k_new[tt - SWA_WINDOW:, grp * SWA_HD:(grp + 1) * SWA_HD].T
            vout_ref[0, grp] = v_new[tt - SWA_WINDOW:, grp * SWA_HD:(grp + 1) * SWA_HD].T
        sout_ref[0] = s_scr[...]


def _prompt_mixer(x, tab, dmask, qdec, kdec, gt, sinks, w_in, gnw, w_out, ln1w, ln1b, w_up, w_down):
    b, t, d = x.shape
    tt = PROMPT_TILE
    rows_step = TILES_PER_STEP * tt
    sps = t // rows_step
    n_steps = b * sps
    up_rows, down_rows = w_up.shape[0] // MLP_CAST_STEPS, w_down.shape[0] // MLP_CAST_STEPS
    row_slice = lambda g: (jnp.minimum(g, MLP_CAST_STEPS - 1), 0)
    const2 = lambda g: (0, 0)
    const3 = lambda g: (0, 0, 0)

    def next_tile(g):
        gn = jnp.minimum(g + 1, n_steps - 1)
        return (gn // sps, (gn % sps) * TILES_PER_STEP, 0)

    resident = dict(pipeline_mode=pl.Buffered(1))
    return pl.pallas_call(
        functools.partial(_prompt_mixer_kernel, steps_per_seq=sps),
        grid=(n_steps,),
        in_specs=[
            pl.BlockSpec(memory_space=pltpu.SMEM),
            pl.BlockSpec((1, rows_step, d), lambda g: (g // sps, g % sps, 0)),
            pl.BlockSpec((1, tt, d), next_tile),
            pl.BlockSpec((rows_step, N_TABS * LANES), lambda g: (g % sps, 0)),
            pl.BlockSpec((RET_HEADS, tt, tt), const3, **resident),
            pl.BlockSpec((tt, RET_W), const2, **resident),
            pl.BlockSpec((tt, RET_W), const2, **resident),
            pl.BlockSpec((1, RET_W), const2),
            pl.BlockSpec((d, PROJ_W), const2, **resident),
            pl.BlockSpec((1, RET_W), const2),
            pl.BlockSpec((MIX_W, d), const2, **resident),
            pl.BlockSpec((1, d), const2),
            pl.BlockSpec((1, d), const2),
            pl.BlockSpec((up_rows, w_up.shape[1]), row_slice),
            pl.BlockSpec((down_rows, w_down.shape[1]), row_slice),
        ],
        out_specs=[
            pl.BlockSpec((1, rows_step, d), lambda g: (g // sps, g % sps, 0)),
            pl.BlockSpec((1, SWA_KV_HEADS, SWA_HD, SWA_WINDOW), lambda g: (g // sps, 0, 0, 0)),
            pl.BlockSpec((1, SWA_KV_HEADS, SWA_HD, SWA_WINDOW), lambda g: (g // sps, 0, 0, 0)),
            pl.BlockSpec((1, RET_HEADS, RET_DK, RET_DV), lambda g: (g // sps, 0, 0, 0)),
            pl.BlockSpec((up_rows, w_up.shape[1]), row_slice),
            pl.BlockSpec((down_rows, w_down.shape[1]), row_slice),
        ],
        out_shape=[
            jax.ShapeDtypeStruct((b, t, d), F32),
            jax.ShapeDtypeStruct((b, SWA_KV_HEADS, SWA_HD, SWA_WINDOW), F32),
            jax.ShapeDtypeStruct((b, SWA_KV_HEADS, SWA_HD, SWA_WINDOW), F32),
            jax.ShapeDtypeStruct((b, RET_HEADS, RET_DK, RET_DV), F32),
            jax.ShapeDtypeStruct(w_up.shape, BF16),
            jax.ShapeDtypeStruct(w_down.shape, BF16),
        ],
        scratch_shapes=[
            pltpu.VMEM((tt, PROJ_W), F32),
            pltpu.VMEM((tt, PROJ_W), F32),
            pltpu.VMEM((tt, d), BF16),
            pltpu.VMEM((RET_HEADS, RET_DK, RET_DV), F32),
            pltpu.VMEM((SWA_WINDOW + tt, KV_W), F32),
            pltpu.VMEM((SWA_WINDOW + tt, KV_W), F32),
            pltpu.VMEM((tt, MIX_W), BF16),
            pltpu.VMEM((N_PAIRS * tt, 2 * (SWA_WINDOW + CHUNK)), F32),
            pltpu.VMEM((N_PAIRS * tt, 2 * (SWA_WINDOW + CHUNK)), BF16),
            pltpu.VMEM((N_PAIRS * tt, LANES), F32),
        ],
        compiler_params=pltpu.CompilerParams(
            dimension_semantics=("arbitrary",), vmem_limit_bytes=VMEM_LIMIT),
        name="prompt_mixer",
    )(sinks, x, x, tab, dmask, qdec, kdec, gt, w_in, gnw, w_out, ln1w, ln1b, w_up, w_down)


def _sample_mixer_kernel(sinks_ref, x_ref, tab_ref, dmask_ref, qdec_ref, kdec_ref, gt_ref, ck_ref, cv_ref,
                         state_ref, w_in_f32_ref, gnw_ref, w_out_f32_ref, ln1w_ref, ln1b_ref,
                         x1_ref, kout_ref, vout_ref, sout_ref, w_in_ref, w_out_ref,
                         p_scr, mix_scr, *, n_new):
    s_idx = pl.program_id(0)
    last_s = pl.num_programs(0) - 1
    ns = SAMPLE_STREAMS
    t = n_new
    r_step = ns * t

    @pl.when(s_idx == 0)
    def _():
        w_in_ref[...] = w_in_f32_ref[...].astype(BF16)
        w_in_ref[:, OFF_SQ:OFF_SK] = (w_in_f32_ref[:, OFF_SQ:OFF_SK] * SWA_Q_SCALE).astype(BF16)
        w_out_ref[...] = w_out_f32_ref[...].astype(BF16)
        p_scr[...] = _dot(x_ref[...].astype(BF16), w_in_ref[...])

    rows = pl.ds(pl.multiple_of(s_idx * r_step, r_step), r_step)
    of_stream = lambda a, si: a[si * t:(si + 1) * t]
    heads = range(RET_HEADS)
    streams = range(ns)
    groups = range(SWA_KV_HEADS)

    head_cols = lambda off, h: slice(off + h * LANES, off + (h + 1) * LANES)
    lanes_of = lambda h: slice(h * LANES, (h + 1) * LANES)
    q_r = [_rope_ret(p_scr[rows, head_cols(OFF_RQ, h)], _tab(tab_ref, T_RC), _tab(tab_ref, T_RS)) for h in heads]
    k_r = [_rope_ret(p_scr[rows, head_cols(OFF_RK, h)], _tab(tab_ref, T_RC), _tab(tab_ref, T_RS)) for h in heads]
    qb = [q.astype(BF16) for q in q_r]
    kb = [k.astype(BF16) for k in k_r]
    kd = [(k_r[h] * kdec_ref[:, lanes_of(h)]).astype(BF16) for h in heads]
    vb = [p_scr[rows, head_cols(OFF_RV, h)].astype(BF16) for h in heads]
    k_new = _rope_swa(p_scr[rows, OFF_SK:OFF_SK + KV_W], _tab(tab_ref, T_SC), _tab(tab_ref, T_S1), _tab(tab_ref, T_S2))
    v_new = p_scr[rows, OFF_SV:OFF_SV + KV_W]
    for si in streams:
        for grp in groups:
            kout_ref[si, :, grp, :] = of_stream(k_new, si)[:, grp * SWA_HD:(grp + 1) * SWA_HD]
            vout_ref[si, :, grp, :] = of_stream(v_new, si)[:, grp * SWA_HD:(grp + 1) * SWA_HD]
    qs = [_rope_swa(p_scr[rows, head_cols(OFF_SQ, m)],
                    _tab(tab_ref, T_SC), _tab(tab_ref, T_S1), _tab(tab_ref, T_S2)).astype(BF16)
          for m in range(N_PAIRS)]
    nk = ck_ref.shape[-1] + t
    cached = lambda ref, si: jnp.concatenate([ref[si, g].T for g in groups], axis=1)
    k_var = [_kv_variants(jnp.concatenate([cached(ck_ref, si), of_stream(k_new, si)], axis=0)) for si in streams]
    v_var = [_kv_variants(jnp.concatenate([cached(cv_ref, si), of_stream(v_new, si)], axis=0)) for si in streams]

    scores = [_dot_nt(qb[h], kb[h]) for h in heads]
    inter = [[_dot(of_stream(qb[h], si), state_ref[si, h].astype(BF16)) for si in streams] for h in heads]
    ktv = [[_dot_tn(of_stream(kd[h], si), of_stream(vb[h], si)) for si in streams] for h in heads]
    sw_s = [[_dot_nt(jnp.concatenate([of_stream(qs[grp * PAIRS_PER_KV + a], si) for a in range(PAIRS_PER_KV)], axis=0),
                     _pair_rows(k_var[si][2 * grp], k_var[si][2 * grp + 1], 0, nk))
             for grp in groups] for si in streams]

    scores = [(scores[h] * dmask_ref[h]).astype(BF16) for h in heads]
    for h in heads:
        for si in streams:
            sout_ref[si, h] = gt_ref[:, lanes_of(h)] * state_ref[si, h] + ktv[h][si]
    row = lax.broadcasted_iota(jnp.int32, (PAIRS_PER_KV * t, 1), 0)
    sw_e = []
    for si in streams:
        per_group = []
        for grp in groups:
            sink_e = sinks_ref[2 * grp * PAIRS_PER_KV]
            sink_o = sinks_ref[2 * grp * PAIRS_PER_KV + 1]
            for a in range(1, PAIRS_PER_KV):
                sink_e = jnp.where(row >= a * t, sinks_ref[2 * (grp * PAIRS_PER_KV + a)], sink_e)
                sink_o = jnp.where(row >= a * t, sinks_ref[2 * (grp * PAIRS_PER_KV + a) + 1], sink_o)
            per_group.append(_swa_softmax(sw_s[si][grp], nk, sink_e, sink_o, None))
        sw_e.append(per_group)

    intra = [_dot(scores[h], vb[h]) for h in heads]
    ind = _head_indicator(nk)
    sw_o = [[_swa_out(sw_e[si][grp][0], _pair_rows(v_var[si][2 * grp], v_var[si][2 * grp + 1], 0, nk), ind,
                      sw_e[si][grp][1]).astype(BF16)
             for grp in groups] for si in streams]

    for h in heads:
        o = intra[h] + jnp.concatenate(inter[h], axis=0) * qdec_ref[:, lanes_of(h)]
        mu = jnp.mean(o, axis=-1, keepdims=True)
        oc = o - mu
        var = jnp.mean(oc * oc, axis=-1, keepdims=True)
        on = oc * lax.rsqrt(var + GN_EPS) * gnw_ref[:, lanes_of(h)]
        mix_scr[rows, lanes_of(h)] = (jax.nn.silu(p_scr[rows, head_cols(OFF_RG, h)]) * on).astype(BF16)
    for m in range(N_PAIRS):
        grp, a = divmod(m, PAIRS_PER_KV)
        o = jnp.concatenate([sw_o[si][grp][a * t:(a + 1) * t] for si in streams], axis=0)
        mix_scr[rows, RET_W + m * LANES:RET_W + (m + 1) * LANES] = o

    @pl.when(s_idx == last_s)
    def _():
        z = ALPHA * x_ref[...] + _dot(mix_scr[...], w_out_ref[...])
        x1_ref[...] = _layer_norm(z, ln1w_ref[...], ln1b_ref[...])


def _sample_mixer(x, tab, dmask, qdec, kdec, gt, cache_k, cache_v, state, sinks, w_in, gnw, w_out, ln1w, ln1b):
    nb, n_new, d = x.shape
    rows = nb * n_new
    win = cache_k.shape[-1]
    ns = SAMPLE_STREAMS
    r_step = ns * n_new
    const2 = lambda si: (0, 0)
    const3 = lambda si: (0, 0, 0)
    return pl.pallas_call(
        functools.partial(_sample_mixer_kernel, n_new=n_new),
        grid=(nb // ns,),
        in_specs=[
            pl.BlockSpec(memory_space=pltpu.SMEM),
            pl.BlockSpec((rows, d), const2),
            pl.BlockSpec((r_step, N_TABS * LANES), const2),
            pl.BlockSpec((RET_HEADS, r_step, r_step), const3),
            pl.BlockSpec((r_step, RET_W), const2),
            pl.BlockSpec((r_step, RET_W), const2),
            pl.BlockSpec((1, RET_W), const2),
            pl.BlockSpec((ns, SWA_KV_HEADS, SWA_HD, win), lambda si: (si, 0, 0, 0)),
            pl.BlockSpec((ns, SWA_KV_HEADS, SWA_HD, win), lambda si: (si, 0, 0, 0)),
            pl.BlockSpec((ns, RET_HEADS, RET_DK, RET_DV), lambda si: (si, 0, 0, 0)),
            pl.BlockSpec((d, PROJ_W), const2, pipeline_mode=pl.Buffered(1)),
            pl.BlockSpec((1, RET_W), const2),
            pl.BlockSpec((MIX_W, d), const2, pipeline_mode=pl.Buffered(1)),
            pl.BlockSpec((1, d), const2),
            pl.BlockSpec((1, d), const2),
        ],
        out_specs=[
            pl.BlockSpec((rows, d), const2),
            pl.BlockSpec((ns, n_new, SWA_KV_HEADS, SWA_HD), lambda si: (si, 0, 0, 0)),
            pl.BlockSpec((ns, n_new, SWA_KV_HEADS, SWA_HD), lambda si: (si, 0, 0, 0)),
            pl.BlockSpec((ns, RET_HEADS, RET_DK, RET_DV), lambda si: (si, 0, 0, 0)),
            pl.BlockSpec((d, PROJ_W), const2),
            pl.BlockSpec((MIX_W, d), const2),
        ],
        out_shape=[
            jax.ShapeDtypeStruct((rows, d), F32),
            jax.ShapeDtypeStruct((nb, n_new, SWA_KV_HEADS, SWA_HD), F32),
            jax.ShapeDtypeStruct((nb, n_new, SWA_KV_HEADS, SWA_HD), F32),
            jax.ShapeDtypeStruct((nb, RET_HEADS, RET_DK, RET_DV), F32),
            jax.ShapeDtypeStruct((d, PROJ_W), BF16),
            jax.ShapeDtypeStruct((MIX_W, d), BF16),
        ],
        scratch_shapes=[
            pltpu.VMEM((rows, PROJ_W), F32),
            pltpu.VMEM((rows, MIX_W), BF16),
        ],
        compiler_params=pltpu.CompilerParams(
            dimension_semantics=("arbitrary",), vmem_limit_bytes=VMEM_LIMIT),
        name="sample_mixer",
    )(sinks, x.reshape(rows, d), tab, dmask, qdec, kdec, gt, cache_k, cache_v, state, w_in, gnw, w_out, ln1w, ln1b)


def _mlp_kernel(xp_ref, xs_ref, w_up_ref, w_down_ref, lnw_ref, lnb_ref, yp_ref, ys_ref, h_scr, *, prompt_steps):
    i = pl.program_id(0)

    def block(x_ref, y_ref):
        for r0 in range(0, x_ref.shape[0], MLP_UP_ROWS):
            n = min(MLP_UP_ROWS, x_ref.shape[0] - r0)
            h = _dot(x_ref[r0:r0 + n, :].astype(BF16), w_up_ref[...])
            h_scr[0:n, :] = jnp.square(jnp.maximum(h, 0.0)).astype(BF16)
            for r in range(0, n, MLP_LN_ROWS):
                rows = slice(r0 + r, r0 + r + MLP_LN_ROWS)
                z = ALPHA * x_ref[rows, :] + _dot(h_scr[r:r + MLP_LN_ROWS, :], w_down_ref[...])
                y_ref[rows, :] = _layer_norm(z, lnw_ref[...], lnb_ref[...])

    @pl.when(i < prompt_steps)
    def _():
        block(xp_ref, yp_ref)

    @pl.when(i == prompt_steps)
    def _():
        block(xs_ref, ys_ref)


def _mlp(xp, xs, w_up, w_down, lnw, lnb):
    rows, d = xp.shape
    rows_s = xs.shape[0]
    steps = rows // MLP_TILE
    const2 = lambda i: (0, 0)
    prompt_tile = lambda i: (jnp.minimum(i, steps - 1), 0)
    resident = dict(pipeline_mode=pl.Buffered(1))
    return pl.pallas_call(
        functools.partial(_mlp_kernel, prompt_steps=steps),
        grid=(steps + 1,),
        in_specs=[
            pl.BlockSpec((MLP_TILE, d), prompt_tile),
            pl.BlockSpec((rows_s, d), const2, **resident),
            pl.BlockSpec((d, D_FF), const2, **resident),
            pl.BlockSpec((D_FF, d), const2, **resident),
            pl.BlockSpec((1, d), const2),
            pl.BlockSpec((1, d), const2),
        ],
        out_specs=[
            pl.BlockSpec((MLP_TILE, d), prompt_tile),
            pl.BlockSpec((rows_s, d), const2),
        ],
        out_shape=[
            jax.ShapeDtypeStruct((rows, d), F32),
            jax.ShapeDtypeStruct((rows_s, d), F32),
        ],
        scratch_shapes=[pltpu.VMEM((MLP_UP_ROWS, D_FF), BF16)],
        compiler_params=pltpu.CompilerParams(
            dimension_semantics=("arbitrary",), vmem_limit_bytes=VMEM_LIMIT),
        name="mlp",
    )(xp, xs, w_up, w_down, lnw, lnb)


def _rope_tables(pos):
    posf = np.asarray(pos, np.float64)[:, None]
    half_r = RET_DK // 2
    ang = posf * (RET_THETA ** (-np.arange(half_r, dtype=np.float64) / half_r))[None, :]
    c, s = np.cos(ang), np.sin(ang)
    rc = np.concatenate([c, c], axis=-1)
    rs = np.concatenate([-s, s], axis=-1)
    half_s = SWA_ROT_DIM // 2
    ang = posf * (SWA_THETA ** (-np.arange(half_s, dtype=np.float64) / half_s))[None, :]
    c, s = np.cos(ang), np.sin(ang)
    t = posf.shape[0]
    rest = SWA_HD - SWA_ROT_DIM
    z = np.zeros((t, half_s))
    sc = np.tile(np.concatenate([c, c, np.ones((t, rest))], axis=-1), (1, LANES // SWA_HD))
    s1 = np.tile(np.concatenate([z, s, np.zeros((t, rest))], axis=-1), (1, LANES // SWA_HD))
    s2 = np.tile(np.concatenate([-s, z, np.zeros((t, rest))], axis=-1), (1, LANES // SWA_HD))
    return jnp.asarray(np.concatenate([rc, rs, sc, s1, s2], axis=-1).astype(np.float32))


def _decay_tables(t, copies=1):
    k_scale = RET_DK ** -0.5
    lg = np.log1p(-(2.0 ** (-5.0 - np.arange(RET_HEADS, dtype=np.float64))))
    idx = np.arange(t, dtype=np.float64)
    diff = idx[:, None] - idx[None, :]
    dmask = k_scale * np.where(diff >= 0, np.exp(lg[:, None, None] * np.maximum(diff, 0.0)), 0.0)
    qdec = np.repeat(np.exp(lg[None, :] * (idx[:, None] + 1.0)), RET_DV, axis=1)
    kdec = k_scale * np.repeat(np.exp(lg[None, :] * (t - 1.0 - idx[:, None])), RET_DV, axis=1)
    gt = np.repeat(np.exp(lg * t)[None, :], RET_DV, axis=1)
    if copies > 1:
        dmask = np.stack([np.kron(np.eye(copies), m) for m in dmask])
        qdec, kdec = np.tile(qdec, (copies, 1)), np.tile(kdec, (copies, 1))
    return tuple(jnp.asarray(a.astype(np.float32)) for a in (dmask, qdec, kdec, gt))


def kernel(x_prompt, x_sample, cache_swa_k, cache_swa_v, state_ret, w_in, ret_gn_w, swa_sinks,
           w_out, ln1_w, ln1_b, w_up, w_down, ln2_w, ln2_b):
    assert w_in.shape[0] == DEPTH == 1
    b, t, d = x_prompt.shape
    nb, n_new, _ = x_sample.shape
    win = cache_swa_k.shape[2]

    sinks = swa_sinks[0]

    tab_p = _rope_tables(np.arange(t))
    tab_s = _rope_tables(np.tile(PAST_LEN + np.arange(n_new), SAMPLE_STREAMS))
    dec_p = _decay_tables(PROMPT_TILE)
    dec_s = _decay_tables(n_new, SAMPLE_STREAMS)

    x1_s, k_s, v_s, r_s, w_in_b, w_out_b = _sample_mixer(
        x_sample, tab_s, *dec_s, jnp.transpose(cache_swa_k[0], (0, 2, 3, 1)), jnp.transpose(cache_swa_v[0], (0, 2, 3, 1)),
        state_ret[0], sinks, w_in[0], ret_gn_w, w_out[0], ln1_w, ln1_b)
    x1_p, k_p, v_p, r_p, w_up_b, w_down_b = _prompt_mixer(
        x_prompt, tab_p, *dec_p, sinks, w_in_b, ret_gn_w, w_out_b, ln1_w, ln1_b, w_up[0], w_down[0])
    y_p, y_s = _mlp(x1_p.reshape(b * t, d), x1_s, w_up_b, w_down_b, ln2_w, ln2_b)

    return (y_p.reshape(b, t, d), y_s.reshape(nb, n_new, d),
            jnp.transpose(k_p, (0, 3, 1, 2))[None],
            jnp.transpose(v_p, (0, 3, 1, 2))[None],
            r_p.reshape(DEPTH, b, RET_HEADS, RET_DK, RET_DV),
            k_s[None], v_s[None],
            r_s.reshape(DEPTH, nb, RET_HEADS, RET_DK, RET_DV))
```

```python
import functools

import numpy as np
import jax
import jax.numpy as jnp
from jax import lax
from jax.experimental import pallas as pl
from jax.experimental.pallas import tpu as pltpu

D_MODEL = 1024
DEPTH = 1
PAST_LEN = 2048
CHUNK = 64
RET_HEADS = 4
RET_DK = 128
RET_DV = 128
RET_THETA = 10000.0
RET_W = RET_HEADS * RET_DV
SWA_HEADS = 8
SWA_KV_HEADS = 2
SWA_HD = 64
SWA_WINDOW = 128
SWA_ROT_DIM = SWA_HD // 4
SWA_THETA = 500000.0
SWA_W = SWA_HEADS * SWA_HD
KV_W = SWA_KV_HEADS * SWA_HD
MIX_W = RET_W + SWA_W
D_FF = 4 * D_MODEL
OFF_RQ, OFF_RK, OFF_RV, OFF_RG = 0, RET_W, 2 * RET_W, 3 * RET_W
OFF_SQ = 4 * RET_W
OFF_SK = OFF_SQ + SWA_W
OFF_SV = OFF_SK + KV_W
PROJ_W = OFF_SV + KV_W
SWA_Q_SCALE = SWA_HD ** -0.5
assert SWA_Q_SCALE == 0.125
ALPHA = (2.0 * DEPTH) ** 0.25
LN_EPS = 1e-5
GN_EPS = 1e-5
NEG_INF = -1e30

LANES = 128
MXU_COLS = 256
N_PAIRS = SWA_W // LANES
PAIRS_PER_KV = N_PAIRS // SWA_KV_HEADS
T_RC, T_RS, T_SC, T_S1, T_S2 = range(5)
N_TABS = 5

PROMPT_TILE = 256
TILES_PER_STEP = 2
N_SLABS = PROJ_W // MXU_COLS
LN_BLOCKS = 4
SLABS_AFTER_RET_PHASE = (1, 1, 1)
SLABS_AFTER_SOFTMAX_BLOCK = (1, 1, 1, 1, 1, 1, 0, 0)
SLABS_AFTER_LN_BLOCK = (1, 1, 0, 0)
MLP_CAST_STEPS = 8
SAMPLE_STREAMS = 8
MLP_TILE = 1024
MLP_UP_ROWS = 512
MLP_LN_ROWS = 256
VMEM_LIMIT = 56 * 1024 * 1024

F32 = jnp.float32
BF16 = jnp.bfloat16


def _tab(tab_ref, i):
    return tab_ref[:, i * LANES:(i + 1) * LANES]


def _dot(a, b):
    return jnp.dot(a, b, preferred_element_type=F32)


def _dot_nt(a, b):
    return lax.dot_general(a, b, (((1,), (1,)), ((), ())), preferred_element_type=F32)


def _dot_tn(a, b):
    return lax.dot_general(a, b, (((0,), (0,)), ((), ())), preferred_element_type=F32)


def _layer_norm(z, w, b):
    mu = jnp.mean(z, axis=-1, keepdims=True)
    zc = z - mu
    var = jnp.mean(zc * zc, axis=-1, keepdims=True)
    return zc * lax.rsqrt(var + LN_EPS) * w + b


def _rope_ret(x, c, s):
    return x * c + pltpu.roll(x, RET_DK // 2, axis=1) * s


def _rope_swa(x, c, s1, s2):
    half = SWA_ROT_DIM // 2
    return x * c + pltpu.roll(x, half, axis=1) * s1 + pltpu.roll(x, LANES - half, axis=1) * s2


def _kv_variants(a):
    lane = lax.broadcasted_iota(jnp.int32, a.shape, 1)
    lo = lane < SWA_HD
    ar = pltpu.roll(a, SWA_HD, axis=1)
    zero = jnp.zeros_like(a)
    return (jnp.where(lo, a, zero).astype(BF16), jnp.where(lo, zero, ar).astype(BF16),
            jnp.where(lo, ar, zero).astype(BF16), jnp.where(lo, zero, a).astype(BF16))


def _pair_rows(a_e, a_o, r0, nk):
    return jnp.concatenate([a_e[r0:r0 + LANES], a_o[r0:r0 + LANES],
                            a_e[r0 + LANES:r0 + nk], a_o[r0 + LANES:r0 + nk]], axis=0)


def _swa_softmax(s, nk, sink_e, sink_o, valid):
    r = nk - LANES
    s_e, s_o, s_x = s[:, :LANES], s[:, LANES:2 * LANES], s[:, 2 * LANES:]
    if valid is not None:
        s_e = jnp.where(valid, s_e, NEG_INF)
        s_o = jnp.where(valid, s_o, NEG_INF)
    x_is_e = lax.broadcasted_iota(jnp.int32, s_x.shape, 1) < r
    rowmax = lambda a: jnp.max(a, axis=-1, keepdims=True)
    x_e = jnp.where(x_is_e, s_x, -jnp.inf)
    x_o = jnp.where(x_is_e, -jnp.inf, s_x)
    if 2 * r == LANES:
        m_e, m_o = rowmax(jnp.maximum(s_e, x_e)), rowmax(jnp.maximum(s_o, x_o))
    else:
        m_e, m_o = jnp.maximum(rowmax(s_e), rowmax(x_e)), jnp.maximum(rowmax(s_o), rowmax(x_o))
    m_e = jnp.maximum(m_e, sink_e)
    m_o = jnp.maximum(m_o, sink_o)
    e_e = jnp.exp(s_e - m_e)
    e_o = jnp.exp(s_o - m_o)
    e_x = jnp.exp(s_x - jnp.where(x_is_e, m_e, m_o))
    lane = lax.broadcasted_iota(jnp.int32, (s.shape[0], LANES), 1)
    e = jnp.concatenate([e_e, e_o, e_x], axis=1).astype(BF16)
    return e, jnp.where(lane < SWA_HD, jnp.exp(sink_e - m_e), jnp.exp(sink_o - m_o))


def _head_indicator(nk):
    row = lax.broadcasted_iota(jnp.int32, (2 * nk, LANES), 0)
    lane = lax.broadcasted_iota(jnp.int32, (2 * nk, LANES), 1)
    even_row = jnp.logical_or(row < LANES, jnp.logical_and(row >= 2 * LANES, row < LANES + nk))
    return jnp.where(even_row == (lane < SWA_HD), 1.0, 0.0).astype(BF16)


def _swa_out(e, vt, ind, sink_term):
    res = _dot(e, jnp.concatenate([vt, ind], axis=1))
    return res[:, :LANES] / (res[:, LANES:] + sink_term)


def _prompt_mixer_kernel(sinks_ref, x_ref, xn_ref, tab_ref, dmask_ref, qdec_ref, kdec_ref, gt_ref, w_in_ref,
                         gnw_ref, w_out_ref, ln1w_ref, ln1b_ref, w_up_f32_ref, w_down_f32_ref,
                         x1_ref, kout_ref, vout_ref, sout_ref, w_up_ref, w_down_ref,
                         pa_scr, pb_scr, xb_scr, s_scr, kext_scr, vext_scr, mix_scr, sw_s_scr, sw_e_scr, sw_d_scr,
                         *, steps_per_seq):
    tt = PROMPT_TILE
    g = pl.program_id(0)
    seq_step = g % steps_per_seq

    @pl.when(g < MLP_CAST_STEPS)
    def _():
        w_up_ref[...] = w_up_f32_ref[...].astype(BF16)
        w_down_ref[...] = w_down_f32_ref[...].astype(BF16)

    @pl.when(seq_step == 0)
    def _():
        s_scr[...] = jnp.zeros_like(s_scr)
        kext_scr[0:SWA_WINDOW, :] = jnp.zeros((SWA_WINDOW, KV_W), F32)
        vext_scr[0:SWA_WINDOW, :] = jnp.zeros((SWA_WINDOW, KV_W), F32)

    @pl.when(g == 0)
    def _():
        pa_scr[...] = _dot(x_ref[0, 0:tt, :].astype(BF16), w_in_ref[...])

    def project_next_slabs(p_nxt, slabs, count):
        for _ in range(count):
            c = next(slabs)
            cols = slice(c * MXU_COLS, (c + 1) * MXU_COLS)
            p_nxt[:, cols] = _dot(xb_scr[...], w_in_ref[:, cols])

    def tile(r0x, p_cur, p_nxt, load_next_x, seq_start):
        rows_x = slice(r0x, r0x + tt)
        tab = lambda i: tab_ref[rows_x, i * LANES:(i + 1) * LANES]
        xb_scr[...] = load_next_x().astype(BF16)
        slabs = iter(range(N_SLABS))

        heads = range(RET_HEADS)
        lanes_of = lambda h: slice(h * LANES, (h + 1) * LANES)
        head_cols = lambda off, h: slice(off + h * LANES, off + (h + 1) * LANES)
        q_r = [_rope_ret(p_cur[:, head_cols(OFF_RQ, h)], tab(T_RC), tab(T_RS)) for h in heads]
        k_r = [_rope_ret(p_cur[:, head_cols(OFF_RK, h)], tab(T_RC), tab(T_RS)) for h in heads]
        qb = [q.astype(BF16) for q in q_r]
        kb = [k.astype(BF16) for k in k_r]
        vb = [p_cur[:, head_cols(OFF_RV, h)].astype(BF16) for h in heads]
        kd = [(k_r[h] * kdec_ref[:, lanes_of(h)]).astype(BF16) for h in heads]
        scores = [_dot_nt(qb[h], kb[h]) for h in heads]
        inter = [_dot(qb[h], s_scr[h].astype(BF16)) for h in heads]
        ktv = [_dot_tn(kd[h], vb[h]) for h in heads]
        project_next_slabs(p_nxt, slabs, SLABS_AFTER_RET_PHASE[0])
        scores = [(scores[h] * dmask_ref[h]).astype(BF16) for h in heads]
        for h in heads:
            s_scr[h] = gt_ref[:, lanes_of(h)] * s_scr[h] + ktv[h]
        project_next_slabs(p_nxt, slabs, SLABS_AFTER_RET_PHASE[1])
        intra = [_dot(scores[h], vb[h]) for h in heads]
        project_next_slabs(p_nxt, slabs, SLABS_AFTER_RET_PHASE[2])
        for h in heads:
            o = intra[h] + inter[h] * qdec_ref[:, lanes_of(h)]
            mu = jnp.mean(o, axis=-1, keepdims=True)
            oc = o - mu
            var = jnp.mean(oc * oc, axis=-1, keepdims=True)
            on = oc * lax.rsqrt(var + GN_EPS) * gnw_ref[:, lanes_of(h)]
            mix_scr[:, lanes_of(h)] = (jax.nn.silu(p_cur[:, head_cols(OFF_RG, h)]) * on).astype(BF16)

        k_new = _rope_swa(p_cur[:, OFF_SK:OFF_SK + KV_W], tab(T_SC), tab(T_S1), tab(T_S2))
        v_new = p_cur[:, OFF_SV:OFF_SV + KV_W]
        kext_scr[SWA_WINDOW:SWA_WINDOW + tt, :] = k_new
        vext_scr[SWA_WINDOW:SWA_WINDOW + tt, :] = v_new
        k_var = _kv_variants(kext_scr[...])
        v_var = _kv_variants(vext_scr[...])
        nk = SWA_WINDOW + CHUNK
        ind = _head_indicator(nk)
        n_chunks = tt // CHUNK
        qs = [_rope_swa(p_cur[:, OFF_SQ + m * LANES:OFF_SQ + (m + 1) * LANES],
                        tab(T_SC), tab(T_S1), tab(T_S2)).astype(BF16)
              for m in range(N_PAIRS)]
        col = lax.broadcasted_iota(jnp.int32, (PAIRS_PER_KV * CHUNK, LANES), 1)
        row = lax.broadcasted_iota(jnp.int32, (PAIRS_PER_KV * CHUNK, 1), 0)
        blocks = [(grp, i) for grp in range(SWA_KV_HEADS) for i in range(n_chunks)]
        rows_of = lambda n: slice(n * PAIRS_PER_KV * CHUNK, (n + 1) * PAIRS_PER_KV * CHUNK)
        for n, (grp, i) in enumerate(blocks):
            r0 = i * CHUNK
            q_blk = jnp.concatenate([qs[grp * PAIRS_PER_KV + a][r0:r0 + CHUNK] for a in range(PAIRS_PER_KV)], axis=0)
            sw_s_scr[rows_of(n), :] = _dot_nt(q_blk, _pair_rows(k_var[2 * grp], k_var[2 * grp + 1], r0, nk))
        for n, (grp, i) in enumerate(blocks):
            sink_e = sinks_ref[2 * grp * PAIRS_PER_KV]
            sink_o = sinks_ref[2 * grp * PAIRS_PER_KV + 1]
            for a in range(1, PAIRS_PER_KV):
                sink_e = jnp.where(row >= a * CHUNK, sinks_ref[2 * (grp * PAIRS_PER_KV + a)], sink_e)
                sink_o = jnp.where(row >= a * CHUNK, sinks_ref[2 * (grp * PAIRS_PER_KV + a) + 1], sink_o)
            valid = None
            if seq_start is not None and i < SWA_WINDOW // CHUNK:
                valid = jnp.logical_or(col >= SWA_WINDOW - i * CHUNK, jnp.logical_not(seq_start))
            e, sink_term = _swa_softmax(sw_s_scr[rows_of(n), :], nk, sink_e, sink_o, valid)
            sw_e_scr[rows_of(n), :] = e
            sw_d_scr[rows_of(n), :] = sink_term
            project_next_slabs(p_nxt, slabs, SLABS_AFTER_SOFTMAX_BLOCK[n])
        for n, (grp, i) in enumerate(blocks):
            r0 = i * CHUNK
            vt = _pair_rows(v_var[2 * grp], v_var[2 * grp + 1], r0, nk)
            o = _swa_out(sw_e_scr[rows_of(n), :], vt, ind, sw_d_scr[rows_of(n), :]).astype(BF16)
            for a in range(PAIRS_PER_KV):
                m = grp * PAIRS_PER_KV + a
                mix_scr[r0:r0 + CHUNK, RET_W + m * LANES:RET_W + (m + 1) * LANES] = o[a * CHUNK:(a + 1) * CHUNK]

        kext_scr[0:SWA_WINDOW, :] = k_new[tt - SWA_WINDOW:, :]
        vext_scr[0:SWA_WINDOW, :] = v_new[tt - SWA_WINDOW:, :]

        x1_ref[0, rows_x, :] = ALPHA * x_ref[0, rows_x, :] + _dot(mix_scr[...], w_out_ref[...])
        lb = tt // LN_BLOCKS
        for r in range(LN_BLOCKS):
            rows_ln = slice(r0x + r * lb, r0x + (r + 1) * lb)
            x1_ref[0, rows_ln, :] = _layer_norm(x1_ref[0, rows_ln, :], ln1w_ref[...], ln1b_ref[...])
            project_next_slabs(p_nxt, slabs, SLABS_AFTER_LN_BLOCK[r])
        assert next(slabs, None) is None
        return k_new, v_new

    for i in range(TILES_PER_STEP):
        p_cur, p_nxt = (pa_scr, pb_scr) if i % 2 == 0 else (pb_scr, pa_scr)
        if i + 1 < TILES_PER_STEP:
            load_next_x = functools.partial(lambda r0: x_ref[0, r0:r0 + tt, :], (i + 1) * tt)
        else:
            load_next_x = lambda: xn_ref[0]
        k_new, v_new = tile(i * tt, p_cur, p_nxt, load_next_x, seq_step == 0 if i == 0 else None)

    @pl.when(seq_step == steps_per_seq - 1)
    def _():
        for grp in range(SWA_KV_HEADS):
            kout_ref[0, grp] = k_new[tt - SWA_WINDOW:, grp * SWA_HD:(grp + 1) * SWA_HD].T
            vout_ref[0, grp] = v_new[tt - SWA_WINDOW:, grp * SWA_HD:(grp + 1) * SWA_HD].T
        sout_ref[0] = s_scr[...]


def _prompt_mixer(x, tab, dmask, qdec, kdec, gt, sinks, w_in, gnw, w_out, ln1w, ln1b, w_up, w_down):
    b, t, d = x.shape
    tt = PROMPT_TILE
    rows_step = TILES_PER_STEP * tt
    sps = t // rows_step
    n_steps = b * sps
    up_rows, down_rows = w_up.shape[0] // MLP_CAST_STEPS, w_down.shape[0] // MLP_CAST_STEPS
    row_slice = lambda g: (jnp.minimum(g, MLP_CAST_STEPS - 1), 0)
    const2 = lambda g: (0, 0)
    const3 = lambda g: (0, 0, 0)

    def next_tile(g):
        gn = jnp.minimum(g + 1, n_steps - 1)
        return (gn // sps, (gn % sps) * TILES_PER_STEP, 0)

    resident = dict(pipeline_mode=pl.Buffered(1))
    return pl.pallas_call(
        functools.partial(_prompt_mixer_kernel, steps_per_seq=sps),
        grid=(n_steps,),
        in_specs=[
            pl.BlockSpec(memory_space=pltpu.SMEM),
            pl.BlockSpec((1, rows_step, d), lambda g: (g // sps, g % sps, 0)),
            pl.BlockSpec((1, tt, d), next_tile),
            pl.BlockSpec((rows_step, N_TABS * LANES), lambda g: (g % sps, 0)),
            pl.BlockSpec((RET_HEADS, tt, tt), const3, **resident),
            pl.BlockSpec((tt, RET_W), const2, **resident),
            pl.BlockSpec((tt, RET_W), const2, **resident),
            pl.BlockSpec((1, RET_W), const2),
            pl.BlockSpec((d, PROJ_W), const2, **resident),
            pl.BlockSpec((1, RET_W), const2),
            pl.BlockSpec((MIX_W, d), const2, **resident),
            pl.BlockSpec((1, d), const2),
            pl.BlockSpec((1, d), const2),
            pl.BlockSpec((up_rows, w_up.shape[1]), row_slice),
            pl.BlockSpec((down_rows, w_down.shape[1]), row_slice),
        ],
        out_specs=[
            pl.BlockSpec((1, rows_step, d), lambda g: (g // sps, g % sps, 0)),
            pl.BlockSpec((1, SWA_KV_HEADS, SWA_HD, SWA_WINDOW), lambda g: (g // sps, 0, 0, 0)),
            pl.BlockSpec((1, SWA_KV_HEADS, SWA_HD, SWA_WINDOW), lambda g: (g // sps, 0, 0, 0)),
            pl.BlockSpec((1, RET_HEADS, RET_DK, RET_DV), lambda g: (g // sps, 0, 0, 0)),
            pl.BlockSpec((up_rows, w_up.shape[1]), row_slice),
            pl.BlockSpec((down_rows, w_down.shape[1]), row_slice),
        ],
        out_shape=[
            jax.ShapeDtypeStruct((b, t, d), F32),
            jax.ShapeDtypeStruct((b, SWA_KV_HEADS, SWA_HD, SWA_WINDOW), F32),
            jax.ShapeDtypeStruct((b, SWA_KV_HEADS, SWA_HD, SWA_WINDOW), F32),
            jax.ShapeDtypeStruct((b, RET_HEADS, RET_DK, RET_DV), F32),
            jax.ShapeDtypeStruct(w_up.shape, BF16),
            jax.ShapeDtypeStruct(w_down.shape, BF16),
        ],
        scratch_shapes=[
            pltpu.VMEM((tt, PROJ_W), F32),
            pltpu.VMEM((tt, PROJ_W), F32),
            pltpu.VMEM((tt, d), BF16),
            pltpu.VMEM((RET_HEADS, RET_DK, RET_DV), F32),
            pltpu.VMEM((SWA_WINDOW + tt, KV_W), F32),
            pltpu.VMEM((SWA_WINDOW + tt, KV_W), F32),
            pltpu.VMEM((tt, MIX_W), BF16),
            pltpu.VMEM((N_PAIRS * tt, 2 * (SWA_WINDOW + CHUNK)), F32),
            pltpu.VMEM((N_PAIRS * tt, 2 * (SWA_WINDOW + CHUNK)), BF16),
            pltpu.VMEM((N_PAIRS * tt, LANES), F32),
        ],
        compiler_params=pltpu.CompilerParams(
            dimension_semantics=("arbitrary",), vmem_limit_bytes=VMEM_LIMIT),
        name="prompt_mixer",
    )(sinks, x, x, tab, dmask, qdec, kdec, gt, w_in, gnw, w_out, ln1w, ln1b, w_up, w_down)


def _sample_mixer_kernel(sinks_ref, x_ref, tab_ref, dmask_ref, qdec_ref, kdec_ref, gt_ref, ck_ref, cv_ref,
                         state_ref, w_in_f32_ref, gnw_ref, w_out_f32_ref, ln1w_ref, ln1b_ref,
                         x1_ref, kout_ref, vout_ref, sout_ref, w_in_ref, w_out_ref,
                         p_scr, mix_scr, *, n_new):
    s_idx = pl.program_id(0)
    last_s = pl.num_programs(0) - 1
    ns = SAMPLE_STREAMS
    t = n_new
    r_step = ns * t

    @pl.when(s_idx == 0)
    def _():
        w_in_ref[...] = w_in_f32_ref[...].astype(BF16)
        w_in_ref[:, OFF_SQ:OFF_SK] = (w_in_f32_ref[:, OFF_SQ:OFF_SK] * SWA_Q_SCALE).astype(BF16)
        w_out_ref[...] = w_out_f32_ref[...].astype(BF16)
        p_scr[...] = _dot(x_ref[...].astype(BF16), w_in_ref[...])

    rows = pl.ds(pl.multiple_of(s_idx * r_step, r_step), r_step)
    of_stream = lambda a, si: a[si * t:(si + 1) * t]
    heads = range(RET_HEADS)
    streams = range(ns)
    groups = range(SWA_KV_HEADS)

    head_cols = lambda off, h: slice(off + h * LANES, off + (h + 1) * LANES)
    lanes_of = lambda h: slice(h * LANES, (h + 1) * LANES)
    q_r = [_rope_ret(p_scr[rows, head_cols(OFF_RQ, h)], _tab(tab_ref, T_RC), _tab(tab_ref, T_RS)) for h in heads]
    k_r = [_rope_ret(p_scr[rows, head_cols(OFF_RK, h)], _tab(tab_ref, T_RC), _tab(tab_ref, T_RS)) for h in heads]
    qb = [q.astype(BF16) for q in q_r]
    kb = [k.astype(BF16) for k in k_r]
    kd = [(k_r[h] * kdec_ref[:, lanes_of(h)]).astype(BF16) for h in heads]
    vb = [p_scr[rows, head_cols(OFF_RV, h)].astype(BF16) for h in heads]
    k_new = _rope_swa(p_scr[rows, OFF_SK:OFF_SK + KV_W], _tab(tab_ref, T_SC), _tab(tab_ref, T_S1), _tab(tab_ref, T_S2))
    v_new = p_scr[rows, OFF_SV:OFF_SV + KV_W]
    for si in streams:
        for grp in groups:
            kout_ref[si, :, grp, :] = of_stream(k_new, si)[:, grp * SWA_HD:(grp + 1) * SWA_HD]
            vout_ref[si, :, grp, :] = of_stream(v_new, si)[:, grp * SWA_HD:(grp + 1) * SWA_HD]
    qs = [_rope_swa(p_scr[rows, head_cols(OFF_SQ, m)],
                    _tab(tab_ref, T_SC), _tab(tab_ref, T_S1), _tab(tab_ref, T_S2)).astype(BF16)
          for m in range(N_PAIRS)]
    nk = ck_ref.shape[-1] + t
    cached = lambda ref, si: jnp.concatenate([ref[si, g].T for g in groups], axis=1)
    k_var = [_kv_variants(jnp.concatenate([cached(ck_ref, si), of_stream(k_new, si)], axis=0)) for si in streams]
    v_var = [_kv_variants(jnp.concatenate([cached(cv_ref, si), of_stream(v_new, si)], axis=0)) for si in streams]

    scores = [_dot_nt(qb[h], kb[h]) for h in heads]
    inter = [[_dot(of_stream(qb[h], si), state_ref[si, h].astype(BF16)) for si in streams] for h in heads]
    ktv = [[_dot_tn(of_stream(kd[h], si), of_stream(vb[h], si)) for si in streams] for h in heads]
    sw_s = [[_dot_nt(jnp.concatenate([of_stream(qs[grp * PAIRS_PER_KV + a], si) for a in range(PAIRS_PER_KV)], axis=0),
                     _pair_rows(k_var[si][2 * grp], k_var[si][2 * grp + 1], 0, nk))
             for grp in groups] for si in streams]

    scores = [(scores[h] * dmask_ref[h]).astype(BF16) for h in heads]
    for h in heads:
        for si in streams:
            sout_ref[si, h] = gt_ref[:, lanes_of(h)] * state_ref[si, h] + ktv[h][si]
    row = lax.broadcasted_iota(jnp.int32, (PAIRS_PER_KV * t, 1), 0)
    sw_e = []
    for si in streams:
        per_group = []
        for grp in groups:
            sink_e = sinks_ref[2 * grp * PAIRS_PER_KV]
            sink_o = sinks_ref[2 * grp * PAIRS_PER_KV + 1]
            for a in range(1, PAIRS_PER_KV):
                sink_e = jnp.where(row >= a * t, sinks_ref[2 * (grp * PAIRS_PER_KV + a)], sink_e)
                sink_o = jnp.where(row >= a * t, sinks_ref[2 * (grp * PAIRS_PER_KV + a) + 1], sink_o)
            per_group.append(_swa_softmax(sw_s[si][grp], nk, sink_e, sink_o, None))
        sw_e.append(per_group)

    intra = [_dot(scores[h], vb[h]) for h in heads]
    ind = _head_indicator(nk)
    sw_o = [[_swa_out(sw_e[si][grp][0], _pair_rows(v_var[si][2 * grp], v_var[si][2 * grp + 1], 0, nk), ind,
                      sw_e[si][grp][1]).astype(BF16)
             for grp in groups] for si in streams]

    for h in heads:
        o = intra[h] + jnp.concatenate(inter[h], axis=0) * qdec_ref[:, lanes_of(h)]
        mu = jnp.mean(o, axis=-1, keepdims=True)
        oc = o - mu
        var = jnp.mean(oc * oc, axis=-1, keepdims=True)
        on = oc * lax.rsqrt(var + GN_EPS) * gnw_ref[:, lanes_of(h)]
        mix_scr[rows, lanes_of(h)] = (jax.nn.silu(p_scr[rows, head_cols(OFF_RG, h)]) * on).astype(BF16)
    for m in range(N_PAIRS):
        grp, a = divmod(m, PAIRS_PER_KV)
        o = jnp.concatenate([sw_o[si][grp][a * t:(a + 1) * t] for si in streams], axis=0)
        mix_scr[rows, RET_W + m * LANES:RET_W + (m + 1) * LANES] = o

    @pl.when(s_idx == last_s)
    def _():
        z = ALPHA * x_ref[...] + _dot(mix_scr[...], w_out_ref[...])
        x1_ref[...] = _layer_norm(z, ln1w_ref[...], ln1b_ref[...])


def _sample_mixer(x, tab, dmask, qdec, kdec, gt, cache_k, cache_v, state, sinks, w_in, gnw, w_out, ln1w, ln1b):
    nb, n_new, d = x.shape
    rows = nb * n_new
    win = cache_k.shape[-1]
    ns = SAMPLE_STREAMS
    r_step = ns * n_new
    const2 = lambda si: (0, 0)
    const3 = lambda si: (0, 0, 0)
    return pl.pallas_call(
        functools.partial(_sample_mixer_kernel, n_new=n_new),
        grid=(nb // ns,),
        in_specs=[
            pl.BlockSpec(memory_space=pltpu.SMEM),
            pl.BlockSpec((rows, d), const2),
            pl.BlockSpec((r_step, N_TABS * LANES), const2),
            pl.BlockSpec((RET_HEADS, r_step, r_step), const3),
            pl.BlockSpec((r_step, RET_W), const2),
            pl.BlockSpec((r_step, RET_W), const2),
            pl.BlockSpec((1, RET_W), const2),
            pl.BlockSpec((ns, SWA_KV_HEADS, SWA_HD, win), lambda si: (si, 0, 0, 0)),
            pl.BlockSpec((ns, SWA_KV_HEADS, SWA_HD, win), lambda si: (si, 0, 0, 0)),
            pl.BlockSpec((ns, RET_HEADS, RET_DK, RET_DV), lambda si: (si, 0, 0, 0)),
            pl.BlockSpec((d, PROJ_W), const2, pipeline_mode=pl.Buffered(1)),
            pl.BlockSpec((1, RET_W), const2),
            pl.BlockSpec((MIX_W, d), const2, pipeline_mode=pl.Buffered(1)),
            pl.BlockSpec((1, d), const2),
            pl.BlockSpec((1, d), const2),
        ],
        out_specs=[
            pl.BlockSpec((rows, d), const2),
            pl.BlockSpec((ns, n_new, SWA_KV_HEADS, SWA_HD), lambda si: (si, 0, 0, 0)),
            pl.BlockSpec((ns, n_new, SWA_KV_HEADS, SWA_HD), lambda si: (si, 0, 0, 0)),
            pl.BlockSpec((ns, RET_HEADS, RET_DK, RET_DV), lambda si: (si, 0, 0, 0)),
            pl.BlockSpec((d, PROJ_W), const2),
            pl.BlockSpec((MIX_W, d), const2),
        ],
        out_shape=[
            jax.ShapeDtypeStruct((rows, d), F32),
            jax.ShapeDtypeStruct((nb, n_new, SWA_KV_HEADS, SWA_HD), F32),
            jax.ShapeDtypeStruct((nb, n_new, SWA_KV_HEADS, SWA_HD), F32),
            jax.ShapeDtypeStruct((nb, RET_HEADS, RET_DK, RET_DV), F32),
            jax.ShapeDtypeStruct((d, PROJ_W), BF16),
            jax.ShapeDtypeStruct((MIX_W, d), BF16),
        ],
        scratch_shapes=[
            pltpu.VMEM((rows, PROJ_W), F32),
            pltpu.VMEM((rows, MIX_W), BF16),
        ],
        compiler_params=pltpu.CompilerParams(
            dimension_semantics=("arbitrary",), vmem_limit_bytes=VMEM_LIMIT),
        name="sample_mixer",
    )(sinks, x.reshape(rows, d), tab, dmask, qdec, kdec, gt, cache_k, cache_v, state, w_in, gnw, w_out, ln1w, ln1b)


def _mlp_kernel(xp_ref, xs_ref, w_up_ref, w_down_ref, lnw_ref, lnb_ref, yp_ref, ys_ref, h_scr, *, prompt_steps):
    i = pl.program_id(0)

    def block(x_ref, y_ref):
        for r0 in range(0, x_ref.shape[0], MLP_UP_ROWS):
            n = min(MLP_UP_ROWS, x_ref.shape[0] - r0)
            h = _dot(x_ref[r0:r0 + n, :].astype(BF16), w_up_ref[...])
            h_scr[0:n, :] = jnp.square(jnp.maximum(h, 0.0)).astype(BF16)
            for r in range(0, n, MLP_LN_ROWS):
                rows = slice(r0 + r, r0 + r + MLP_LN_ROWS)
                z = ALPHA * x_ref[rows, :] + _dot(h_scr[r:r + MLP_LN_ROWS, :], w_down_ref[...])
                y_ref[rows, :] = _layer_norm(z, lnw_ref[...], lnb_ref[...])

    @pl.when(i < prompt_steps)
    def _():
        block(xp_ref, yp_ref)

    @pl.when(i == prompt_steps)
    def _():
        block(xs_ref, ys_ref)


def _mlp(xp, xs, w_up, w_down, lnw, lnb):
    rows, d = xp.shape
    rows_s = xs.shape[0]
    steps = rows // MLP_TILE
    const2 = lambda i: (0, 0)
    prompt_tile = lambda i: (jnp.minimum(i, steps - 1), 0)
    resident = dict(pipeline_mode=pl.Buffered(1))
    return pl.pallas_call(
        functools.partial(_mlp_kernel, prompt_steps=steps),
        grid=(steps + 1,),
        in_specs=[
            pl.BlockSpec((MLP_TILE, d), prompt_tile),
            pl.BlockSpec((rows_s, d), const2, **resident),
            pl.BlockSpec((d, D_FF), const2, **resident),
            pl.BlockSpec((D_FF, d), const2, **resident),
            pl.BlockSpec((1, d), const2),
            pl.BlockSpec((1, d), const2),
        ],
        out_specs=[
            pl.BlockSpec((MLP_TILE, d), prompt_tile),
            pl.BlockSpec((rows_s, d), const2),
        ],
        out_shape=[
            jax.ShapeDtypeStruct((rows, d), F32),
            jax.ShapeDtypeStruct((rows_s, d), F32),
        ],
        scratch_shapes=[pltpu.VMEM((MLP_UP_ROWS, D_FF), BF16)],
        compiler_params=pltpu.CompilerParams(
            dimension_semantics=("arbitrary",), vmem_limit_bytes=VMEM_LIMIT),
        name="mlp",
    )(xp, xs, w_up, w_down, lnw, lnb)


def _rope_tables(pos):
    posf = np.asarray(pos, np.float64)[:, None]
    half_r = RET_DK // 2
    ang = posf * (RET_THETA ** (-np.arange(half_r, dtype=np.float64) / half_r))[None, :]
    c, s = np.cos(ang), np.sin(ang)
    rc = np.concatenate([c, c], axis=-1)
    rs = np.concatenate([-s, s], axis=-1)
    half_s = SWA_ROT_DIM // 2
    ang = posf * (SWA_THETA ** (-np.arange(half_s, dtype=np.float64) / half_s))[None, :]
    c, s = np.cos(ang), np.sin(ang)
    t = posf.shape[0]
    rest = SWA_HD - SWA_ROT_DIM
    z = np.zeros((t, half_s))
    sc = np.tile(np.concatenate([c, c, np.ones((t, rest))], axis=-1), (1, LANES // SWA_HD))
    s1 = np.tile(np.concatenate([z, s, np.zeros((t, rest))], axis=-1), (1, LANES // SWA_HD))
    s2 = np.tile(np.concatenate([-s, z, np.zeros((t, rest))], axis=-1), (1, LANES // SWA_HD))
    return jnp.asarray(np.concatenate([rc, rs, sc, s1, s2], axis=-1).astype(np.float32))


def _decay_tables(t, copies=1):
    k_scale = RET_DK ** -0.5
    lg = np.log1p(-(2.0 ** (-5.0 - np.arange(RET_HEADS, dtype=np.float64))))
    idx = np.arange(t, dtype=np.float64)
    diff = idx[:, None] - idx[None, :]
    dmask = k_scale * np.where(diff >= 0, np.exp(lg[:, None, None] * np.maximum(diff, 0.0)), 0.0)
    qdec = np.repeat(np.exp(lg[None, :] * (idx[:, None] + 1.0)), RET_DV, axis=1)
    kdec = k_scale * np.repeat(np.exp(lg[None, :] * (t - 1.0 - idx[:, None])), RET_DV, axis=1)
    gt = np.repeat(np.exp(lg * t)[None, :], RET_DV, axis=1)
    if copies > 1:
        dmask = np.stack([np.kron(np.eye(copies), m) for m in dmask])
        qdec, kdec = np.tile(qdec, (copies, 1)), np.tile(kdec, (copies, 1))
    return tuple(jnp.asarray(a.astype(np.float32)) for a in (dmask, qdec, kdec, gt))


def kernel(x_prompt, x_sample, cache_swa_k, cache_swa_v, state_ret, w_in, ret_gn_w, swa_sinks,
           w_out, ln1_w, ln1_b, w_up, w_down, ln2_w, ln2_b):
    assert w_in.shape[0] == DEPTH == 1
    b, t, d = x_prompt.shape
    nb, n_new, _ = x_sample.shape
    win = cache_swa_k.shape[2]

    sinks = swa_sinks[0]

    tab_p = _rope_tables(np.arange(t))
    tab_s = _rope_tables(np.tile(PAST_LEN + np.arange(n_new), SAMPLE_STREAMS))
    dec_p = _decay_tables(PROMPT_TILE)
    dec_s = _decay_tables(n_new, SAMPLE_STREAMS)

    x1_s, k_s, v_s, r_s, w_in_b, w_out_b = _sample_mixer(
        x_sample, tab_s, *dec_s, jnp.transpose(cache_swa_k[0], (0, 2, 3, 1)), jnp.transpose(cache_swa_v[0], (0, 2, 3, 1)),
        state_ret[0], sinks, w_in[0], ret_gn_w, w_out[0], ln1_w, ln1_b)
    x1_p, k_p, v_p, r_p, w_up_b, w_down_b = _prompt_mixer(
        x_prompt, tab_p, *dec_p, sinks, w_in_b, ret_gn_w, w_out_b, ln1_w, ln1_b, w_up[0], w_down[0])
    y_p, y_s = _mlp(x1_p.reshape(b * t, d), x1_s, w_up_b, w_down_b, ln2_w, ln2_b)

    return (y_p.reshape(b, t, d), y_s.reshape(nb, n_new, d),
            jnp.transpose(k_p, (0, 3, 1, 2))[None],
            jnp.transpose(v_p, (0, 3, 1, 2))[None],
            r_p.reshape(DEPTH, b, RET_HEADS, RET_DK, RET_DV),
            k_s[None], v_s[None],
            r_s.reshape(DEPTH, nb, RET_HEADS, RET_DK, RET_DV))
```

```python
import functools

import numpy as np
import jax
import jax.numpy as jnp
from jax import lax
from jax.experimental import pallas as pl
from jax.experimental.pallas import tpu as pltpu

D_MODEL = 1024
DEPTH = 1
PAST_LEN = 2048
CHUNK = 64
RET_HEADS = 4
RET_DK = 128
RET_DV = 128
RET_THETA = 10000.0
RET_W = RET_HEADS * RET_DV
SWA_HEADS = 8
SWA_KV_HEADS = 2
SWA_HD = 64
SWA_WINDOW = 128
SWA_ROT_DIM = SWA_HD // 4
SWA_THETA = 500000.0
SWA_W = SWA_HEADS * SWA_HD
KV_W = SWA_KV_HEADS * SWA_HD
MIX_W = RET_W + SWA_W
D_FF = 4 * D_MODEL
OFF_RQ, OFF_RK, OFF_RV, OFF_RG = 0, RET_W, 2 * RET_W, 3 * RET_W
OFF_SQ = 4 * RET_W
OFF_SK = OFF_SQ + SWA_W
OFF_SV = OFF_SK + KV_W
PROJ_W = OFF_SV + KV_W
SWA_Q_SCALE = SWA_HD ** -0.5
assert SWA_Q_SCALE == 0.125
ALPHA = (2.0 * DEPTH) ** 0.25
LN_EPS = 1e-5
GN_EPS = 1e-5
NEG_INF = -1e30

LANES = 128
MXU_COLS = 256
N_PAIRS = SWA_W // LANES
PAIRS_PER_KV = N_PAIRS // SWA_KV_HEADS
T_RC, T_RS, T_SC, T_S1, T_S2 = range(5)
N_TABS = 5

PROMPT_TILE = 256
TILES_PER_STEP = 2
N_SLABS = PROJ_W // MXU_COLS
LN_BLOCKS = 4
SLABS_AFTER_RET_PHASE = (1, 1, 1)
SLABS_AFTER_SOFTMAX_BLOCK = (1, 1, 1, 1, 1, 1, 0, 0)
SLABS_AFTER_LN_BLOCK = (1, 1, 0, 0)
MLP_CAST_STEPS = 8
SAMPLE_STREAMS = 8
MLP_TILE = 1024
MLP_UP_ROWS = 512
MLP_LN_ROWS = 64
VMEM_LIMIT = 56 * 1024 * 1024

F32 = jnp.float32
BF16 = jnp.bfloat16


def _tab(tab_ref, i):
    return tab_ref[:, i * LANES:(i + 1) * LANES]


def _dot(a, b):
    return jnp.dot(a, b, preferred_element_type=F32)


def _dot_nt(a, b):
    return lax.dot_general(a, b, (((1,), (1,)), ((), ())), preferred_element_type=F32)


def _dot_tn(a, b):
    return lax.dot_general(a, b, (((0,), (0,)), ((), ())), preferred_element_type=F32)


def _layer_norm(z, w, b):
    mu = jnp.mean(z, axis=-1, keepdims=True)
    zc = z - mu
    var = jnp.mean(zc * zc, axis=-1, keepdims=True)
    return zc * lax.rsqrt(var + LN_EPS) * w + b


def _rope_ret(x, c, s):
    return x * c + pltpu.roll(x, RET_DK // 2, axis=1) * s


def _rope_swa(x, c, s1, s2):
    half = SWA_ROT_DIM // 2
    return x * c + pltpu.roll(x, half, axis=1) * s1 + pltpu.roll(x, LANES - half, axis=1) * s2


def _kv_variants(a):
    lane = lax.broadcasted_iota(jnp.int32, a.shape, 1)
    lo = lane < SWA_HD
    ar = pltpu.roll(a, SWA_HD, axis=1)
    zero = jnp.zeros_like(a)
    return (jnp.where(lo, a, zero).astype(BF16), jnp.where(lo, zero, ar).astype(BF16),
            jnp.where(lo, ar, zero).astype(BF16), jnp.where(lo, zero, a).astype(BF16))


def _pair_rows(a_e, a_o, r0, nk):
    return jnp.concatenate([a_e[r0:r0 + LANES], a_o[r0:r0 + LANES],
                            a_e[r0 + LANES:r0 + nk], a_o[r0 + LANES:r0 + nk]], axis=0)


def _swa_softmax(s, nk, sink_e, sink_o, valid):
    r = nk - LANES
    s_e, s_o, s_x = s[:, :LANES], s[:, LANES:2 * LANES], s[:, 2 * LANES:]
    if valid is not None:
        s_e = jnp.where(valid, s_e, NEG_INF)
        s_o = jnp.where(valid, s_o, NEG_INF)
    x_is_e = lax.broadcasted_iota(jnp.int32, s_x.shape, 1) < r
    rowmax = lambda a: jnp.max(a, axis=-1, keepdims=True)
    x_e = jnp.where(x_is_e, s_x, -jnp.inf)
    x_o = jnp.where(x_is_e, -jnp.inf, s_x)
    if 2 * r == LANES:
        m_e, m_o = rowmax(jnp.maximum(s_e, x_e)), rowmax(jnp.maximum(s_o, x_o))
    else:
        m_e, m_o = jnp.maximum(rowmax(s_e), rowmax(x_e)), jnp.maximum(rowmax(s_o), rowmax(x_o))
    m_e = jnp.maximum(m_e, sink_e)
    m_o = jnp.maximum(m_o, sink_o)
    e_e = jnp.exp(s_e - m_e)
    e_o = jnp.exp(s_o - m_o)
    e_x = jnp.exp(s_x - jnp.where(x_is_e, m_e, m_o))
    lane = lax.broadcasted_iota(jnp.int32, (s.shape[0], LANES), 1)
    e = jnp.concatenate([e_e, e_o, e_x], axis=1).astype(BF16)
    return e, jnp.where(lane < SWA_HD, jnp.exp(sink_e - m_e), jnp.exp(sink_o - m_o))


def _head_indicator(nk):
    row = lax.broadcasted_iota(jnp.int32, (2 * nk, LANES), 0)
    lane = lax.broadcasted_iota(jnp.int32, (2 * nk, LANES), 1)
    even_row = jnp.logical_or(row < LANES, jnp.logical_and(row >= 2 * LANES, row < LANES + nk))
    return jnp.where(even_row == (lane < SWA_HD), 1.0, 0.0).astype(BF16)


def _swa_out(e, vt, ind, sink_term):
    res = _dot(e, jnp.concatenate([vt, ind], axis=1))
    return res[:, :LANES] / (res[:, LANES:] + sink_term)


def _prompt_mixer_kernel(sinks_ref, x_ref, xn_ref, tab_ref, dmask_ref, qdec_ref, kdec_ref, gt_ref, w_in_ref,
                         gnw_ref, w_out_ref, ln1w_ref, ln1b_ref, w_up_f32_ref, w_down_f32_ref,
                         x1_ref, kout_ref, vout_ref, sout_ref, w_up_ref, w_down_ref,
                         pa_scr, pb_scr, xb_scr, s_scr, kext_scr, vext_scr, mix_scr, sw_s_scr, sw_e_scr, sw_d_scr,
                         *, steps_per_seq):
    tt = PROMPT_TILE
    g = pl.program_id(0)
    seq_step = g % steps_per_seq

    @pl.when(g < MLP_CAST_STEPS)
    def _():
        w_up_ref[...] = w_up_f32_ref[...].astype(BF16)
        w_down_ref[...] = w_down_f32_ref[...].astype(BF16)

    @pl.when(seq_step == 0)
    def _():
        s_scr[...] = jnp.zeros_like(s_scr)
        kext_scr[0:SWA_WINDOW, :] = jnp.zeros((SWA_WINDOW, KV_W), F32)
        vext_scr[0:SWA_WINDOW, :] = jnp.zeros((SWA_WINDOW, KV_W), F32)

    @pl.when(g == 0)
    def _():
        pa_scr[...] = _dot(x_ref[0, 0:tt, :].astype(BF16), w_in_ref[...])

    def project_next_slabs(p_nxt, slabs, count):
        for _ in range(count):
            c = next(slabs)
            cols = slice(c * MXU_COLS, (c + 1) * MXU_COLS)
            p_nxt[:, cols] = _dot(xb_scr[...], w_in_ref[:, cols])

    def tile(r0x, p_cur, p_nxt, load_next_x, seq_start):
        rows_x = slice(r0x, r0x + tt)
        tab = lambda i: tab_ref[rows_x, i * LANES:(i + 1) * LANES]
        xb_scr[...] = load_next_x().astype(BF16)
        slabs = iter(range(N_SLABS))

        heads = range(RET_HEADS)
        lanes_of = lambda h: slice(h * LANES, (h + 1) * LANES)
        head_cols = lambda off, h: slice(off + h * LANES, off + (h + 1) * LANES)
        q_r = [_rope_ret(p_cur[:, head_cols(OFF_RQ, h)], tab(T_RC), tab(T_RS)) for h in heads]
        k_r = [_rope_ret(p_cur[:, head_cols(OFF_RK, h)], tab(T_RC), tab(T_RS)) for h in heads]
        qb = [q.astype(BF16) for q in q_r]
        kb = [k.astype(BF16) for k in k_r]
        vb = [p_cur[:, head_cols(OFF_RV, h)].astype(BF16) for h in heads]
        kd = [(k_r[h] * kdec_ref[:, lanes_of(h)]).astype(BF16) for h in heads]
        scores = [_dot_nt(qb[h], kb[h]) for h in heads]
        inter = [_dot(qb[h], s_scr[h].astype(BF16)) for h in heads]
        ktv = [_dot_tn(kd[h], vb[h]) for h in heads]
        project_next_slabs(p_nxt, slabs, SLABS_AFTER_RET_PHASE[0])
        scores = [(scores[h] * dmask_ref[h]).astype(BF16) for h in heads]
        for h in heads:
            s_scr[h] = gt_ref[:, lanes_of(h)] * s_scr[h] + ktv[h]
        project_next_slabs(p_nxt, slabs, SLABS_AFTER_RET_PHASE[1])
        intra = [_dot(scores[h], vb[h]) for h in heads]
        project_next_slabs(p_nxt, slabs, SLABS_AFTER_RET_PHASE[2])
        for h in heads:
            o = intra[h] + inter[h] * qdec_ref[:, lanes_of(h)]
            mu = jnp.mean(o, axis=-1, keepdims=True)
            oc = o - mu
            var = jnp.mean(oc * oc, axis=-1, keepdims=True)
            on = oc * lax.rsqrt(var + GN_EPS) * gnw_ref[:, lanes_of(h)]
            mix_scr[:, lanes_of(h)] = (jax.nn.silu(p_cur[:, head_cols(OFF_RG, h)]) * on).astype(BF16)

        k_new = _rope_swa(p_cur[:, OFF_SK:OFF_SK + KV_W], tab(T_SC), tab(T_S1), tab(T_S2))
        v_new = p_cur[:, OFF_SV:OFF_SV + KV_W]
        kext_scr[SWA_WINDOW:SWA_WINDOW + tt, :] = k_new
        vext_scr[SWA_WINDOW:SWA_WINDOW + tt, :] = v_new
        k_var = _kv_variants(kext_scr[...])
        v_var = _kv_variants(vext_scr[...])
        nk = SWA_WINDOW + CHUNK
        ind = _head_indicator(nk)
        n_chunks = tt // CHUNK
        qs = [_rope_swa(p_cur[:, OFF_SQ + m * LANES:OFF_SQ + (m + 1) * LANES],
                        tab(T_SC), tab(T_S1), tab(T_S2)).astype(BF16)
              for m in range(N_PAIRS)]
        col = lax.broadcasted_iota(jnp.int32, (PAIRS_PER_KV * CHUNK, LANES), 1)
        row = lax.broadcasted_iota(jnp.int32, (PAIRS_PER_KV * CHUNK, 1), 0)
        blocks = [(grp, i) for grp in range(SWA_KV_HEADS) for i in range(n_chunks)]
        rows_of = lambda n: slice(n * PAIRS_PER_KV * CHUNK, (n + 1) * PAIRS_PER_KV * CHUNK)
        for n, (grp, i) in enumerate(blocks):
            r0 = i * CHUNK
            q_blk = jnp.concatenate([qs[grp * PAIRS_PER_KV + a][r0:r0 + CHUNK] for a in range(PAIRS_PER_KV)], axis=0)
            sw_s_scr[rows_of(n), :] = _dot_nt(q_blk, _pair_rows(k_var[2 * grp], k_var[2 * grp + 1], r0, nk))
        for n, (grp, i) in enumerate(blocks):
            sink_e = sinks_ref[2 * grp * PAIRS_PER_KV]
            sink_o = sinks_ref[2 * grp * PAIRS_PER_KV + 1]
            for a in range(1, PAIRS_PER_KV):
                sink_e = jnp.where(row >= a * CHUNK, sinks_ref[2 * (grp * PAIRS_PER_KV + a)], sink_e)
                sink_o = jnp.where(row >= a * CHUNK, sinks_ref[2 * (grp * PAIRS_PER_KV + a) + 1], sink_o)
            valid = None
            if seq_start is not None and i < SWA_WINDOW // CHUNK:
                valid = jnp.logical_or(col >= SWA_WINDOW - i * CHUNK, jnp.logical_not(seq_start))
            e, sink_term = _swa_softmax(sw_s_scr[rows_of(n), :], nk, sink_e, sink_o, valid)
            sw_e_scr[rows_of(n), :] = e
            sw_d_scr[rows_of(n), :] = sink_term
            project_next_slabs(p_nxt, slabs, SLABS_AFTER_SOFTMAX_BLOCK[n])
        for n, (grp, i) in enumerate(blocks):
            r0 = i * CHUNK
            vt = _pair_rows(v_var[2 * grp], v_var[2 * grp + 1], r0, nk)
            o = _swa_out(sw_e_scr[rows_of(n), :], vt, ind, sw_d_scr[rows_of(n), :]).astype(BF16)
            for a in range(PAIRS_PER_KV):
                m = grp * PAIRS_PER_KV + a
                mix_scr[r0:r0 + CHUNK, RET_W + m * LANES:RET_W + (m + 1) * LANES] = o[a * CHUNK:(a + 1) * CHUNK]

        kext_scr[0:SWA_WINDOW, :] = k_new[tt - SWA_WINDOW:, :]
        vext_scr[0:SWA_WINDOW, :] = v_new[tt - SWA_WINDOW:, :]

        x1_ref[0, rows_x, :] = ALPHA * x_ref[0, rows_x, :] + _dot(mix_scr[...], w_out_ref[...])
        lb = tt // LN_BLOCKS
        for r in range(LN_BLOCKS):
            rows_ln = slice(r0x + r * lb, r0x + (r + 1) * lb)
            x1_ref[0, rows_ln, :] = _layer_norm(x1_ref[0, rows_ln, :], ln1w_ref[...], ln1b_ref[...])
            project_next_slabs(p_nxt, slabs, SLABS_AFTER_LN_BLOCK[r])
        assert next(slabs, None) is None
        return k_new, v_new

    for i in range(TILES_PER_STEP):
        p_cur, p_nxt = (pa_scr, pb_scr) if i % 2 == 0 else (pb_scr, pa_scr)
        if i + 1 < TILES_PER_STEP:
            load_next_x = functools.partial(lambda r0: x_ref[0, r0:r0 + tt, :], (i + 1) * tt)
        else:
            load_next_x = lambda: xn_ref[0]
        k_new, v_new = tile(i * tt, p_cur, p_nxt, load_next_x, seq_step == 0 if i == 0 else None)

    @pl.when(seq_step == steps_per_seq - 1)
    def _():
        for grp in range(SWA_KV_HEADS):
            kout_ref[0, grp] = k_new[tt - SWA_WINDOW:, grp * SWA_HD:(grp + 1) * SWA_HD].T
            vout_ref[0, grp] = v_new[tt - SWA_WINDOW:, grp * SWA_HD:(grp + 1) * SWA_HD].T
        sout_ref[0] = s_scr[...]


def _prompt_mixer(x, tab, dmask, qdec, kdec, gt, sinks, w_in, gnw, w_out, ln1w, ln1b, w_up, w_down):
    b, t, d = x.shape
    tt = PROMPT_TILE
    rows_step = TILES_PER_STEP * tt
    sps = t // rows_step
    n_steps = b * sps
    up_rows, down_rows = w_up.shape[0] // MLP_CAST_STEPS, w_down.shape[0] // MLP_CAST_STEPS
    row_slice = lambda g: (jnp.minimum(g, MLP_CAST_STEPS - 1), 0)
    const2 = lambda g: (0, 0)
    const3 = lambda g: (0, 0, 0)

    def next_tile(g):
        gn = jnp.minimum(g + 1, n_steps - 1)
        return (gn // sps, (gn % sps) * TILES_PER_STEP, 0)

    resident = dict(pipeline_mode=pl.Buffered(1))
    return pl.pallas_call(
        functools.partial(_prompt_mixer_kernel, steps_per_seq=sps),
        grid=(n_steps,),
        in_specs=[
            pl.BlockSpec(memory_space=pltpu.SMEM),
            pl.BlockSpec((1, rows_step, d), lambda g: (g // sps, g % sps, 0)),
            pl.BlockSpec((1, tt, d), next_tile),
            pl.BlockSpec((rows_step, N_TABS * LANES), lambda g: (g % sps, 0)),
            pl.BlockSpec((RET_HEADS, tt, tt), const3, **resident),
            pl.BlockSpec((tt, RET_W), const2, **resident),
            pl.BlockSpec((tt, RET_W), const2, **resident),
            pl.BlockSpec((1, RET_W), const2),
            pl.BlockSpec((d, PROJ_W), const2, **resident),
            pl.BlockSpec((1, RET_W), const2),
            pl.BlockSpec((MIX_W, d), const2, **resident),
            pl.BlockSpec((1, d), const2),
            pl.BlockSpec((1, d), const2),
            pl.BlockSpec((up_rows, w_up.shape[1]), row_slice),
            pl.BlockSpec((down_rows, w_down.shape[1]), row_slice),
        ],
        out_specs=[
            pl.BlockSpec((1, rows_step, d), lambda g: (g // sps, g % sps, 0)),
            pl.BlockSpec((1, SWA_KV_HEADS, SWA_HD, SWA_WINDOW), lambda g: (g // sps, 0, 0, 0)),
            pl.BlockSpec((1, SWA_KV_HEADS, SWA_HD, SWA_WINDOW), lambda g: (g // sps, 0, 0, 0)),
            pl.BlockSpec((1, RET_HEADS, RET_DK, RET_DV), lambda g: (g // sps, 0, 0, 0)),
            pl.BlockSpec((up_rows, w_up.shape[1]), row_slice),
            pl.BlockSpec((down_rows, w_down.shape[1]), row_slice),
        ],
        out_shape=[
            jax.ShapeDtypeStruct((b, t, d), F32),
            jax.ShapeDtypeStruct((b, SWA_KV_HEADS, SWA_HD, SWA_WINDOW), F32),
            jax.ShapeDtypeStruct((b, SWA_KV_HEADS, SWA_HD, SWA_WINDOW), F32),
            jax.ShapeDtypeStruct((b, RET_HEADS, RET_DK, RET_DV), F32),
            jax.ShapeDtypeStruct(w_up.shape, BF16),
            jax.ShapeDtypeStruct(w_down.shape, BF16),
        ],
        scratch_shapes=[
            pltpu.VMEM((tt, PROJ_W), F32),
            pltpu.VMEM((tt, PROJ_W), F32),
            pltpu.VMEM((tt, d), BF16),
            pltpu.VMEM((RET_HEADS, RET_DK, RET_DV), F32),
            pltpu.VMEM((SWA_WINDOW + tt, KV_W), F32),
            pltpu.VMEM((SWA_WINDOW + tt, KV_W), F32),
            pltpu.VMEM((tt, MIX_W), BF16),
            pltpu.VMEM((N_PAIRS * tt, 2 * (SWA_WINDOW + CHUNK)), F32),
            pltpu.VMEM((N_PAIRS * tt, 2 * (SWA_WINDOW + CHUNK)), BF16),
            pltpu.VMEM((N_PAIRS * tt, LANES), F32),
        ],
        compiler_params=pltpu.CompilerParams(
            dimension_semantics=("arbitrary",), vmem_limit_bytes=VMEM_LIMIT),
        name="prompt_mixer",
    )(sinks, x, x, tab, dmask, qdec, kdec, gt, w_in, gnw, w_out, ln1w, ln1b, w_up, w_down)


def _sample_mixer_kernel(sinks_ref, x_ref, tab_ref, dmask_ref, qdec_ref, kdec_ref, gt_ref, ck_ref, cv_ref,
                         state_ref, w_in_f32_ref, gnw_ref, w_out_f32_ref, ln1w_ref, ln1b_ref,
                         x1_ref, kout_ref, vout_ref, sout_ref, w_in_ref, w_out_ref,
                         p_scr, mix_scr, *, n_new):
    s_idx = pl.program_id(0)
    last_s = pl.num_programs(0) - 1
    ns = SAMPLE_STREAMS
    t = n_new
    r_step = ns * t

    @pl.when(s_idx == 0)
    def _():
        w_in_ref[...] = w_in_f32_ref[...].astype(BF16)
        w_in_ref[:, OFF_SQ:OFF_SK] = (w_in_f32_ref[:, OFF_SQ:OFF_SK] * SWA_Q_SCALE).astype(BF16)
        w_out_ref[...] = w_out_f32_ref[...].astype(BF16)
        p_scr[...] = _dot(x_ref[...].astype(BF16), w_in_ref[...])

    rows = pl.ds(pl.multiple_of(s_idx * r_step, r_step), r_step)
    of_stream = lambda a, si: a[si * t:(si + 1) * t]
    heads = range(RET_HEADS)
    streams = range(ns)
    groups = range(SWA_KV_HEADS)

    head_cols = lambda off, h: slice(off + h * LANES, off + (h + 1) * LANES)
    lanes_of = lambda h: slice(h * LANES, (h + 1) * LANES)
    q_r = [_rope_ret(p_scr[rows, head_cols(OFF_RQ, h)], _tab(tab_ref, T_RC), _tab(tab_ref, T_RS)) for h in heads]
    k_r = [_rope_ret(p_scr[rows, head_cols(OFF_RK, h)], _tab(tab_ref, T_RC), _tab(tab_ref, T_RS)) for h in heads]
    qb = [q.astype(BF16) for q in q_r]
    kb = [k.astype(BF16) for k in k_r]
    kd = [(k_r[h] * kdec_ref[:, lanes_of(h)]).astype(BF16) for h in heads]
    vb = [p_scr[rows, head_cols(OFF_RV, h)].astype(BF16) for h in heads]
    k_new = _rope_swa(p_scr[rows, OFF_SK:OFF_SK + KV_W], _tab(tab_ref, T_SC), _tab(tab_ref, T_S1), _tab(tab_ref, T_S2))
    v_new = p_scr[rows, OFF_SV:OFF_SV + KV_W]
    for si in streams:
        for grp in groups:
            kout_ref[si, :, grp, :] = of_stream(k_new, si)[:, grp * SWA_HD:(grp + 1) * SWA_HD]
            vout_ref[si, :, grp, :] = of_stream(v_new, si)[:, grp * SWA_HD:(grp + 1) * SWA_HD]
    qs = [_rope_swa(p_scr[rows, head_cols(OFF_SQ, m)],
                    _tab(tab_ref, T_SC), _tab(tab_ref, T_S1), _tab(tab_ref, T_S2)).astype(BF16)
          for m in range(N_PAIRS)]
    nk = ck_ref.shape[-1] + t
    cached = lambda ref, si: jnp.concatenate([ref[si, g].T for g in groups], axis=1)
    k_var = [_kv_variants(jnp.concatenate([cached(ck_ref, si), of_stream(k_new, si)], axis=0)) for si in streams]
    v_var = [_kv_variants(jnp.concatenate([cached(cv_ref, si), of_stream(v_new, si)], axis=0)) for si in streams]

    scores = [_dot_nt(qb[h], kb[h]) for h in heads]
    inter = [[_dot(of_stream(qb[h], si), state_ref[si, h].astype(BF16)) for si in streams] for h in heads]
    ktv = [[_dot_tn(of_stream(kd[h], si), of_stream(vb[h], si)) for si in streams] for h in heads]
    sw_s = [[_dot_nt(jnp.concatenate([of_stream(qs[grp * PAIRS_PER_KV + a], si) for a in range(PAIRS_PER_KV)], axis=0),
                     _pair_rows(k_var[si][2 * grp], k_var[si][2 * grp + 1], 0, nk))
             for grp in groups] for si in streams]

    scores = [(scores[h] * dmask_ref[h]).astype(BF16) for h in heads]
    for h in heads:
        for si in streams:
            sout_ref[si, h] = gt_ref[:, lanes_of(h)] * state_ref[si, h] + ktv[h][si]
    row = lax.broadcasted_iota(jnp.int32, (PAIRS_PER_KV * t, 1), 0)
    sw_e = []
    for si in streams:
        per_group = []
        for grp in groups:
            sink_e = sinks_ref[2 * grp * PAIRS_PER_KV]
            sink_o = sinks_ref[2 * grp * PAIRS_PER_KV + 1]
            for a in range(1, PAIRS_PER_KV):
                sink_e = jnp.where(row >= a * t, sinks_ref[2 * (grp * PAIRS_PER_KV + a)], sink_e)
                sink_o = jnp.where(row >= a * t, sinks_ref[2 * (grp * PAIRS_PER_KV + a) + 1], sink_o)
            per_group.append(_swa_softmax(sw_s[si][grp], nk, sink_e, sink_o, None))
        sw_e.append(per_group)

    intra = [_dot(scores[h], vb[h]) for h in heads]
    ind = _head_indicator(nk)
    sw_o = [[_swa_out(sw_e[si][grp][0], _pair_rows(v_var[si][2 * grp], v_var[si][2 * grp + 1], 0, nk), ind,
                      sw_e[si][grp][1]).astype(BF16)
             for grp in groups] for si in streams]

    for h in heads:
        o = intra[h] + jnp.concatenate(inter[h], axis=0) * qdec_ref[:, lanes_of(h)]
        mu = jnp.mean(o, axis=-1, keepdims=True)
        oc = o - mu
        var = jnp.mean(oc * oc, axis=-1, keepdims=True)
        on = oc * lax.rsqrt(var + GN_EPS) * gnw_ref[:, lanes_of(h)]
        mix_scr[rows, lanes_of(h)] = (jax.nn.silu(p_scr[rows, head_cols(OFF_RG, h)]) * on).astype(BF16)
    for m in range(N_PAIRS):
        grp, a = divmod(m, PAIRS_PER_KV)
        o = jnp.concatenate([sw_o[si][grp][a * t:(a + 1) * t] for si in streams], axis=0)
        mix_scr[rows, RET_W + m * LANES:RET_W + (m + 1) * LANES] = o

    @pl.when(s_idx == last_s)
    def _():
        z = ALPHA * x_ref[...] + _dot(mix_scr[...], w_out_ref[...])
        x1_ref[...] = _layer_norm(z, ln1w_ref[...], ln1b_ref[...])


def _sample_mixer(x, tab, dmask, qdec, kdec, gt, cache_k, cache_v, state, sinks, w_in, gnw, w_out, ln1w, ln1b):
    nb, n_new, d = x.shape
    rows = nb * n_new
    win = cache_k.shape[-1]
    ns = SAMPLE_STREAMS
    r_step = ns * n_new
    const2 = lambda si: (0, 0)
    const3 = lambda si: (0, 0, 0)
    return pl.pallas_call(
        functools.partial(_sample_mixer_kernel, n_new=n_new),
        grid=(nb // ns,),
        in_specs=[
            pl.BlockSpec(memory_space=pltpu.SMEM),
            pl.BlockSpec((rows, d), const2),
            pl.BlockSpec((r_step, N_TABS * LANES), const2),
            pl.BlockSpec((RET_HEADS, r_step, r_step), const3),
            pl.BlockSpec((r_step, RET_W), const2),
            pl.BlockSpec((r_step, RET_W), const2),
            pl.BlockSpec((1, RET_W), const2),
            pl.BlockSpec((ns, SWA_KV_HEADS, SWA_HD, win), lambda si: (si, 0, 0, 0)),
            pl.BlockSpec((ns, SWA_KV_HEADS, SWA_HD, win), lambda si: (si, 0, 0, 0)),
            pl.BlockSpec((ns, RET_HEADS, RET_DK, RET_DV), lambda si: (si, 0, 0, 0)),
            pl.BlockSpec((d, PROJ_W), const2, pipeline_mode=pl.Buffered(1)),
            pl.BlockSpec((1, RET_W), const2),
            pl.BlockSpec((MIX_W, d), const2, pipeline_mode=pl.Buffered(1)),
            pl.BlockSpec((1, d), const2),
            pl.BlockSpec((1, d), const2),
        ],
        out_specs=[
            pl.BlockSpec((rows, d), const2),
            pl.BlockSpec((ns, n_new, SWA_KV_HEADS, SWA_HD), lambda si: (si, 0, 0, 0)),
            pl.BlockSpec((ns, n_new, SWA_KV_HEADS, SWA_HD), lambda si: (si, 0, 0, 0)),
            pl.BlockSpec((ns, RET_HEADS, RET_DK, RET_DV), lambda si: (si, 0, 0, 0)),
            pl.BlockSpec((d, PROJ_W), const2),
            pl.BlockSpec((MIX_W, d), const2),
        ],
        out_shape=[
            jax.ShapeDtypeStruct((rows, d), F32),
            jax.ShapeDtypeStruct((nb, n_new, SWA_KV_HEADS, SWA_HD), F32),
            jax.ShapeDtypeStruct((nb, n_new, SWA_KV_HEADS, SWA_HD), F32),
            jax.ShapeDtypeStruct((nb, RET_HEADS, RET_DK, RET_DV), F32),
            jax.ShapeDtypeStruct((d, PROJ_W), BF16),
            jax.ShapeDtypeStruct((MIX_W, d), BF16),
        ],
        scratch_shapes=[
            pltpu.VMEM((rows, PROJ_W), F32),
            pltpu.VMEM((rows, MIX_W), BF16),
        ],
        compiler_params=pltpu.CompilerParams(
            dimension_semantics=("arbitrary",), vmem_limit_bytes=VMEM_LIMIT),
        name="sample_mixer",
    )(sinks, x.reshape(rows, d), tab, dmask, qdec, kdec, gt, cache_k, cache_v, state, w_in, gnw, w_out, ln1w, ln1b)


def _mlp_kernel(xp_ref, xs_ref, w_up_ref, w_down_ref, lnw_ref, lnb_ref, yp_ref, ys_ref, h_scr, *, prompt_steps):
    i = pl.program_id(0)

    def block(x_ref, y_ref):
        for r0 in range(0, x_ref.shape[0], MLP_UP_ROWS):
            n = min(MLP_UP_ROWS, x_ref.shape[0] - r0)
            h = _dot(x_ref[r0:r0 + n, :].astype(BF16), w_up_ref[...])
            h_scr[0:n, :] = jnp.square(jnp.maximum(h, 0.0)).astype(BF16)
            for r in range(0, n, MLP_LN_ROWS):
                rows = slice(r0 + r, r0 + r + MLP_LN_ROWS)
                z = ALPHA * x_ref[rows, :] + _dot(h_scr[r:r + MLP_LN_ROWS, :], w_down_ref[...])
                y_ref[rows, :] = _layer_norm(z, lnw_ref[...], lnb_ref[...])

    @pl.when(i < prompt_steps)
    def _():
        block(xp_ref, yp_ref)

    @pl.when(i == prompt_steps)
    def _():
        block(xs_ref, ys_ref)


def _mlp(xp, xs, w_up, w_down, lnw, lnb):
    rows, d = xp.shape
    rows_s = xs.shape[0]
    steps = rows // MLP_TILE
    const2 = lambda i: (0, 0)
    prompt_tile = lambda i: (jnp.minimum(i, steps - 1), 0)
    resident = dict(pipeline_mode=pl.Buffered(1))
    return pl.pallas_call(
        functools.partial(_mlp_kernel, prompt_steps=steps),
        grid=(steps + 1,),
        in_specs=[
            pl.BlockSpec((MLP_TILE, d), prompt_tile),
            pl.BlockSpec((rows_s, d), const2, **resident),
            pl.BlockSpec((d, D_FF), const2, **resident),
            pl.BlockSpec((D_FF, d), const2, **resident),
            pl.BlockSpec((1, d), const2),
            pl.BlockSpec((1, d), const2),
        ],
        out_specs=[
            pl.BlockSpec((MLP_TILE, d), prompt_tile),
            pl.BlockSpec((rows_s, d), const2),
        ],
        out_shape=[
            jax.ShapeDtypeStruct((rows, d), F32),
            jax.ShapeDtypeStruct((rows_s, d), F32),
        ],
        scratch_shapes=[pltpu.VMEM((MLP_UP_ROWS, D_FF), BF16)],
        compiler_params=pltpu.CompilerParams(
            dimension_semantics=("arbitrary",), vmem_limit_bytes=VMEM_LIMIT),
        name="mlp",
    )(xp, xs, w_up, w_down, lnw, lnb)


def _rope_tables(pos):
    posf = np.asarray(pos, np.float64)[:, None]
    half_r = RET_DK // 2
    ang = posf * (RET_THETA ** (-np.arange(half_r, dtype=np.float64) / half_r))[None, :]
    c, s = np.cos(ang), np.sin(ang)
    rc = np.concatenate([c, c], axis=-1)
    rs = np.concatenate([-s, s], axis=-1)
    half_s = SWA_ROT_DIM // 2
    ang = posf * (SWA_THETA ** (-np.arange(half_s, dtype=np.float64) / half_s))[None, :]
    c, s = np.cos(ang), np.sin(ang)
    t = posf.shape[0]
    rest = SWA_HD - SWA_ROT_DIM
    z = np.zeros((t, half_s))
    sc = np.tile(np.concatenate([c, c, np.ones((t, rest))], axis=-1), (1, LANES // SWA_HD))
    s1 = np.tile(np.concatenate([z, s, np.zeros((t, rest))], axis=-1), (1, LANES // SWA_HD))
    s2 = np.tile(np.concatenate([-s, z, np.zeros((t, rest))], axis=-1), (1, LANES // SWA_HD))
    return jnp.asarray(np.concatenate([rc, rs, sc, s1, s2], axis=-1).astype(np.float32))


def _decay_tables(t, copies=1):
    k_scale = RET_DK ** -0.5
    lg = np.log1p(-(2.0 ** (-5.0 - np.arange(RET_HEADS, dtype=np.float64))))
    idx = np.arange(t, dtype=np.float64)
    diff = idx[:, None] - idx[None, :]
    dmask = k_scale * np.where(diff >= 0, np.exp(lg[:, None, None] * np.maximum(diff, 0.0)), 0.0)
    qdec = np.repeat(np.exp(lg[None, :] * (idx[:, None] + 1.0)), RET_DV, axis=1)
    kdec = k_scale * np.repeat(np.exp(lg[None, :] * (t - 1.0 - idx[:, None])), RET_DV, axis=1)
    gt = np.repeat(np.exp(lg * t)[None, :], RET_DV, axis=1)
    if copies > 1:
        dmask = np.stack([np.kron(np.eye(copies), m) for m in dmask])
        qdec, kdec = np.tile(qdec, (copies, 1)), np.tile(kdec, (copies, 1))
    return tuple(jnp.asarray(a.astype(np.float32)) for a in (dmask, qdec, kdec, gt))


def kernel(x_prompt, x_sample, cache_swa_k, cache_swa_v, state_ret, w_in, ret_gn_w, swa_sinks,
           w_out, ln1_w, ln1_b, w_up, w_down, ln2_w, ln2_b):
    assert w_in.shape[0] == DEPTH == 1
    b, t, d = x_prompt.shape
    nb, n_new, _ = x_sample.shape
    win = cache_swa_k.shape[2]

    sinks = swa_sinks[0]

    tab_p = _rope_tables(np.arange(t))
    tab_s = _rope_tables(np.tile(PAST_LEN + np.arange(n_new), SAMPLE_STREAMS))
    dec_p = _decay_tables(PROMPT_TILE)
    dec_s = _decay_tables(n_new, SAMPLE_STREAMS)

    x1_s, k_s, v_s, r_s, w_in_b, w_out_b = _sample_mixer(
        x_sample, tab_s, *dec_s, jnp.transpose(cache_swa_k[0], (0, 2, 3, 1)), jnp.transpose(cache_swa_v[0], (0, 2, 3, 1)),
        state_ret[0], sinks, w_in[0], ret_gn_w, w_out[0], ln1_w, ln1_b)
    x1_p, k_p, v_p, r_p, w_up_b, w_down_b = _prompt_mixer(
        x_prompt, tab_p, *dec_p, sinks, w_in_b, ret_gn_w, w_out_b, ln1_w, ln1_b, w_up[0], w_down[0])
    y_p, y_s = _mlp(x1_p.reshape(b * t, d), x1_s, w_up_b, w_down_b, ln2_w, ln2_b)

    return (y_p.reshape(b, t, d), y_s.reshape(nb, n_new, d),
            jnp.transpose(k_p, (0, 3, 1, 2))[None],
            jnp.transpose(v_p, (0, 3, 1, 2))[None],
            r_p.reshape(DEPTH, b, RET_HEADS, RET_DK, RET_DV),
            k_s[None], v_s[None],
            r_s.reshape(DEPTH, nb, RET_HEADS, RET_DK, RET_DV))
```
